```python
import math
import jax
import jax.numpy as jnp
from jax import lax
import numpy as np

D_MODEL = 4096
BATCH = 2
SEQ = 8192
DEPTH = 2

HEAD_DIM = 128
ROT_DIM = HEAD_DIM // 4
ROPE_THETA = 500000.0

A_HEADS = D_MODEL // (4 * HEAD_DIM)
A_WIDTH = A_HEADS * 2 * HEAD_DIM
B_HEADS = D_MODEL // (2 * HEAD_DIM)
B_WIDTH = B_HEADS * HEAD_DIM
MOBA_BLOCK = 256
MOBA_TOPK = 3
MOBA_Q_CHUNK = 16
Q_BLOCK = 128

C_HEADS = D_MODEL // HEAD_DIM
C_KV_HEADS = C_HEADS // 4
C_Q_WIDTH = C_HEADS * HEAD_DIM
C_KV_WIDTH = C_KV_HEADS * HEAD_DIM
WINDOW = 128

N_EXPERTS = 16
N_GROUPS = 4
EXPERTS_PER_GROUP = N_EXPERTS // N_GROUPS
TOPK_GROUPS = 1
TOPK_EXPERTS = 2
D_FF_EXPERT = D_MODEL // 4

DN_ALPHA = float((2 * DEPTH) ** 0.25)
DN_BETA = float((8 * DEPTH) ** -0.25)
LN_EPS = 1e-5
RMS_EPS = 1e-5

kernel_name = "hybrid_diff_moba_swa_moe_deepnorm"


def layer_norm(x, g, b):
    xf = x.astype(jnp.float32)
    mu = jnp.mean(xf, axis=-1, keepdims=True)
    var = jnp.mean(jnp.square(xf - mu), axis=-1, keepdims=True)
    y = (xf - mu) * lax.rsqrt(var + LN_EPS) * g.astype(jnp.float32) + b.astype(jnp.float32)
    return y.astype(x.dtype)


def adaln_modulation(c, w_ada, b_ada):
    m = jax.nn.silu(c) @ w_ada + b_ada
    return jnp.split(m[:, None, :], 6, axis=-1)


def rope_tables(positions):
    inv_freq = ROPE_THETA ** (-jnp.arange(0, ROT_DIM, 2, dtype=jnp.float32) / ROT_DIM)
    ang = positions.astype(jnp.float32)[..., None] * inv_freq
    return jnp.cos(ang), jnp.sin(ang)


def apply_partial_rope(x, cos, sin):
    half = ROT_DIM // 2
    xr = x[..., :ROT_DIM].astype(jnp.float32)
    x1, x2 = xr[..., :half], xr[..., half:]
    cs = cos[:, :, None, :]
    sn = sin[:, :, None, :]
    rot = jnp.concatenate([x1 * cs - x2 * sn, x2 * cs + x1 * sn], axis=-1).astype(x.dtype)
    return jnp.concatenate([rot, x[..., ROT_DIM:]], axis=-1)


def diff_attention(q, k, v, lam, subln_g, lam_init):
    Bn, S, H = q.shape[0], q.shape[1], q.shape[2]
    scale = HEAD_DIM ** -0.5
    nq = S // Q_BLOCK
    qb = q.reshape(Bn, nq, Q_BLOCK, H, 2, HEAD_DIM).transpose(1, 0, 2, 3, 4, 5)
    kpos = jnp.arange(S)

    def block(args):
        qi, i = args
        s = jnp.einsum('bqhmd,bkhmd->bhmqk', qi, k).astype(jnp.float32) * scale
        qpos = i * Q_BLOCK + jnp.arange(Q_BLOCK)
        s = jnp.where(kpos[None, :] <= qpos[:, None], s, -jnp.inf)
        p = jax.nn.softmax(s, axis=-1)
        a = p[:, :, 0] - lam * p[:, :, 1]
        return jnp.einsum('bhqk,bkhe->bqhe', a.astype(v.dtype), v)

    o = lax.map(block, (qb, jnp.arange(nq)))
    o = o.transpose(1, 0, 2, 3, 4).reshape(Bn, S, H, 2 * HEAD_DIM)
    of = o.astype(jnp.float32)
    of = of * lax.rsqrt(jnp.mean(of * of, axis=-1, keepdims=True) + RMS_EPS) * subln_g.astype(jnp.float32)
    return (of * (1.0 - lam_init)).astype(q.dtype)


def moba_attention(q, k, v):
    Bn, S, H, D = q.shape
    scale = HEAD_DIM ** -0.5
    nb = -(-S // MOBA_BLOCK)
    pad = nb * MOBA_BLOCK - S
    kp = jnp.pad(k, ((0, 0), (0, pad), (0, 0), (0, 0)))
    vp = jnp.pad(v, ((0, 0), (0, pad), (0, 0), (0, 0)))
    kblk = kp.reshape(Bn, nb, MOBA_BLOCK, H, D).transpose(0, 3, 1, 2, 4)
    vblk = vp.reshape(Bn, nb, MOBA_BLOCK, H, D).transpose(0, 3, 1, 2, 4)
    kmean = jnp.mean(kblk.astype(jnp.float32), axis=3)
    ksel = min(MOBA_TOPK, max(nb - 1, 1))
    nc = S // MOBA_Q_CHUNK
    qc = q.reshape(Bn, nc, MOBA_Q_CHUNK, H, D).transpose(1, 0, 3, 2, 4)
    bi = jnp.arange(Bn)[:, None, None, None]
    hi = jnp.arange(H)[None, :, None, None]

    def chunk(args):
        qi, ci = args
        start = ci * MOBA_Q_CHUNK
        own = start // MOBA_BLOCK
        gate = jnp.einsum('bhqd,bhnd->bhqn', qi.astype(jnp.float32), kmean)
        gate = jnp.where(jnp.arange(nb) < own, gate, -jnp.inf)
        _, idx = lax.top_k(gate, ksel)
        sel_valid = jnp.arange(ksel) < own
        k_sel = kblk[bi, hi, idx]
        v_sel = vblk[bi, hi, idx]
        k_own = lax.dynamic_index_in_dim(kblk, own, axis=2, keepdims=False)
        v_own = lax.dynamic_index_in_dim(vblk, own, axis=2, keepdims=False)
        s_sel = jnp.einsum('bhqd,bhqnkd->bhqnk', qi, k_sel).astype(jnp.float32) * scale
        s_sel = jnp.where(sel_valid[:, None], s_sel, -jnp.inf)
        s_own = jnp.einsum('bhqd,bhkd->bhqk', qi, k_own).astype(jnp.float32) * scale
        qpos = start + jnp.arange(MOBA_Q_CHUNK)
        kpos = own * MOBA_BLOCK + jnp.arange(MOBA_BLOCK)
        s_own = jnp.where(kpos[None, :] <= qpos[:, None], s_own, -jnp.inf)
        s = jnp.concatenate([s_sel.reshape(Bn, H, MOBA_Q_CHUNK, ksel * MOBA_BLOCK), s_own], axis=-1)
        p = jax.nn.softmax(s, axis=-1).astype(v.dtype)
        p_sel = p[..., :ksel * MOBA_BLOCK].reshape(Bn, H, MOBA_Q_CHUNK, ksel, MOBA_BLOCK)
        p_own = p[..., ksel * MOBA_BLOCK:]
        return (jnp.einsum('bhqnk,bhqnkd->bhqd', p_sel, v_sel)
                + jnp.einsum('bhqk,bhkd->bhqd', p_own, v_own))

    o = lax.map(chunk, (qc, jnp.arange(nc)))
    return o.transpose(1, 0, 3, 2, 4).reshape(Bn, S, H, D)


def swa_sink_attention(q, k, v, sinks):
    Bn, S, Hq, D = q.shape
    Hkv = k.shape[2]
    G = Hq // Hkv
    scale = HEAD_DIM ** -0.5
    nblk = S // WINDOW
    qb = q.reshape(Bn, nblk, WINDOW, Hkv, G, D)
    kb = k.reshape(Bn, nblk, WINDOW, Hkv, D)
    vb = v.reshape(Bn, nblk, WINDOW, Hkv, D)
    pad_cfg = ((0, 0), (1, 0), (0, 0), (0, 0), (0, 0))
    kk = jnp.concatenate([jnp.pad(kb, pad_cfg)[:, :-1], kb], axis=2)
    vv = jnp.concatenate([jnp.pad(vb, pad_cfg)[:, :-1], vb], axis=2)
    s = jnp.einsum('bnqhgd,bnkhd->bnhgqk', qb, kk).astype(jnp.float32) * scale
    qi = jnp.arange(WINDOW)[:, None] + WINDOW
    ki = jnp.arange(2 * WINDOW)[None, :]
    rel = qi - ki
    band = (rel >= 0) & (rel < WINDOW)
    inside = (jnp.arange(nblk)[:, None, None] * WINDOW + ki[None] - WINDOW) >= 0
    mask = band[None] & inside
    s = jnp.where(mask[None, :, None, None], s, -jnp.inf)
    sink = sinks.astype(jnp.float32).reshape(Hkv, G)[None, None, :, :, None, None]
    m = jnp.maximum(jnp.max(s, axis=-1, keepdims=True), sink)
    p = jnp.exp(s - m)
    p = p / (jnp.sum(p, axis=-1, keepdims=True) + jnp.exp(sink - m))
    o = jnp.einsum('bnhgqk,bnkhd->bnqhgd', p.astype(v.dtype), vv)
    return o.reshape(Bn, S, Hq, D)


def diff_moba_mixer(h, cos, sin, w_in, lambda_q1, lambda_k1, lambda_q2, lambda_k2, subln_g, w_out, lam_init):
    Bn, S, _ = h.shape
    proj = h @ w_in
    splits = [A_WIDTH, 2 * A_WIDTH, 3 * A_WIDTH, 3 * A_WIDTH + B_WIDTH, 3 * A_WIDTH + 2 * B_WIDTH]
    aq, ak, av, bq, bk, bv = jnp.split(proj, splits, axis=-1)
    aq = apply_partial_rope(aq.reshape(Bn, S, 2 * A_HEADS, HEAD_DIM), cos, sin).reshape(Bn, S, A_HEADS, 2, HEAD_DIM)
    ak = apply_partial_rope(ak.reshape(Bn, S, 2 * A_HEADS, HEAD_DIM), cos, sin).reshape(Bn, S, A_HEADS, 2, HEAD_DIM)
    av = av.reshape(Bn, S, A_HEADS, 2 * HEAD_DIM)
    lam = (jnp.exp(jnp.sum(lambda_q1.astype(jnp.float32) * lambda_k1.astype(jnp.float32)))
           - jnp.exp(jnp.sum(lambda_q2.astype(jnp.float32) * lambda_k2.astype(jnp.float32))) + lam_init)
    o_a = diff_attention(aq, ak, av, lam, subln_g, lam_init).reshape(Bn, S, A_WIDTH)
    bq = apply_partial_rope(bq.reshape(Bn, S, B_HEADS, HEAD_DIM), cos, sin)
    bk = apply_partial_rope(bk.reshape(Bn, S, B_HEADS, HEAD_DIM), cos, sin)
    bv = bv.reshape(Bn, S, B_HEADS, HEAD_DIM)
    o_b = moba_attention(bq, bk, bv).reshape(Bn, S, B_WIDTH)
    return jnp.concatenate([o_a, o_b], axis=-1) @ w_out


def swa_mixer(h, cos, sin, w_in, b_in, sinks, w_out):
    Bn, S, _ = h.shape
    proj = h @ w_in + b_in
    q, k, v = jnp.split(proj, [C_Q_WIDTH, C_Q_WIDTH + C_KV_WIDTH], axis=-1)
    q = apply_partial_rope(q.reshape(Bn, S, C_HEADS, HEAD_DIM), cos, sin)
    k = apply_partial_rope(k.reshape(Bn, S, C_KV_HEADS, HEAD_DIM), cos, sin)
    v = v.reshape(Bn, S, C_KV_HEADS, HEAD_DIM)
    return swa_sink_attention(q, k, v, sinks).reshape(Bn, S, C_Q_WIDTH) @ w_out


def grouped_moe(h, router_w, router_bias, w_gate, w_up, w_down):
    Bn, S, D = h.shape
    t = h.reshape(Bn * S, D)
    scores = jax.nn.sigmoid((t @ router_w).astype(jnp.float32))
    biased = scores + router_bias.astype(jnp.float32)
    gscore = jnp.sum(lax.top_k(biased.reshape(-1, N_GROUPS, EXPERTS_PER_GROUP), 2)[0], axis=-1)
    _, gidx = lax.top_k(gscore, TOPK_GROUPS)
    gmask = jnp.any(gidx[..., None] == jnp.arange(N_GROUPS), axis=-2)
    emask = jnp.repeat(gmask, EXPERTS_PER_GROUP, axis=-1)
    _, eidx = lax.top_k(jnp.where(emask, biased, -jnp.inf), TOPK_EXPERTS)
    w = jnp.take_along_axis(scores, eidx, axis=-1)
    w = w / jnp.sum(w, axis=-1, keepdims=True)
    combine = jnp.sum(jax.nn.one_hot(eidx, N_EXPERTS, dtype=jnp.float32) * w[..., None], axis=1)
    out = jnp.zeros(t.shape, jnp.float32)
    for e in range(N_EXPERTS):
        he = jax.nn.silu(t @ w_gate[e]) * (t @ w_up[e])
        out = out + combine[:, e:e + 1] * (he @ w_down[e]).astype(jnp.float32)
    return out.astype(h.dtype).reshape(Bn, S, D)


def setup_inputs(seed: int = 0) -> dict:
    key = jax.random.key(seed)
    ks = iter(jax.random.split(key, 48))
    D = D_MODEL

    def nrm(shape, scale):
        return jax.random.normal(next(ks), shape, jnp.float32) * scale

    def gain(n):
        return 1.0 + nrm((n,), 0.02)

    x = nrm((BATCH, SEQ, D), 1.0)
    c = nrm((BATCH, D), 1.0)
    offsets = jax.random.randint(next(ks), (BATCH, 1), 0, 1024, dtype=jnp.int32)
    positions = (offsets + jnp.arange(SEQ, dtype=jnp.int32)[None, :]).astype(jnp.int32)
    router_w = nrm((D, N_EXPERTS), D ** -0.5)
    router_bias = nrm((N_EXPERTS,), 0.01)
    ada_scale = 0.5 * D ** -0.5

    def experts():
        return (nrm((N_EXPERTS, D, D_FF_EXPERT), D ** -0.5),
                nrm((N_EXPERTS, D, D_FF_EXPERT), D ** -0.5),
                nrm((N_EXPERTS, D_FF_EXPERT, D), D_FF_EXPERT ** -0.5 * DN_BETA))

    col0 = jnp.concatenate([jnp.ones((2 * A_WIDTH,), jnp.float32), jnp.full((A_WIDTH,), DN_BETA, jnp.float32),
                            jnp.ones((2 * B_WIDTH,), jnp.float32), jnp.full((B_WIDTH,), DN_BETA, jnp.float32)])
    l0_w_ada = nrm((D, 6 * D), ada_scale)
    l0_b_ada = nrm((6 * D,), 0.02)
    l0_w_in = nrm((D, 3 * A_WIDTH + 3 * B_WIDTH), D ** -0.5) * col0
    l0_lambda_q1 = nrm((HEAD_DIM,), 0.1)
    l0_lambda_k1 = nrm((HEAD_DIM,), 0.1)
    l0_lambda_q2 = nrm((HEAD_DIM,), 0.1)
    l0_lambda_k2 = nrm((HEAD_DIM,), 0.1)
    l0_subln_g = gain(2 * HEAD_DIM)
    l0_w_out = nrm((A_WIDTH + B_WIDTH, D), (A_WIDTH + B_WIDTH) ** -0.5 * DN_BETA)
    l0_ln1_g = gain(D)
    l0_ln1_b = nrm((D,), 0.02)
    l0_w_gate, l0_w_up, l0_w_down = experts()
    l0_ln2_g = gain(D)
    l0_ln2_b = nrm((D,), 0.02)
    col1 = jnp.concatenate([jnp.ones((C_Q_WIDTH + C_KV_WIDTH,), jnp.float32),
                            jnp.full((C_KV_WIDTH,), DN_BETA, jnp.float32)])
    l1_w_ada = nrm((D, 6 * D), ada_scale)
    l1_b_ada = nrm((6 * D,), 0.02)
    l1_w_in = nrm((D, C_Q_WIDTH + 2 * C_KV_WIDTH), D ** -0.5) * col1
    l1_b_in = nrm((C_Q_WIDTH + 2 * C_KV_WIDTH,), 0.02)
    l1_sinks = nrm((C_HEADS,), 0.5)
    l1_w_out = nrm((C_Q_WIDTH, D), C_Q_WIDTH ** -0.5 * DN_BETA)
    l1_ln1_g = gain(D)
    l1_ln1_b = nrm((D,), 0.02)
    l1_w_gate, l1_w_up, l1_w_down = experts()
    l1_ln2_g = gain(D)
    l1_ln2_b = nrm((D,), 0.02)
    return {
        "x": x, "c": c, "positions": positions,
        "router_w": router_w, "router_bias": router_bias,
        "l0_w_ada": l0_w_ada, "l0_b_ada": l0_b_ada, "l0_w_in": l0_w_in,
        "l0_lambda_q1": l0_lambda_q1, "l0_lambda_k1": l0_lambda_k1,
        "l0_lambda_q2": l0_lambda_q2, "l0_lambda_k2": l0_lambda_k2,
        "l0_subln_g": l0_subln_g, "l0_w_out": l0_w_out,
        "l0_ln1_g": l0_ln1_g, "l0_ln1_b": l0_ln1_b,
        "l0_w_gate": l0_w_gate, "l0_w_up": l0_w_up, "l0_w_down": l0_w_down,
        "l0_ln2_g": l0_ln2_g, "l0_ln2_b": l0_ln2_b,
        "l1_w_ada": l1_w_ada, "l1_b_ada": l1_b_ada, "l1_w_in": l1_w_in, "l1_b_in": l1_b_in,
        "l1_sinks": l1_sinks, "l1_w_out": l1_w_out,
        "l1_ln1_g": l1_ln1_g, "l1_ln1_b": l1_ln1_b,
        "l1_w_gate": l1_w_gate, "l1_w_up": l1_w_up, "l1_w_down": l1_w_down,
        "l1_ln2_g": l1_ln2_g, "l1_ln2_b": l1_ln2_b,
    }


def reference(x, c, positions, router_w, router_bias,
              l0_w_ada, l0_b_ada, l0_w_in, l0_lambda_q1, l0_lambda_k1, l0_lambda_q2, l0_lambda_k2,
              l0_subln_g, l0_w_out, l0_ln1_g, l0_ln1_b, l0_w_gate, l0_w_up, l0_w_down, l0_ln2_g, l0_ln2_b,
              l1_w_ada, l1_b_ada, l1_w_in, l1_b_in, l1_sinks, l1_w_out, l1_ln1_g, l1_ln1_b,
              l1_w_gate, l1_w_up, l1_w_down, l1_ln2_g, l1_ln2_b):
    cos, sin = rope_tables(positions)
    ada = [(l0_w_ada, l0_b_ada), (l1_w_ada, l1_b_ada)]
    post = [(l0_ln1_g, l0_ln1_b, l0_ln2_g, l0_ln2_b), (l1_ln1_g, l1_ln1_b, l1_ln2_g, l1_ln2_b)]
    ffn = [(l0_w_gate, l0_w_up, l0_w_down), (l1_w_gate, l1_w_up, l1_w_down)]
    for layer in range(DEPTH):
        shift1, scale1, gate1, shift2, scale2, gate2 = adaln_modulation(c, ada[layer][0], ada[layer][1])
        ln1_g, ln1_b, ln2_g, ln2_b = post[layer]
        h = x * (1.0 + scale1) + shift1
        if layer % 2 == 0:
            lam_init = 0.8 - 0.6 * math.exp(-0.3 * layer)
            y = diff_moba_mixer(h, cos, sin, l0_w_in, l0_lambda_q1, l0_lambda_k1, l0_lambda_q2, l0_lambda_k2,
                                l0_subln_g, l0_w_out, lam_init)
        else:
            y = swa_mixer(h, cos, sin, l1_w_in, l1_b_in, l1_sinks, l1_w_out)
        x = layer_norm(DN_ALPHA * x + gate1 * y, ln1_g, ln1_b)
        h = x * (1.0 + scale2) + shift2
        y = grouped_moe(h, router_w, router_bias, ffn[layer][0], ffn[layer][1], ffn[layer][2])
        x = layer_norm(DN_ALPHA * x + gate2 * y, ln2_g, ln2_b)
    return x
```

```python
import functools
import math

import jax
import jax.numpy as jnp
from jax import lax
from jax.experimental import pallas as pl
from jax.experimental.pallas import tpu as pltpu

F32 = jnp.float32
BF16 = jnp.bfloat16
U32 = jnp.uint32

HEAD_DIM = 128
ROT_DIM = HEAD_DIM // 4
ROT_HALF = ROT_DIM // 2
ROPE_THETA = 500000.0
MOBA_BLOCK = 256
MOBA_TOPK = 3
WINDOW = 128
N_EXPERTS = 16
N_GROUPS = 4
EXPERTS_PER_GROUP = N_EXPERTS // N_GROUPS
DEPTH = 2
DN_ALPHA = float((2 * DEPTH) ** 0.25)
LN_EPS = 1e-5
RMS_EPS = 1e-5
ATTN_SCALE = HEAD_DIM ** -0.5
NEG = -1e30

LANES = 128
V7X_VMEM_BYTES = 64 * 1024 * 1024
VMEM_LIMIT = 56 * 1024 * 1024


def _cparams(semantics):
    return pltpu.CompilerParams(dimension_semantics=semantics, vmem_limit_bytes=VMEM_LIMIT)


def _tile(n, pref):
    t = min(n, pref)
    assert n % t == 0, (n, t)
    return t


def _nt_dot(a, b):
    return lax.dot_general(a, b, (((1,), (1,)), ((), ())), preferred_element_type=F32)


def _ada_kernel(c_ref, w_ref, b_ref, o_ref):
    c = c_ref[...]
    a = (c * jax.nn.sigmoid(c)).astype(BF16)
    o_ref[...] = jnp.dot(a, w_ref[...].astype(BF16), preferred_element_type=F32) + b_ref[...]


def ada_modulation(c, w_ada, b_ada):
    bn, d = c.shape
    n = w_ada.shape[1]
    rows = 8
    cp = jnp.zeros((rows, d), F32).at[:bn].set(c)
    tn = _tile(n, 512)
    out = pl.pallas_call(
        _ada_kernel,
        grid=(n // tn,),
        in_specs=[pl.BlockSpec((rows, d), lambda j: (0, 0)),
                  pl.BlockSpec((d, tn), lambda j: (0, j)),
                  pl.BlockSpec((1, tn), lambda j: (0, j))],
        out_specs=pl.BlockSpec((rows, tn), lambda j: (0, j)),
        out_shape=jax.ShapeDtypeStruct((rows, n), F32),
        compiler_params=_cparams(("arbitrary",)),
        name="ada_modulation",
    )(cp, w_ada, b_ada.reshape(1, n))
    return out[:bn].reshape(bn, 6, 1, d)


def _modulate_kernel(x_ref, sc_ref, sh_ref, o_ref):
    o_ref[...] = (x_ref[...] * (1.0 + sc_ref[...]) + sh_ref[...]).astype(o_ref.dtype)


def modulate(x2, mod, which_shift, which_scale, seq):
    t, d = x2.shape
    tm = _tile(seq, 512)

    def vec(which):
        return pl.BlockSpec((None, None, 1, d), lambda i: ((i * tm) // seq, which, 0, 0))

    return pl.pallas_call(
        _modulate_kernel,
        grid=(t // tm,),
        in_specs=[pl.BlockSpec((tm, d), lambda i: (i, 0)), vec(which_scale), vec(which_shift)],
        out_specs=pl.BlockSpec((tm, d), lambda i: (i, 0)),
        out_shape=jax.ShapeDtypeStruct((t, d), BF16),
        compiler_params=_cparams(("arbitrary",)),
        name="modulate",
    )(x2, mod, mod)


def _proj_kernel(flag_ref, a_ref, w_ref, b_ref, c_ref, s1_ref, s2_ref, o_ref):
    j = pl.program_id(1)
    acc = jnp.dot(a_ref[...], w_ref[...], preferred_element_type=F32) + b_ref[...]
    tn = acc.shape[1]

    @pl.when(flag_ref[j] == 0)
    def _():
        o_ref[...] = acc.astype(o_ref.dtype)

    @pl.when(flag_ref[j] != 0)
    def _():
        cm, s1, s2 = c_ref[...], s1_ref[...], s2_ref[...]
        for hh in range(tn // HEAD_DIM):
            sl = slice(hh * HEAD_DIM, (hh + 1) * HEAD_DIM)
            xh = acc[:, sl]
            up = pltpu.roll(xh, HEAD_DIM - ROT_HALF, axis=1)
            dn = pltpu.roll(xh, ROT_HALF, axis=1)
            o_ref[:, sl] = (xh * cm + up * s1 + dn * s2).astype(o_ref.dtype)


def projection(a, w, bias, rope_flags, rope_tabs, out_dtype, tm_pref=512, tn_pref=512):
    m, k = a.shape
    n = w.shape[1]
    tm = _tile(m, tm_pref)
    tn = _tile(n, tn_pref)
    assert rope_flags.shape == (n // tn,)
    cm, s1, s2 = rope_tabs
    tab = pl.BlockSpec((tm, HEAD_DIM), lambda i, j, f: (i, 0))
    return pl.pallas_call(
        _proj_kernel,
        grid_spec=pltpu.PrefetchScalarGridSpec(
            num_scalar_prefetch=1,
            grid=(m // tm, n // tn),
            in_specs=[pl.BlockSpec((tm, k), lambda i, j, f: (i, 0)),
                      pl.BlockSpec((k, tn), lambda i, j, f: (0, j)),
                      pl.BlockSpec((1, tn), lambda i, j, f: (0, j)),
                      tab, tab, tab],
            out_specs=pl.BlockSpec((tm, tn), lambda i, j, f: (i, j)),
        ),
        out_shape=jax.ShapeDtypeStruct((m, n), out_dtype),
        compiler_params=_cparams(("arbitrary", "arbitrary")),
        name="projection",
    )(rope_flags, a, w, bias.reshape(1, n).astype(F32), cm, s1, s2)


def rope_tables(positions):
    inv_freq = ROPE_THETA ** (-jnp.arange(0, ROT_DIM, 2, dtype=F32) / ROT_DIM)
    ang = positions.reshape(-1).astype(F32)[:, None] * inv_freq
    cos, sin = jnp.cos(ang), jnp.sin(ang)
    t = ang.shape[0]
    rest = HEAD_DIM - ROT_DIM
    cm = jnp.concatenate([cos, cos, jnp.ones((t, rest), F32)], axis=1)
    s1 = jnp.concatenate([-sin, jnp.zeros((t, HEAD_DIM - ROT_HALF), F32)], axis=1)
    s2 = jnp.concatenate([jnp.zeros((t, ROT_HALF), F32), sin, jnp.zeros((t, rest), F32)], axis=1)
    return cm, s1, s2


def _online_softmax_step(s, v, m_ref, l_ref, acc_ref):
    m_prev = m_ref[...]
    m_new = jnp.maximum(m_prev, jnp.max(s, axis=1, keepdims=True))
    alpha = jnp.exp(m_prev - m_new)
    p = jnp.exp(s - m_new)
    l_ref[...] = alpha * l_ref[...] + jnp.sum(p, axis=1, keepdims=True)
    acc_ref[...] = alpha * acc_ref[...] + jnp.dot(p.astype(v.dtype), v, preferred_element_type=F32)
    m_ref[...] = m_new


def _diff_attn_kernel(lam_ref, q_ref, k_ref, v_ref, g_ref, o_ref, m_ref, l_ref, acc_ref,
                      *, tq, tk, lam_init):
    qi = pl.program_id(2)
    ki = pl.program_id(3)
    last_k = ((qi + 1) * tq - 1) // tk

    @pl.when(ki == 0)
    def _():
        m_ref[...] = jnp.full(m_ref.shape, NEG, F32)
        l_ref[...] = jnp.zeros(l_ref.shape, F32)
        acc_ref[...] = jnp.zeros(acc_ref.shape, F32)

    def step(masked):
        v = v_ref[...]
        for mp in range(2):
            sl = slice(mp * HEAD_DIM, (mp + 1) * HEAD_DIM)
            s = _nt_dot(q_ref[:, sl], k_ref[:, sl]) * ATTN_SCALE
            if masked:
                qpos = qi * tq + lax.broadcasted_iota(jnp.int32, (tq, tk), 0)
                kpos = ki * tk + lax.broadcasted_iota(jnp.int32, (tq, tk), 1)
                s = jnp.where(kpos <= qpos, s, NEG)
            _online_softmax_step(s, v, m_ref.at[mp], l_ref.at[mp], acc_ref.at[mp])

    fully_visible = (ki + 1) * tk - 1 <= qi * tq

    @pl.when(jnp.logical_and(ki <= last_k, fully_visible))
    def _():
        step(False)

    @pl.when(jnp.logical_and(ki <= last_k, jnp.logical_not(fully_visible)))
    def _():
        step(True)

    @pl.when(ki == last_k)
    def _():
        lam_full = (jnp.exp(jnp.sum(lam_ref[0:1, :] * lam_ref[1:2, :], axis=1, keepdims=True))
                    - jnp.exp(jnp.sum(lam_ref[2:3, :] * lam_ref[3:4, :], axis=1, keepdims=True)) + lam_init)
        o = acc_ref[0] / l_ref[0] - lam_full * (acc_ref[1] / l_ref[1])
        o = o * lax.rsqrt(jnp.mean(o * o, axis=1, keepdims=True) + RMS_EPS) * g_ref[...]
        o_ref[...] = (o * (1.0 - lam_init)).astype(o_ref.dtype)


def diff_attention(proj, lam_vecs, subln_g, *, batch, seq, n_heads, q_col, k_col, v_col, lam_init):
    t = proj.shape[0]
    width = 2 * HEAD_DIM
    tq = _tile(seq, 1024)
    tk = _tile(seq, 512)
    nq, nk = seq // tq, seq // tk

    def kv_map(col):
        def f(b, h, qi, ki):
            last = ((qi + 1) * tq - 1) // tk
            return (b * nk + jnp.minimum(ki, last), col + h)
        return f

    kern = functools.partial(_diff_attn_kernel, tq=tq, tk=tk, lam_init=lam_init)
    return pl.pallas_call(
        kern,
        grid=(batch, n_heads, nq, nk),
        in_specs=[pl.BlockSpec((4, HEAD_DIM), lambda b, h, qi, ki: (0, 0)),
                  pl.BlockSpec((tq, width), lambda b, h, qi, ki: (b * nq + qi, q_col + h)),
                  pl.BlockSpec((tk, width), kv_map(k_col)),
                  pl.BlockSpec((tk, width), kv_map(v_col)),
                  pl.BlockSpec((1, width), lambda b, h, qi, ki: (0, 0))],
        out_specs=pl.BlockSpec((tq, width), lambda b, h, qi, ki: (b * nq + qi, h)),
        out_shape=jax.ShapeDtypeStruct((t, n_heads * width), BF16),
        scratch_shapes=[pltpu.VMEM((2, tq, 1), F32), pltpu.VMEM((2, tq, 1), F32),
                        pltpu.VMEM((2, tq, width), F32)],
        compiler_params=_cparams(("arbitrary", "arbitrary", "arbitrary", "arbitrary")),
        name="diff_attention",
    )(lam_vecs, proj, proj, proj, subln_g.reshape(1, width).astype(F32))


def _kmean_kernel(k_ref, o_ref, *, nb):
    k = k_ref[...].astype(F32)
    o_ref[...] = jnp.mean(k.reshape(nb, MOBA_BLOCK, HEAD_DIM), axis=1)


def moba_kmean(proj, *, batch, seq, n_heads, k_col):
    nb = seq // MOBA_BLOCK
    return pl.pallas_call(
        functools.partial(_kmean_kernel, nb=nb),
        grid=(batch, n_heads),
        in_specs=[pl.BlockSpec((seq, HEAD_DIM), lambda b, h: (b, k_col + h))],
        out_specs=pl.BlockSpec((None, None, nb, HEAD_DIM), lambda b, h: (b, h, 0, 0)),
        out_shape=jax.ShapeDtypeStruct((batch, n_heads, nb, HEAD_DIM), F32),
        compiler_params=_cparams(("arbitrary", "arbitrary")),
        name="moba_kmean",
    )(proj)


def _moba_kernel(q_ref, k_ref, v_ref, km_ref, o_ref, sel_ref, m_ref, l_ref, acc_ref, *, nb):
    qi = pl.program_id(2)
    ki = pl.program_id(3)
    tq = q_ref.shape[0]
    tk = k_ref.shape[0]

    @pl.when(ki == 0)
    def _():
        m_ref[...] = jnp.full(m_ref.shape, NEG, F32)
        l_ref[...] = jnp.zeros(l_ref.shape, F32)
        acc_ref[...] = jnp.zeros(acc_ref.shape, F32)
        km = km_ref[...]
        km_hi = km.astype(BF16)
        km_lo = (km - km_hi.astype(F32)).astype(BF16)
        q = q_ref[...]
        gate = _nt_dot(km_hi, q) + _nt_dot(km_lo, q)
        blk = lax.broadcasted_iota(jnp.int32, (nb, tq), 0)
        past = blk < qi
        rank = jnp.zeros((nb, tq), jnp.int32)
        for jp in range(nb):
            gj = gate[jp:jp + 1, :]
            before = jnp.logical_or(gj > gate, jnp.logical_and(gj == gate, blk > jp))
            rank = rank + jnp.where(before, jnp.where(jp < qi, 1, 0), 0)
        sel = jnp.where(jnp.logical_and(past, rank < MOBA_TOPK), 1.0, 0.0)
        if nb < LANES:
            sel = jnp.concatenate([sel, jnp.zeros((LANES - nb, tq), F32)], axis=0)
        sel_ref[...] = sel.T

    def scores():
        return _nt_dot(q_ref[...], k_ref[...]) * ATTN_SCALE

    @pl.when(ki == 0)
    def _():
        qpos = lax.broadcasted_iota(jnp.int32, (tq, tk), 0)
        kpos = lax.broadcasted_iota(jnp.int32, (tq, tk), 1)
        s = jnp.where(kpos <= qpos, scores(), NEG)
        _online_softmax_step(s, v_ref[...], m_ref, l_ref, acc_ref)

    @pl.when(jnp.logical_and(ki > 0, ki <= qi))
    def _():
        j = ki - 1
        lane = lax.broadcasted_iota(jnp.int32, (tq, LANES), 1)
        col = jnp.sum(jnp.where(lane == j, sel_ref[...], 0.0), axis=1, keepdims=True)
        s = jnp.where(col > 0.5, scores(), NEG)
        _online_softmax_step(s, v_ref[...], m_ref, l_ref, acc_ref)

    @pl.when(ki == qi)
    def _():
        o_ref[...] = (acc_ref[...] / l_ref[...]).astype(o_ref.dtype)


def moba_attention(proj, kmean, *, batch, seq, n_heads, q_col, k_col, v_col):
    t = proj.shape[0]
    nb = seq // MOBA_BLOCK
    assert nb <= LANES
    tq = MOBA_BLOCK

    def kv_map(col):
        def f(b, h, qi, ki):
            j = jnp.where(ki == 0, qi, jnp.maximum(jnp.minimum(ki, qi) - 1, 0))
            return (b * nb + j, col + h)
        return f

    return pl.pallas_call(
        functools.partial(_moba_kernel, nb=nb),
        grid=(batch, n_heads, nb, nb),
        in_specs=[pl.BlockSpec((tq, HEAD_DIM), lambda b, h, qi, ki: (b * nb + qi, q_col + h)),
                  pl.BlockSpec((tq, HEAD_DIM), kv_map(k_col)),
                  pl.BlockSpec((tq, HEAD_DIM), kv_map(v_col)),
                  pl.BlockSpec((None, None, nb, HEAD_DIM), lambda b, h, qi, ki: (b, h, 0, 0))],
        out_specs=pl.BlockSpec((tq, HEAD_DIM), lambda b, h, qi, ki: (b * nb + qi, h)),
        out_shape=jax.ShapeDtypeStruct((t, n_heads * HEAD_DIM), BF16),
        scratch_shapes=[pltpu.VMEM((tq, LANES), F32), pltpu.VMEM((tq, 1), F32),
                        pltpu.VMEM((tq, 1), F32), pltpu.VMEM((tq, HEAD_DIM), F32)],
        compiler_params=_cparams(("arbitrary", "arbitrary", "arbitrary", "arbitrary")),
        name="moba_attention",
    )(proj, proj, proj, kmean)


def _swa_kernel(sink_ref, q_ref, kc_ref, kp_ref, vc_ref, vp_ref, o_ref, *, group, nsub):
    g = pl.program_id(1)
    i = pl.program_id(2)
    w = WINDOW
    rows = group * w
    qrow = lax.broadcasted_iota(jnp.int32, (rows, 2 * w), 0) % w + w
    kcol = lax.broadcasted_iota(jnp.int32, (rows, 2 * w), 1)
    rel = qrow - kcol
    band = jnp.logical_and(rel >= 0, rel < w)
    head_of_row = lax.broadcasted_iota(jnp.int32, (rows, 1), 0) // w
    sink = jnp.zeros((rows, 1), F32)
    for jh in range(group):
        sink = jnp.where(head_of_row == jh, sink_ref[g * group + jh], sink)

    for n in range(nsub):
        qs = jnp.concatenate(
            [q_ref[n * w:(n + 1) * w, jh * HEAD_DIM:(jh + 1) * HEAD_DIM] for jh in range(group)], axis=0)
        if n == 0:
            kk = jnp.concatenate([kp_ref[...], kc_ref[0:w, :]], axis=0)
            vv = jnp.concatenate([vp_ref[...], vc_ref[0:w, :]], axis=0)
            mask = jnp.logical_and(band, kcol >= jnp.where(i > 0, 0, w))
        else:
            kk = kc_ref[(n - 1) * w:(n + 1) * w, :]
            vv = vc_ref[(n - 1) * w:(n + 1) * w, :]
            mask = band
        s = jnp.where(mask, _nt_dot(qs, kk) * ATTN_SCALE, NEG)
        m = jnp.maximum(jnp.max(s, axis=1, keepdims=True), sink)
        p = jnp.exp(s - m)
        denom = jnp.sum(p, axis=1, keepdims=True) + jnp.exp(sink - m)
        o = jnp.dot(p.astype(vv.dtype), vv, preferred_element_type=F32) / denom
        for jh in range(group):
            o_ref[n * w:(n + 1) * w, jh * HEAD_DIM:(jh + 1) * HEAD_DIM] = (
                o[jh * w:(jh + 1) * w, :].astype(o_ref.dtype))


def swa_attention(proj, sinks, *, batch, seq, n_q_heads, n_kv_heads, k_col, v_col):
    t = proj.shape[0]
    group = n_q_heads // n_kv_heads
    tq = _tile(seq, 4 * WINDOW)
    nsub = tq // WINDOW
    nq = seq // tq
    gw = group * HEAD_DIM

    def prev_map(col):
        def f(b, g, i):
            return (b * (seq // WINDOW) + jnp.maximum(i * nsub - 1, 0), col + g)
        return f

    return pl.pallas_call(
        functools.partial(_swa_kernel, group=group, nsub=nsub),
        grid_spec=pltpu.PrefetchScalarGridSpec(
            num_scalar_prefetch=1,
            grid=(batch, n_kv_heads, nq),
            in_specs=[pl.BlockSpec((tq, gw), lambda b, g, i, s: (b * nq + i, g)),
                      pl.BlockSpec((tq, HEAD_DIM), lambda b, g, i, s: (b * nq + i, k_col + g)),
                      pl.BlockSpec((WINDOW, HEAD_DIM), lambda b, g, i, s: prev_map(k_col)(b, g, i)),
                      pl.BlockSpec((tq, HEAD_DIM), lambda b, g, i, s: (b * nq + i, v_col + g)),
                      pl.BlockSpec((WINDOW, HEAD_DIM), lambda b, g, i, s: prev_map(v_col)(b, g, i))],
            out_specs=pl.BlockSpec((tq, gw), lambda b, g, i, s: (b * nq + i, g)),
        ),
        out_shape=jax.ShapeDtypeStruct((t, n_q_heads * HEAD_DIM), BF16),
        compiler_params=_cparams(("arbitrary", "arbitrary", "arbitrary")),
        name="swa_attention",
    )(sinks.astype(F32), proj, proj, proj, proj, proj)


def _layer_norm_rows(z, g, b):
    mu = jnp.mean(z, axis=1, keepdims=True)
    zc = z - mu
    var = jnp.mean(zc * zc, axis=1, keepdims=True)
    return zc * lax.rsqrt(var + LN_EPS) * g + b


def _pack_bf16_halves(h):
    half = h.shape[1] // 2
    lo = lax.bitcast_convert_type(h[:, :half].astype(BF16).astype(F32), U32)
    hi = lax.bitcast_convert_type(h[:, half:].astype(BF16).astype(F32), U32)
    return jnp.bitwise_or(jnp.right_shift(lo, jnp.uint32(16)), hi)


def _unpack_bf16_halves(w):
    lo = lax.bitcast_convert_type(jnp.left_shift(w, jnp.uint32(16)), F32).astype(BF16)
    hi = lax.bitcast_convert_type(jnp.bitwise_and(w, jnp.uint32(0xFFFF0000)), F32).astype(BF16)
    return lo, hi


def _route(scores, biased):
    e, tm = scores.shape
    epg = EXPERTS_PER_GROUP
    gs = []
    for gi in range(N_GROUPS):
        rows = [biased[gi * epg + r:gi * epg + r + 1, :] for r in range(epg)]
        best = None
        for a in range(epg):
            for b in range(a + 1, epg):
                pair = rows[a] + rows[b]
                best = pair if best is None else jnp.maximum(best, pair)
        gs.append(best)
    gsel = []
    for gi in range(N_GROUPS):
        ok = None
        for go in range(N_GROUPS):
            if go == gi:
                continue
            c = (gs[gi] > gs[go]) if go < gi else (gs[gi] >= gs[go])
            ok = c if ok is None else jnp.logical_and(ok, c)
        gsel.append(ok)
    sel_rows, comb_rows = [], []
    for gi in range(N_GROUPS):
        rows = [biased[gi * epg + r:gi * epg + r + 1, :] for r in range(epg)]
        for r in range(epg):
            rank = jnp.zeros((1, tm), jnp.int32)
            for o in range(epg):
                if o == r:
                    continue
                before = (rows[o] >= rows[r]) if o < r else (rows[o] > rows[r])
                rank = rank + jnp.where(before, 1, 0)
            s = jnp.logical_and(gsel[gi], rank < 2)
            sel_rows.append(jnp.where(s, 1.0, 0.0))
            comb_rows.append(jnp.where(s, scores[gi * epg + r:gi * epg + r + 1, :], 0.0))
    sel = jnp.concatenate(sel_rows, axis=0)
    comb = jnp.concatenate(comb_rows, axis=0)
    comb = comb / jnp.sum(comb, axis=0, keepdims=True)
    return sel, comb


def _ln_router_kernel(x_ref, y_ref, gate_ref, g_ref, b_ref, sc_ref, sh_ref, rwh_ref, rwl_ref, rb_ref,
                      x1_ref, hp_ref, sel_ref, comb_ref):
    z = DN_ALPHA * x_ref[...] + gate_ref[...] * y_ref[...]
    x1 = _layer_norm_rows(z, g_ref[...], b_ref[...])
    x1_ref[...] = x1
    h = x1 * (1.0 + sc_ref[...]) + sh_ref[...]
    hp_ref[...] = _pack_bf16_halves(h)
    h_hi = h.astype(BF16)
    h_lo = (h - h_hi.astype(F32)).astype(BF16)
    logits = _nt_dot(rwh_ref[...], h_hi) + _nt_dot(rwl_ref[...], h_hi) + _nt_dot(rwh_ref[...], h_lo)
    scores = jax.nn.sigmoid(logits)
    sel, comb = _route(scores, scores + rb_ref[...])
    sel_ref[...] = sel
    comb_ref[...] = comb


def ln_router(x2, y, mod, which_gate, which_scale, which_shift, ln_g, ln_b, router_w, router_bias, seq):
    t, d = x2.shape
    tm = _tile(seq, 256)
    e = router_w.shape[1]
    rwt = router_w.T.astype(F32)
    rw_hi = rwt.astype(BF16)
    rw_lo = (rwt - rw_hi.astype(F32)).astype(BF16)

    def vec(which):
        return pl.BlockSpec((None, None, 1, d), lambda i: ((i * tm) // seq, which, 0, 0))

    row = pl.BlockSpec((tm, d), lambda i: (i, 0))
    const = lambda shape: pl.BlockSpec(shape, lambda i: (0, 0))
    return pl.pallas_call(
        _ln_router_kernel,
        grid=(t // tm,),
        in_specs=[row, row, vec(which_gate), const((1, d)), const((1, d)), vec(which_scale), vec(which_shift),
                  const((e, d)), const((e, d)), const((e, 1))],
        out_specs=[row, pl.BlockSpec((tm, d // 2), lambda i: (i, 0)),
                   pl.BlockSpec((e, tm), lambda i: (0, i)), pl.BlockSpec((e, tm), lambda i: (0, i))],
        out_shape=[jax.ShapeDtypeStruct((t, d), F32), jax.ShapeDtypeStruct((t, d // 2), U32),
                   jax.ShapeDtypeStruct((e, t), F32), jax.ShapeDtypeStruct((e, t), F32)],
        compiler_params=_cparams(("arbitrary",)),
        name="ln_router",
    )(x2, y, mod, ln_g.reshape(1, d), ln_b.reshape(1, d), mod, mod, rw_hi, rw_lo,
      router_bias.reshape(e, 1).astype(F32))


def _gather_rows_kernel(idx_ref, src_ref, out_ref, sem, *, tg):
    i = pl.program_id(0)

    def row_copy(r):
        return pltpu.make_async_copy(src_ref.at[pl.ds(idx_ref[0, 0, r], 1)],
                                     out_ref.at[pl.ds(i * tg + r, 1)], sem)

    def start(r, c):
        row_copy(r).start()
        return c

    def wait(r, c):
        row_copy(r).wait()
        return c

    lax.fori_loop(0, tg, start, 0)
    lax.fori_loop(0, tg, wait, 0)


def gather_rows(src, idx, tg_pref=512):
    r = idx.shape[0]
    tg = _tile(r, tg_pref)
    return pl.pallas_call(
        functools.partial(_gather_rows_kernel, tg=tg),
        grid=(r // tg,),
        in_specs=[pl.BlockSpec((1, 1, tg), lambda i: (i, 0, 0), memory_space=pltpu.SMEM),
                  pl.BlockSpec(memory_space=pl.ANY)],
        out_specs=pl.BlockSpec(memory_space=pl.ANY),
        out_shape=jax.ShapeDtypeStruct((r, src.shape[1]), src.dtype),
        scratch_shapes=[pltpu.SemaphoreType.DMA(())],
        compiler_params=_cparams(("arbitrary",)),
        name="gather_rows",
    )(idx.reshape(r // tg, 1, tg), src)


def _moe_ffn_kernel(te_ref, nv_ref, xs_ref, wg_ref, wu_ref, wd_ref, rw_ref, ys_ref, xb_ref):
    i = pl.program_id(0)
    f = pl.program_id(1)
    half = xs_ref.shape[1]

    @pl.when(i < nv_ref[0])
    def _():
        @pl.when(f == 0)
        def _():
            lo, hi = _unpack_bf16_halves(xs_ref[...])
            xb_ref[:, :half] = lo
            xb_ref[:, half:] = hi

        xb = xb_ref[...]
        gt = jnp.dot(xb, wg_ref[...], preferred_element_type=F32)
        up = jnp.dot(xb, wu_ref[...], preferred_element_type=F32)
        hm = (gt * jax.nn.sigmoid(gt) * up).astype(BF16)
        contrib = jnp.dot(hm, wd_ref[...], preferred_element_type=F32) * rw_ref[...]

        @pl.when(f == 0)
        def _():
            ys_ref[...] = contrib

        @pl.when(f != 0)
        def _():
            ys_ref[...] += contrib

    @pl.when(jnp.logical_and(i >= nv_ref[0], f == 0))
    def _():
        ys_ref[...] = jnp.zeros(ys_ref.shape, ys_ref.dtype)


def moe_ffn(xs, w_gate, w_up, w_down, row_weight, tile_expert, n_valid, tm):
    r, half = xs.shape
    e, d, ff = w_gate.shape
    tf = _tile(ff, 256)
    nff = ff // tf
    n_tiles = r // tm

    def row_i(i, nv):
        return jnp.minimum(i, nv[0] - 1)

    def f_eff(i, f, nv):
        return jnp.where(i < nv[0], f, nff - 1)

    return pl.pallas_call(
        _moe_ffn_kernel,
        grid_spec=pltpu.PrefetchScalarGridSpec(
            num_scalar_prefetch=2,
            grid=(n_tiles, nff),
            in_specs=[pl.BlockSpec((tm, half), lambda i, f, te, nv: (row_i(i, nv), 0)),
                      pl.BlockSpec((None, d, tf), lambda i, f, te, nv: (te[i], 0, f_eff(i, f, nv))),
                      pl.BlockSpec((None, d, tf), lambda i, f, te, nv: (te[i], 0, f_eff(i, f, nv))),
                      pl.BlockSpec((None, tf, d), lambda i, f, te, nv: (te[i], f_eff(i, f, nv), 0)),
                      pl.BlockSpec((tm, 1), lambda i, f, te, nv: (row_i(i, nv), 0))],
            out_specs=pl.BlockSpec((tm, d), lambda i, f, te, nv: (i, 0)),
            scratch_shapes=[pltpu.VMEM((tm, d), BF16)],
        ),
        out_shape=jax.ShapeDtypeStruct((r, d), F32),
        compiler_params=_cparams(("arbitrary", "arbitrary")),
        name="moe_ffn",
    )(tile_expert, n_valid, xs, w_gate, w_up, w_down, row_weight)


def moe_dispatch(sel_t, comb_t, tm):
    e, t = sel_t.shape
    k = 2
    sel = sel_t.T > 0.5
    seli = sel.astype(jnp.int32)
    counts = jnp.sum(seli, axis=0)
    padded = ((counts + tm - 1) // tm) * tm
    ends = jnp.cumsum(padded)
    starts = ends - padded
    pos = jnp.cumsum(seli, axis=0) - seli
    dest_all = starts[None, :] + pos
    _, eidx = lax.top_k(seli, k)
    dest = jnp.take_along_axis(dest_all, eidx, axis=1)
    wts = jnp.take_along_axis(comb_t.T, eidx, axis=1)
    r_max = k * t + e * tm
    flat = dest.reshape(-1)
    row_token = jnp.zeros((r_max,), jnp.int32).at[flat].set(jnp.repeat(jnp.arange(t, dtype=jnp.int32), k))
    row_weight = jnp.zeros((r_max,), F32).at[flat].set(wts.reshape(-1))
    n_tiles = r_max // tm
    tile_start = jnp.arange(n_tiles, dtype=jnp.int32) * tm
    n_valid = (ends[-1] // tm).astype(jnp.int32)
    tile_expert = jnp.minimum(jnp.searchsorted(ends, tile_start, side="right"), e - 1).astype(jnp.int32)
    last_valid_expert = tile_expert[jnp.maximum(n_valid - 1, 0)]
    tile_expert = jnp.where(jnp.arange(n_tiles) < n_valid, tile_expert, last_valid_expert)
    return row_token, row_weight.reshape(r_max, 1), tile_expert, n_valid.reshape(1), dest


def _ln_combine_kernel(dest_ref, x_ref, ys_ref, gate_ref, g_ref, b_ref, sc_ref, sh_ref,
                       xo_ref, h_ref, buf_ref, sem, *, tm, emit_h):
    def row_copy(n):
        return pltpu.make_async_copy(ys_ref.at[pl.ds(dest_ref[0, 0, n], 1)],
                                     buf_ref.at[n % 2, pl.ds(n // 2, 1)], sem)

    def start(n, c):
        row_copy(n).start()
        return c

    def wait(n, c):
        row_copy(n).wait()
        return c

    lax.fori_loop(0, 2 * tm, start, 0)
    lax.fori_loop(0, 2 * tm, wait, 0)
    y = buf_ref[0] + buf_ref[1]
    z = DN_ALPHA * x_ref[...] + gate_ref[...] * y
    xo = _layer_norm_rows(z, g_ref[...], b_ref[...])
    xo_ref[...] = xo
    if emit_h:
        h_ref[...] = (xo * (1.0 + sc_ref[...]) + sh_ref[...]).astype(h_ref.dtype)
    else:
        h_ref[...] = jnp.zeros(h_ref.shape, h_ref.dtype)


def ln_combine(x1, ys, dest, mod, which_gate, ln_g, ln_b, next_mod, seq, emit_h):
    t, d = x1.shape
    tm = _tile(seq, 256)

    def vec(which):
        return pl.BlockSpec((None, None, 1, d), lambda i: ((i * tm) // seq, which, 0, 0))

    row = pl.BlockSpec((tm, d), lambda i: (i, 0))
    const = lambda shape: pl.BlockSpec(shape, lambda i: (0, 0))
    h_rows = tm if emit_h else 8
    h_spec = pl.BlockSpec((h_rows, d), (lambda i: (i, 0)) if emit_h else (lambda i: (0, 0)))
    xo, h = pl.pallas_call(
        functools.partial(_ln_combine_kernel, tm=tm, emit_h=emit_h),
        grid=(t // tm,),
        in_specs=[pl.BlockSpec((1, 1, 2 * tm), lambda i: (i, 0, 0), memory_space=pltpu.SMEM),
                  row, pl.BlockSpec(memory_space=pl.ANY), vec(which_gate), const((1, d)), const((1, d)),
                  pl.BlockSpec((None, None, 1, d), lambda i: ((i * tm) // seq, 1, 0, 0)),
                  pl.BlockSpec((None, None, 1, d), lambda i: ((i * tm) // seq, 0, 0, 0))],
        out_specs=[row, h_spec],
        out_shape=[jax.ShapeDtypeStruct((t, d), F32),
                   jax.ShapeDtypeStruct((t if emit_h else 8, d), BF16)],
        scratch_shapes=[pltpu.VMEM((2, tm, d), F32), pltpu.SemaphoreType.DMA(())],
        compiler_params=_cparams(("arbitrary",)),
        name="ln_combine",
    )(dest.reshape(t // tm, 1, 2 * tm), x1, ys, mod, ln_g.reshape(1, d), ln_b.reshape(1, d), next_mod, next_mod)
    return xo, h


MOE_TILE = 512


def _moe_block(x1, hp, sel_t, comb_t, mod, ln_g, ln_b, w_gate, w_up, w_down, next_mod, seq, emit_h):
    tm = min(MOE_TILE, x1.shape[0])
    row_token, row_weight, tile_expert, n_valid, dest = moe_dispatch(sel_t, comb_t, tm)
    xs = gather_rows(hp, row_token)
    ys = moe_ffn(xs, w_gate.astype(BF16), w_up.astype(BF16), w_down.astype(BF16),
                 row_weight, tile_expert, n_valid, tm)
    return ln_combine(x1, ys, dest, mod, 5, ln_g, ln_b, next_mod, seq, emit_h)


def _flags(n, tn, rope_ranges):
    out = []
    for j in range(n // tn):
        c = j * tn
        out.append(1 if any(lo <= c < hi for lo, hi in rope_ranges) else 0)
    return jnp.asarray(out, jnp.int32)


def kernel(x, c, positions, router_w, router_bias, l0_w_ada, l0_b_ada, l0_w_in, l0_lambda_q1, l0_lambda_k1, l0_lambda_q2, l0_lambda_k2, l0_subln_g, l0_w_out, l0_ln1_g, l0_ln1_b, l0_w_gate, l0_w_up, l0_w_down, l0_ln2_g, l0_ln2_b, l1_w_ada, l1_b_ada, l1_w_in, l1_b_in, l1_sinks, l1_w_out, l1_ln1_g, l1_ln1_b, l1_w_gate, l1_w_up, l1_w_down, l1_ln2_g, l1_ln2_b):
    batch, seq, d = x.shape
    t = batch * seq
    a_heads = d // (4 * HEAD_DIM)
    a_width = a_heads * 2 * HEAD_DIM
    b_heads = d // (2 * HEAD_DIM)
    b_width = b_heads * HEAD_DIM
    c_heads = d // HEAD_DIM
    c_kv = c_heads // 4
    assert l0_w_in.shape[1] == 3 * a_width + 3 * b_width
    assert seq % MOBA_BLOCK == 0 and seq % WINDOW == 0

    x2 = x.reshape(t, d)
    tabs = rope_tables(positions)
    mod0 = ada_modulation(c, l0_w_ada, l0_b_ada)
    mod1 = ada_modulation(c, l1_w_ada, l1_b_ada)
    tn = 512 if d >= 2048 else 256

    h = modulate(x2, mod0, 0, 1, seq)
    n0 = l0_w_in.shape[1]
    rope0 = [(0, 2 * a_width), (3 * a_width, 3 * a_width + 2 * b_width)]
    proj = projection(h, l0_w_in.astype(BF16), jnp.zeros((n0,), F32), _flags(n0, tn, rope0), tabs, BF16,
                      tn_pref=tn)
    lam_vecs = jnp.stack([l0_lambda_q1, l0_lambda_k1, l0_lambda_q2, l0_lambda_k2]).astype(F32)
    lam_init = 0.8 - 0.6 * math.exp(-0.3 * 0)
    wa = 2 * HEAD_DIM
    o_a = diff_attention(proj, lam_vecs, l0_subln_g, batch=batch, seq=seq, n_heads=a_heads,
                         q_col=0, k_col=a_width // wa, v_col=2 * a_width // wa, lam_init=lam_init)
    bq0 = 3 * a_width // HEAD_DIM
    nbh = b_width // HEAD_DIM
    kmean = moba_kmean(proj, batch=batch, seq=seq, n_heads=b_heads, k_col=bq0 + nbh)
    o_b = moba_attention(proj, kmean, batch=batch, seq=seq, n_heads=b_heads,
                         q_col=bq0, k_col=bq0 + nbh, v_col=bq0 + 2 * nbh)
    o = jnp.concatenate([o_a, o_b], axis=1)
    y = projection(o, l0_w_out.astype(BF16), jnp.zeros((d,), F32), _flags(d, tn, []), tabs, F32, tn_pref=tn)
    x1, hp, sel_t, comb_t = ln_router(x2, y, mod0, 2, 4, 3, l0_ln1_g, l0_ln1_b, router_w, router_bias, seq)
    x2, h = _moe_block(x1, hp, sel_t, comb_t, mod0, l0_ln2_g, l0_ln2_b, l0_w_gate, l0_w_up, l0_w_down,
                       mod1, seq, True)

    n1 = l1_w_in.shape[1]
    cq = c_heads * HEAD_DIM
    ckv = c_kv * HEAD_DIM
    rope1 = [(0, cq + ckv)]
    proj = projection(h, l1_w_in.astype(BF16), l1_b_in, _flags(n1, tn, rope1), tabs, BF16, tn_pref=tn)
    o = swa_attention(proj, l1_sinks, batch=batch, seq=seq, n_q_heads=c_heads, n_kv_heads=c_kv,
                      k_col=cq // HEAD_DIM, v_col=(cq + ckv) // HEAD_DIM)
    y = projection(o, l1_w_out.astype(BF16), jnp.zeros((d,), F32), _flags(d, tn, []), tabs, F32, tn_pref=tn)
    x1, hp, sel_t, comb_t = ln_router(x2, y, mod1, 2, 4, 3, l1_ln1_g, l1_ln1_b, router_w, router_bias, seq)
    x2, _ = _moe_block(x1, hp, sel_t, comb_t, mod1, l1_ln2_g, l1_ln2_b, l1_w_gate, l1_w_up, l1_w_down,
                       mod1, seq, False)
    return x2.reshape(batch, seq, d)
```

```python
import functools
import math

import jax
import jax.numpy as jnp
from jax import lax
from jax.experimental import pallas as pl
from jax.experimental.pallas import tpu as pltpu

F32 = jnp.float32
BF16 = jnp.bfloat16
U32 = jnp.uint32

HEAD_DIM = 128
ROT_DIM = HEAD_DIM // 4
ROT_HALF = ROT_DIM // 2
ROPE_THETA = 500000.0
MOBA_BLOCK = 256
MOBA_TOPK = 3
WINDOW = 128
N_EXPERTS = 16
N_GROUPS = 4
EXPERTS_PER_GROUP = N_EXPERTS // N_GROUPS
DEPTH = 2
DN_ALPHA = float((2 * DEPTH) ** 0.25)
LN_EPS = 1e-5
RMS_EPS = 1e-5
ATTN_SCALE = HEAD_DIM ** -0.5
SCALE_LOG2E = ATTN_SCALE * math.log2(math.e)
NEG = -1e30

LANES = 128
V7X_VMEM_BYTES = 64 * 1024 * 1024
VMEM_LIMIT = 56 * 1024 * 1024


def _cparams(semantics):
    return pltpu.CompilerParams(dimension_semantics=semantics, vmem_limit_bytes=VMEM_LIMIT)


def _tile(n, pref):
    t = min(n, pref)
    assert n % t == 0, (n, t)
    return t


def _nt_dot(a, b):
    return lax.dot_general(a, b, (((1,), (1,)), ((), ())), preferred_element_type=F32)


def _ada_kernel(c_ref, w_ref, b_ref, o_ref):
    c = c_ref[...]
    a = (c * jax.nn.sigmoid(c)).astype(BF16)
    o_ref[...] = jnp.dot(a, w_ref[...].astype(BF16), preferred_element_type=F32) + b_ref[...]


def ada_modulation(c, w_ada, b_ada):
    bn, d = c.shape
    n = w_ada.shape[1]
    rows = 8
    cp = jnp.zeros((rows, d), F32).at[:bn].set(c)
    tn = _tile(n, 512)
    out = pl.pallas_call(
        _ada_kernel,
        grid=(n // tn,),
        in_specs=[pl.BlockSpec((rows, d), lambda j: (0, 0)),
                  pl.BlockSpec((d, tn), lambda j: (0, j)),
                  pl.BlockSpec((1, tn), lambda j: (0, j))],
        out_specs=pl.BlockSpec((rows, tn), lambda j: (0, j)),
        out_shape=jax.ShapeDtypeStruct((rows, n), F32),
        compiler_params=_cparams(("arbitrary",)),
        name="ada_modulation",
    )(cp, w_ada, b_ada.reshape(1, n))
    return out[:bn].reshape(bn, 6, 1, d)


def _modulate_kernel(x_ref, sc_ref, sh_ref, o_ref):
    o_ref[...] = (x_ref[...] * (1.0 + sc_ref[...]) + sh_ref[...]).astype(o_ref.dtype)


def modulate(x2, mod, which_shift, which_scale, seq):
    t, d = x2.shape
    tm = _tile(seq, 512)

    def vec(which):
        return pl.BlockSpec((None, None, 1, d), lambda i: ((i * tm) // seq, which, 0, 0))

    return pl.pallas_call(
        _modulate_kernel,
        grid=(t // tm,),
        in_specs=[pl.BlockSpec((tm, d), lambda i: (i, 0)), vec(which_scale), vec(which_shift)],
        out_specs=pl.BlockSpec((tm, d), lambda i: (i, 0)),
        out_shape=jax.ShapeDtypeStruct((t, d), BF16),
        compiler_params=_cparams(("arbitrary",)),
        name="modulate",
    )(x2, mod, mod)


def _proj_kernel(flag_ref, a_ref, w_ref, b_ref, c_ref, s1_ref, s2_ref, o_ref):
    j = pl.program_id(1)
    acc = jnp.dot(a_ref[...], w_ref[...], preferred_element_type=F32) + b_ref[...]
    tn = acc.shape[1]

    @pl.when(flag_ref[j] == 0)
    def _():
        o_ref[...] = acc.astype(o_ref.dtype)

    @pl.when(flag_ref[j] != 0)
    def _():
        cm, s1, s2 = c_ref[...], s1_ref[...], s2_ref[...]
        for hh in range(tn // HEAD_DIM):
            sl = slice(hh * HEAD_DIM, (hh + 1) * HEAD_DIM)
            xh = acc[:, sl]
            up = pltpu.roll(xh, HEAD_DIM - ROT_HALF, axis=1)
            dn = pltpu.roll(xh, ROT_HALF, axis=1)
            o_ref[:, sl] = (xh * cm + up * s1 + dn * s2).astype(o_ref.dtype)


def projection(a, w, bias, rope_flags, rope_tabs, out_dtype, tm_pref=512, tn_pref=512):
    m, k = a.shape
    n = w.shape[1]
    tm = _tile(m, tm_pref)
    tn = _tile(n, tn_pref)
    assert rope_flags.shape == (n // tn,)
    cm, s1, s2 = rope_tabs
    tab = pl.BlockSpec((tm, HEAD_DIM), lambda i, j, f: (i, 0))
    return pl.pallas_call(
        _proj_kernel,
        grid_spec=pltpu.PrefetchScalarGridSpec(
            num_scalar_prefetch=1,
            grid=(m // tm, n // tn),
            in_specs=[pl.BlockSpec((tm, k), lambda i, j, f: (i, 0)),
                      pl.BlockSpec((k, tn), lambda i, j, f: (0, j)),
                      pl.BlockSpec((1, tn), lambda i, j, f: (0, j)),
                      tab, tab, tab],
            out_specs=pl.BlockSpec((tm, tn), lambda i, j, f: (i, j)),
        ),
        out_shape=jax.ShapeDtypeStruct((m, n), out_dtype),
        compiler_params=_cparams(("arbitrary", "arbitrary")),
        name="projection",
    )(rope_flags, a, w, bias.reshape(1, n).astype(F32), cm, s1, s2)


def rope_tables(positions):
    inv_freq = ROPE_THETA ** (-jnp.arange(0, ROT_DIM, 2, dtype=F32) / ROT_DIM)
    ang = positions.reshape(-1).astype(F32)[:, None] * inv_freq
    cos, sin = jnp.cos(ang), jnp.sin(ang)
    t = ang.shape[0]
    rest = HEAD_DIM - ROT_DIM
    cm = jnp.concatenate([cos, cos, jnp.ones((t, rest), F32)], axis=1)
    s1 = jnp.concatenate([-sin, jnp.zeros((t, HEAD_DIM - ROT_HALF), F32)], axis=1)
    s2 = jnp.concatenate([jnp.zeros((t, ROT_HALF), F32), sin, jnp.zeros((t, rest), F32)], axis=1)
    return cm, s1, s2


def _flash_step(s, v, m_ref, l_ref, acc_ref):
    m_prev = m_ref[...]
    m_new = jnp.maximum(m_prev, jnp.max(s, axis=1, keepdims=True))
    alpha = jnp.exp2((m_prev - m_new) * SCALE_LOG2E)
    p = jnp.exp2((s - m_new) * SCALE_LOG2E)
    l_ref[...] = alpha * l_ref[...] + jnp.sum(p, axis=1, keepdims=True)
    acc_ref[...] = alpha * acc_ref[...] + jnp.dot(p.astype(v.dtype), v, preferred_element_type=F32)
    m_ref[...] = m_new


def _flash_init(m_ref, l_ref, acc_ref):
    m_ref[...] = jnp.full(m_ref.shape, NEG, F32)
    l_ref[...] = jnp.zeros(l_ref.shape, F32)
    acc_ref[...] = jnp.zeros(acc_ref.shape, F32)


def _causal_mask(tq, tk, key_offset):
    qpos = lax.broadcasted_iota(jnp.int32, (tq, tk), 0)
    kpos = lax.broadcasted_iota(jnp.int32, (tq, tk), 1) + key_offset
    return kpos <= qpos


def _diff_attn_kernel(lam_ref, q_ref, k_ref, v_ref, g_ref, o_ref, m_ref, l_ref, acc_ref,
                      *, tq, tk, lam_init):
    qi = pl.program_id(2)
    _flash_init(m_ref, l_ref, acc_ref)

    def block(j, mask):
        start = pl.multiple_of(j * tk, tk)
        vb = v_ref[pl.ds(start, tk), :]
        for mp in range(2):
            sl = slice(mp * HEAD_DIM, (mp + 1) * HEAD_DIM)
            s = _nt_dot(q_ref[:, sl], k_ref[pl.ds(start, tk), sl])
            if mask is not None:
                s = jnp.where(mask, s, NEG)
            _flash_step(s, vb, m_ref.at[mp], l_ref.at[mp], acc_ref.at[mp])

    n_full = qi * (tq // tk)

    def body(j, carry):
        block(j, None)
        return carry

    lax.fori_loop(0, n_full, body, 0)
    for jj in range(tq // tk):
        block(n_full + jj, _causal_mask(tq, tk, jj * tk))

    lam_full = (jnp.exp(jnp.sum(lam_ref[0:1, :] * lam_ref[1:2, :], axis=1, keepdims=True))
                - jnp.exp(jnp.sum(lam_ref[2:3, :] * lam_ref[3:4, :], axis=1, keepdims=True)) + lam_init)
    o = acc_ref[0] / l_ref[0] - lam_full * (acc_ref[1] / l_ref[1])
    o = o * lax.rsqrt(jnp.mean(o * o, axis=1, keepdims=True) + RMS_EPS) * g_ref[...]
    o_ref[...] = (o * (1.0 - lam_init)).astype(o_ref.dtype)


def diff_attention(proj, lam_vecs, subln_g, *, batch, seq, n_heads, q_col, k_col, v_col, lam_init):
    t = proj.shape[0]
    width = 2 * HEAD_DIM
    tq = _tile(seq, 512)
    tk = _tile(tq, 512)
    nq = seq // tq
    kern = functools.partial(_diff_attn_kernel, tq=tq, tk=tk, lam_init=lam_init)
    return pl.pallas_call(
        kern,
        grid=(batch, n_heads, nq),
        in_specs=[pl.BlockSpec((4, HEAD_DIM), lambda b, h, qi: (0, 0)),
                  pl.BlockSpec((tq, width), lambda b, h, qi: (b * nq + qi, q_col + h)),
                  pl.BlockSpec((seq, width), lambda b, h, qi: (b, k_col + h)),
                  pl.BlockSpec((seq, width), lambda b, h, qi: (b, v_col + h)),
                  pl.BlockSpec((1, width), lambda b, h, qi: (0, 0))],
        out_specs=pl.BlockSpec((tq, width), lambda b, h, qi: (b * nq + qi, h)),
        out_shape=jax.ShapeDtypeStruct((t, n_heads * width), BF16),
        scratch_shapes=[pltpu.VMEM((2, tq, 1), F32), pltpu.VMEM((2, tq, 1), F32),
                        pltpu.VMEM((2, tq, width), F32)],
        compiler_params=_cparams(("arbitrary", "arbitrary", "arbitrary")),
        name="diff_attention",
    )(lam_vecs, proj, proj, proj, subln_g.reshape(1, width).astype(F32))


def _kmean_kernel(k_ref, o_ref, *, nb):
    k = k_ref[...].astype(F32)
    o_ref[...] = jnp.mean(k.reshape(nb, MOBA_BLOCK, HEAD_DIM), axis=1)


def moba_kmean(proj, *, batch, seq, n_heads, k_col):
    nb = seq // MOBA_BLOCK
    return pl.pallas_call(
        functools.partial(_kmean_kernel, nb=nb),
        grid=(batch, n_heads),
        in_specs=[pl.BlockSpec((seq, HEAD_DIM), lambda b, h: (b, k_col + h))],
        out_specs=pl.BlockSpec((None, None, nb, HEAD_DIM), lambda b, h: (b, h, 0, 0)),
        out_shape=jax.ShapeDtypeStruct((batch, n_heads, nb, HEAD_DIM), F32),
        compiler_params=_cparams(("arbitrary", "arbitrary")),
        name="moba_kmean",
    )(proj)


def _moba_block_bias(q, km, qi, *, nb, tq):
    km_hi = km.astype(BF16)
    km_lo = (km - km_hi.astype(F32)).astype(BF16)
    gate = _nt_dot(km_hi, q) + _nt_dot(km_lo, q)
    blk = lax.broadcasted_iota(jnp.int32, (nb, tq), 0)
    own = qi * (tq // MOBA_BLOCK) + lax.broadcasted_iota(jnp.int32, (nb, tq), 1) // MOBA_BLOCK
    rank = jnp.zeros((nb, tq), jnp.int32)
    for jp in range(nb):
        gj = gate[jp:jp + 1, :]
        before = jnp.logical_or(gj > gate, jnp.logical_and(gj == gate, blk > jp))
        rank = rank + jnp.where(jnp.logical_and(before, own > jp), 1, 0)
    visible = jnp.logical_or(jnp.logical_and(blk < own, rank < MOBA_TOPK), blk == own)
    bias = jnp.where(visible, 0.0, NEG)
    if nb < LANES:
        bias = jnp.concatenate([bias, jnp.zeros((LANES - nb, tq), F32)], axis=0)
    return bias.T


def _moba_kernel(q_ref, k_ref, v_ref, km_ref, o_ref, kaug_ref, qaug_ref, m_ref, l_ref, acc_ref,
                 *, nb, tq):
    qi = pl.program_id(2)
    seq = k_ref.shape[0]

    @pl.when(qi == 0)
    def _():
        kaug_ref[:, :HEAD_DIM] = k_ref[...]
        key_blk = lax.broadcasted_iota(jnp.int32, (seq, LANES), 0) // MOBA_BLOCK
        lane = lax.broadcasted_iota(jnp.int32, (seq, LANES), 1)
        kaug_ref[:, HEAD_DIM:] = jnp.where(key_blk == lane, 1.0, 0.0).astype(BF16)

    _flash_init(m_ref, l_ref, acc_ref)
    q = q_ref[...]
    qaug_ref[:, :HEAD_DIM] = q
    qaug_ref[:, HEAD_DIM:] = _moba_block_bias(q, km_ref[...], qi, nb=nb, tq=tq).astype(BF16)

    def block(j, mask):
        start = pl.multiple_of(j * tq, tq)
        s = _nt_dot(qaug_ref[...], kaug_ref[pl.ds(start, tq), :])
        if mask is not None:
            s = jnp.where(mask, s, NEG)
        _flash_step(s, v_ref[pl.ds(start, tq), :], m_ref, l_ref, acc_ref)

    def body(j, carry):
        block(j, None)
        return carry

    lax.fori_loop(0, qi, body, 0)
    block(qi, _causal_mask(tq, tq, 0))
    o_ref[...] = (acc_ref[...] / l_ref[...]).astype(o_ref.dtype)


def moba_attention(proj, kmean, *, batch, seq, n_heads, q_col, k_col, v_col):
    t = proj.shape[0]
    nb = seq // MOBA_BLOCK
    assert nb <= LANES
    tq = _tile(seq, 2 * MOBA_BLOCK)
    assert tq % MOBA_BLOCK == 0
    nq = seq // tq
    return pl.pallas_call(
        functools.partial(_moba_kernel, nb=nb, tq=tq),
        grid=(batch, n_heads, nq),
        in_specs=[pl.BlockSpec((tq, HEAD_DIM), lambda b, h, qi: (b * nq + qi, q_col + h)),
                  pl.BlockSpec((seq, HEAD_DIM), lambda b, h, qi: (b, k_col + h)),
                  pl.BlockSpec((seq, HEAD_DIM), lambda b, h, qi: (b, v_col + h)),
                  pl.BlockSpec((None, None, nb, HEAD_DIM), lambda b, h, qi: (b, h, 0, 0))],
        out_specs=pl.BlockSpec((tq, HEAD_DIM), lambda b, h, qi: (b * nq + qi, h)),
        out_shape=jax.ShapeDtypeStruct((t, n_heads * HEAD_DIM), BF16),
        scratch_shapes=[pltpu.VMEM((seq, 2 * HEAD_DIM), BF16), pltpu.VMEM((tq, 2 * HEAD_DIM), BF16),
                        pltpu.VMEM((tq, 1), F32), pltpu.VMEM((tq, 1), F32), pltpu.VMEM((tq, HEAD_DIM), F32)],
        compiler_params=_cparams(("arbitrary", "arbitrary", "arbitrary")),
        name="moba_attention",
    )(proj, proj, proj, kmean)


def _swa_kernel(sink_ref, q_ref, kc_ref, kp_ref, vc_ref, vp_ref, o_ref, *, group, nsub):
    g = pl.program_id(1)
    i = pl.program_id(2)
    w = WINDOW
    rows = group * w
    qrow = lax.broadcasted_iota(jnp.int32, (rows, 2 * w), 0) % w + w
    kcol = lax.broadcasted_iota(jnp.int32, (rows, 2 * w), 1)
    rel = qrow - kcol
    band = jnp.logical_and(rel >= 0, rel < w)
    head_of_row = lax.broadcasted_iota(jnp.int32, (rows, 1), 0) // w
    sink = jnp.zeros((rows, 1), F32)
    for jh in range(group):
        sink = jnp.where(head_of_row == jh, sink_ref[g * group + jh], sink)

    for n in range(nsub):
        qs = jnp.concatenate(
            [q_ref[n * w:(n + 1) * w, jh * HEAD_DIM:(jh + 1) * HEAD_DIM] for jh in range(group)], axis=0)
        if n == 0:
            kk = jnp.concatenate([kp_ref[...], kc_ref[0:w, :]], axis=0)
            vv = jnp.concatenate([vp_ref[...], vc_ref[0:w, :]], axis=0)
            mask = jnp.logical_and(band, kcol >= jnp.where(i > 0, 0, w))
        else:
            kk = kc_ref[(n - 1) * w:(n + 1) * w, :]
            vv = vc_ref[(n - 1) * w:(n + 1) * w, :]
            mask = band
        s = jnp.where(mask, _nt_dot(qs, kk) * ATTN_SCALE, NEG)
        m = jnp.maximum(jnp.max(s, axis=1, keepdims=True), sink)
        p = jnp.exp(s - m)
        denom = jnp.sum(p, axis=1, keepdims=True) + jnp.exp(sink - m)
        o = jnp.dot(p.astype(vv.dtype), vv, preferred_element_type=F32) / denom
        for jh in range(group):
            o_ref[n * w:(n + 1) * w, jh * HEAD_DIM:(jh + 1) * HEAD_DIM] = (
                o[jh * w:(jh + 1) * w, :].astype(o_ref.dtype))


def swa_attention(proj, sinks, *, batch, seq, n_q_heads, n_kv_heads, k_col, v_col):
    t = proj.shape[0]
    group = n_q_heads // n_kv_heads
    tq = _tile(seq, 4 * WINDOW)
    nsub = tq // WINDOW
    nq = seq // tq
    gw = group * HEAD_DIM

    def prev_map(col):
        def f(b, g, i):
            return (b * (seq // WINDOW) + jnp.maximum(i * nsub - 1, 0), col + g)
        return f

    return pl.pallas_call(
        functools.partial(_swa_kernel, group=group, nsub=nsub),
        grid_spec=pltpu.PrefetchScalarGridSpec(
            num_scalar_prefetch=1,
            grid=(batch, n_kv_heads, nq),
            in_specs=[pl.BlockSpec((tq, gw), lambda b, g, i, s: (b * nq + i, g)),
                      pl.BlockSpec((tq, HEAD_DIM), lambda b, g, i, s: (b * nq + i, k_col + g)),
                      pl.BlockSpec((WINDOW, HEAD_DIM), lambda b, g, i, s: prev_map(k_col)(b, g, i)),
                      pl.BlockSpec((tq, HEAD_DIM), lambda b, g, i, s: (b * nq + i, v_col + g)),
                      pl.BlockSpec((WINDOW, HEAD_DIM), lambda b, g, i, s: prev_map(v_col)(b, g, i))],
            out_specs=pl.BlockSpec((tq, gw), lambda b, g, i, s: (b * nq + i, g)),
        ),
        out_shape=jax.ShapeDtypeStruct((t, n_q_heads * HEAD_DIM), BF16),
        compiler_params=_cparams(("arbitrary", "arbitrary", "arbitrary")),
        name="swa_attention",
    )(sinks.astype(F32), proj, proj, proj, proj, proj)


def _layer_norm_rows(z, g, b):
    mu = jnp.mean(z, axis=1, keepdims=True)
    zc = z - mu
    var = jnp.mean(zc * zc, axis=1, keepdims=True)
    return zc * lax.rsqrt(var + LN_EPS) * g + b


def _pack_bf16_halves(h):
    half = h.shape[1] // 2
    lo = lax.bitcast_convert_type(h[:, :half].astype(BF16).astype(F32), U32)
    hi = lax.bitcast_convert_type(h[:, half:].astype(BF16).astype(F32), U32)
    return jnp.bitwise_or(jnp.right_shift(lo, jnp.uint32(16)), hi)


def _unpack_bf16_halves(w):
    lo = lax.bitcast_convert_type(jnp.left_shift(w, jnp.uint32(16)), F32).astype(BF16)
    hi = lax.bitcast_convert_type(jnp.bitwise_and(w, jnp.uint32(0xFFFF0000)), F32).astype(BF16)
    return lo, hi


def _route(scores, biased):
    e, tm = scores.shape
    epg = EXPERTS_PER_GROUP
    gs = []
    for gi in range(N_GROUPS):
        rows = [biased[gi * epg + r:gi * epg + r + 1, :] for r in range(epg)]
        best = None
        for a in range(epg):
            for b in range(a + 1, epg):
                pair = rows[a] + rows[b]
                best = pair if best is None else jnp.maximum(best, pair)
        gs.append(best)
    gsel = []
    for gi in range(N_GROUPS):
        ok = None
        for go in range(N_GROUPS):
            if go == gi:
                continue
            c = (gs[gi] > gs[go]) if go < gi else (gs[gi] >= gs[go])
            ok = c if ok is None else jnp.logical_and(ok, c)
        gsel.append(ok)
    sel_rows, comb_rows = [], []
    for gi in range(N_GROUPS):
        rows = [biased[gi * epg + r:gi * epg + r + 1, :] for r in range(epg)]
        for r in range(epg):
            rank = jnp.zeros((1, tm), jnp.int32)
            for o in range(epg):
                if o == r:
                    continue
                before = (rows[o] >= rows[r]) if o < r else (rows[o] > rows[r])
                rank = rank + jnp.where(before, 1, 0)
            s = jnp.logical_and(gsel[gi], rank < 2)
            sel_rows.append(jnp.where(s, 1.0, 0.0))
            comb_rows.append(jnp.where(s, scores[gi * epg + r:gi * epg + r + 1, :], 0.0))
    sel = jnp.concatenate(sel_rows, axis=0)
    comb = jnp.concatenate(comb_rows, axis=0)
    comb = comb / jnp.sum(comb, axis=0, keepdims=True)
    return sel, comb


def _ln_router_kernel(x_ref, y_ref, gate_ref, g_ref, b_ref, sc_ref, sh_ref, rwh_ref, rwl_ref, rb_ref,
                      x1_ref, hp_ref, sel_ref, comb_ref):
    z = DN_ALPHA * x_ref[...] + gate_ref[...] * y_ref[...]
    x1 = _layer_norm_rows(z, g_ref[...], b_ref[...])
    x1_ref[...] = x1
    h = x1 * (1.0 + sc_ref[...]) + sh_ref[...]
    hp_ref[...] = _pack_bf16_halves(h)
    h_hi = h.astype(BF16)
    h_lo = (h - h_hi.astype(F32)).astype(BF16)
    logits = _nt_dot(rwh_ref[...], h_hi) + _nt_dot(rwl_ref[...], h_hi) + _nt_dot(rwh_ref[...], h_lo)
    scores = jax.nn.sigmoid(logits)
    sel, comb = _route(scores, scores + rb_ref[...])
    sel_ref[...] = sel
    comb_ref[...] = comb


def ln_router(x2, y, mod, which_gate, which_scale, which_shift, ln_g, ln_b, router_w, router_bias, seq):
    t, d = x2.shape
    tm = _tile(seq, 256)
    e = router_w.shape[1]
    rwt = router_w.T.astype(F32)
    rw_hi = rwt.astype(BF16)
    rw_lo = (rwt - rw_hi.astype(F32)).astype(BF16)

    def vec(which):
        return pl.BlockSpec((None, None, 1, d), lambda i: ((i * tm) // seq, which, 0, 0))

    row = pl.BlockSpec((tm, d), lambda i: (i, 0))
    const = lambda shape: pl.BlockSpec(shape, lambda i: (0, 0))
    return pl.pallas_call(
        _ln_router_kernel,
        grid=(t // tm,),
        in_specs=[row, row, vec(which_gate), const((1, d)), const((1, d)), vec(which_scale), vec(which_shift),
                  const((e, d)), const((e, d)), const((e, 1))],
        out_specs=[row, pl.BlockSpec((tm, d // 2), lambda i: (i, 0)),
                   pl.BlockSpec((e, tm), lambda i: (0, i)), pl.BlockSpec((e, tm), lambda i: (0, i))],
        out_shape=[jax.ShapeDtypeStruct((t, d), F32), jax.ShapeDtypeStruct((t, d // 2), U32),
                   jax.ShapeDtypeStruct((e, t), F32), jax.ShapeDtypeStruct((e, t), F32)],
        compiler_params=_cparams(("arbitrary",)),
        name="ln_router",
    )(x2, y, mod, ln_g.reshape(1, d), ln_b.reshape(1, d), mod, mod, rw_hi, rw_lo,
      router_bias.reshape(e, 1).astype(F32))


DMA_LOOP_UNROLL = 8


def _gather_rows_kernel(idx_ref, src_ref, out_ref, sem, *, tg):
    def row_copy(r):
        return pltpu.make_async_copy(src_ref.at[pl.ds(idx_ref[0, 0, r], 1)], out_ref.at[pl.ds(r, 1)], sem)

    def start(r, c):
        row_copy(r).start()
        return c

    def wait(r, c):
        row_copy(r).wait()
        return c

    lax.fori_loop(0, tg, start, 0, unroll=DMA_LOOP_UNROLL)
    lax.fori_loop(0, tg, wait, 0, unroll=DMA_LOOP_UNROLL)


def gather_rows(src, idx, tg_pref=256):
    r = idx.shape[0]
    tg = _tile(r, tg_pref)
    return pl.pallas_call(
        functools.partial(_gather_rows_kernel, tg=tg),
        grid=(r // tg,),
        in_specs=[pl.BlockSpec((1, 1, tg), lambda i: (i, 0, 0), memory_space=pltpu.SMEM),
                  pl.BlockSpec(memory_space=pl.ANY)],
        out_specs=pl.BlockSpec((tg, src.shape[1]), lambda i: (i, 0)),
        out_shape=jax.ShapeDtypeStruct((r, src.shape[1]), src.dtype),
        scratch_shapes=[pltpu.SemaphoreType.DMA(())],
        compiler_params=_cparams(("arbitrary",)),
        name="gather_rows",
    )(idx.reshape(r // tg, 1, tg), src)


def _moe_ffn_kernel(te_ref, nv_ref, xs_ref, wg_ref, wu_ref, wd_ref, rw_ref, ys_ref, xb_ref):
    i = pl.program_id(0)
    f = pl.program_id(1)
    half = xs_ref.shape[1]

    @pl.when(i < nv_ref[0])
    def _():
        @pl.when(f == 0)
        def _():
            lo, hi = _unpack_bf16_halves(xs_ref[...])
            xb_ref[:, :half] = lo
            xb_ref[:, half:] = hi

        xb = xb_ref[...]
        gt = jnp.dot(xb, wg_ref[...], preferred_element_type=F32)
        up = jnp.dot(xb, wu_ref[...], preferred_element_type=F32)
        hm = (gt * jax.nn.sigmoid(gt) * up).astype(BF16)
        contrib = jnp.dot(hm, wd_ref[...], preferred_element_type=F32) * rw_ref[...]

        @pl.when(f == 0)
        def _():
            ys_ref[...] = contrib

        @pl.when(f != 0)
        def _():
            ys_ref[...] += contrib

    @pl.when(jnp.logical_and(i >= nv_ref[0], f == 0))
    def _():
        ys_ref[...] = jnp.zeros(ys_ref.shape, ys_ref.dtype)


def moe_ffn(xs, w_gate, w_up, w_down, row_weight, tile_expert, n_valid, tm):
    r, half = xs.shape
    e, d, ff = w_gate.shape
    tf = _tile(ff, 256)
    nff = ff // tf
    n_tiles = r // tm

    def row_i(i, nv):
        return jnp.minimum(i, nv[0] - 1)

    def f_eff(i, f, nv):
        return jnp.where(i < nv[0], f, nff - 1)

    return pl.pallas_call(
        _moe_ffn_kernel,
        grid_spec=pltpu.PrefetchScalarGridSpec(
            num_scalar_prefetch=2,
            grid=(n_tiles, nff),
            in_specs=[pl.BlockSpec((tm, half), lambda i, f, te, nv: (row_i(i, nv), 0)),
                      pl.BlockSpec((None, d, tf), lambda i, f, te, nv: (te[i], 0, f_eff(i, f, nv))),
                      pl.BlockSpec((None, d, tf), lambda i, f, te, nv: (te[i], 0, f_eff(i, f, nv))),
                      pl.BlockSpec((None, tf, d), lambda i, f, te, nv: (te[i], f_eff(i, f, nv), 0)),
                      pl.BlockSpec((tm, 1), lambda i, f, te, nv: (row_i(i, nv), 0))],
            out_specs=pl.BlockSpec((tm, d), lambda i, f, te, nv: (i, 0)),
            scratch_shapes=[pltpu.VMEM((tm, d), BF16)],
        ),
        out_shape=jax.ShapeDtypeStruct((r, d), F32),
        compiler_params=_cparams(("arbitrary", "arbitrary")),
        name="moe_ffn",
    )(tile_expert, n_valid, xs, w_gate, w_up, w_down, row_weight)


def moe_dispatch(sel_t, comb_t, tm):
    e, t = sel_t.shape
    k = 2
    sel = sel_t.T > 0.5
    seli = sel.astype(jnp.int32)
    counts = jnp.sum(seli, axis=0)
    padded = ((counts + tm - 1) // tm) * tm
    ends = jnp.cumsum(padded)
    starts = ends - padded
    pos = jnp.cumsum(seli, axis=0) - seli
    dest_all = starts[None, :] + pos
    nth = jnp.cumsum(seli, axis=1) * seli
    pick = [(nth == n + 1) for n in range(k)]
    dest = jnp.stack([jnp.sum(jnp.where(p, dest_all, 0), axis=1) for p in pick], axis=1)
    wts = jnp.stack([jnp.sum(jnp.where(p, comb_t.T, 0.0), axis=1) for p in pick], axis=1)
    r_max = k * t + e * tm
    flat = dest.reshape(-1)
    row_token = jnp.zeros((r_max,), jnp.int32).at[flat].set(jnp.repeat(jnp.arange(t, dtype=jnp.int32), k))
    row_weight = jnp.zeros((r_max,), F32).at[flat].set(wts.reshape(-1))
    n_tiles = r_max // tm
    tile_start = jnp.arange(n_tiles, dtype=jnp.int32) * tm
    n_valid = (ends[-1] // tm).astype(jnp.int32)
    tile_expert = jnp.sum((tile_start[:, None] >= ends[None, :]).astype(jnp.int32), axis=1)
    tile_expert = jnp.minimum(tile_expert, e - 1)
    last_valid_expert = tile_expert[jnp.maximum(n_valid - 1, 0)]
    tile_expert = jnp.where(jnp.arange(n_tiles) < n_valid, tile_expert, last_valid_expert)
    return row_token, row_weight.reshape(r_max, 1), tile_expert, n_valid.reshape(1), dest


def _ln_combine_kernel(dest_ref, x_ref, ys_ref, gate_ref, g_ref, b_ref, sc_ref, sh_ref,
                       xo_ref, h_ref, buf_ref, sem, *, tm, emit_h):
    def row_copy(n):
        return pltpu.make_async_copy(ys_ref.at[pl.ds(dest_ref[0, 0, n], 1)],
                                     buf_ref.at[n % 2, pl.ds(n // 2, 1)], sem)

    def start(n, c):
        row_copy(n).start()
        return c

    def wait(n, c):
        row_copy(n).wait()
        return c

    lax.fori_loop(0, 2 * tm, start, 0, unroll=DMA_LOOP_UNROLL)
    lax.fori_loop(0, 2 * tm, wait, 0, unroll=DMA_LOOP_UNROLL)
    y = buf_ref[0] + buf_ref[1]
    z = DN_ALPHA * x_ref[...] + gate_ref[...] * y
    xo = _layer_norm_rows(z, g_ref[...], b_ref[...])
    xo_ref[...] = xo
    if emit_h:
        h_ref[...] = (xo * (1.0 + sc_ref[...]) + sh_ref[...]).astype(h_ref.dtype)
    else:
        h_ref[...] = jnp.zeros(h_ref.shape, h_ref.dtype)


def ln_combine(x1, ys, dest, mod, which_gate, ln_g, ln_b, next_mod, seq, emit_h):
    t, d = x1.shape
    tm = _tile(seq, 256)

    def vec(which):
        return pl.BlockSpec((None, None, 1, d), lambda i: ((i * tm) // seq, which, 0, 0))

    row = pl.BlockSpec((tm, d), lambda i: (i, 0))
    const = lambda shape: pl.BlockSpec(shape, lambda i: (0, 0))
    h_rows = tm if emit_h else 8
    h_spec = pl.BlockSpec((h_rows, d), (lambda i: (i, 0)) if emit_h else (lambda i: (0, 0)))
    xo, h = pl.pallas_call(
        functools.partial(_ln_combine_kernel, tm=tm, emit_h=emit_h),
        grid=(t // tm,),
        in_specs=[pl.BlockSpec((1, 1, 2 * tm), lambda i: (i, 0, 0), memory_space=pltpu.SMEM),
                  row, pl.BlockSpec(memory_space=pl.ANY), vec(which_gate), const((1, d)), const((1, d)),
                  pl.BlockSpec((None, None, 1, d), lambda i: ((i * tm) // seq, 1, 0, 0)),
                  pl.BlockSpec((None, None, 1, d), lambda i: ((i * tm) // seq, 0, 0, 0))],
        out_specs=[row, h_spec],
        out_shape=[jax.ShapeDtypeStruct((t, d), F32),
                   jax.ShapeDtypeStruct((t if emit_h else 8, d), BF16)],
        scratch_shapes=[pltpu.VMEM((2, tm, d), F32), pltpu.SemaphoreType.DMA(())],
        compiler_params=_cparams(("arbitrary",)),
        name="ln_combine",
    )(dest.reshape(t // tm, 1, 2 * tm), x1, ys, mod, ln_g.reshape(1, d), ln_b.reshape(1, d), next_mod, next_mod)
    return xo, h


MOE_TILE = 512


def _moe_block(x1, hp, sel_t, comb_t, mod, ln_g, ln_b, w_gate, w_up, w_down, next_mod, seq, emit_h):
    tm = min(MOE_TILE, x1.shape[0])
    row_token, row_weight, tile_expert, n_valid, dest = moe_dispatch(sel_t, comb_t, tm)
    xs = gather_rows(hp, row_token)
    ys = moe_ffn(xs, w_gate.astype(BF16), w_up.astype(BF16), w_down.astype(BF16),
                 row_weight, tile_expert, n_valid, tm)
    return ln_combine(x1, ys, dest, mod, 5, ln_g, ln_b, next_mod, seq, emit_h)


def _flags(n, tn, rope_ranges):
    out = []
    for j in range(n // tn):
        c = j * tn
        out.append(1 if any(lo <= c < hi for lo, hi in rope_ranges) else 0)
    return jnp.asarray(out, jnp.int32)


def kernel(x, c, positions, router_w, router_bias, l0_w_ada, l0_b_ada, l0_w_in, l0_lambda_q1, l0_lambda_k1, l0_lambda_q2, l0_lambda_k2, l0_subln_g, l0_w_out, l0_ln1_g, l0_ln1_b, l0_w_gate, l0_w_up, l0_w_down, l0_ln2_g, l0_ln2_b, l1_w_ada, l1_b_ada, l1_w_in, l1_b_in, l1_sinks, l1_w_out, l1_ln1_g, l1_ln1_b, l1_w_gate, l1_w_up, l1_w_down, l1_ln2_g, l1_ln2_b):
    batch, seq, d = x.shape
    t = batch * seq
    a_heads = d // (4 * HEAD_DIM)
    a_width = a_heads * 2 * HEAD_DIM
    b_heads = d // (2 * HEAD_DIM)
    b_width = b_heads * HEAD_DIM
    c_heads = d // HEAD_DIM
    c_kv = c_heads // 4
    assert l0_w_in.shape[1] == 3 * a_width + 3 * b_width
    assert seq % MOBA_BLOCK == 0 and seq % WINDOW == 0

    x2 = x.reshape(t, d)
    tabs = rope_tables(positions)
    mod0 = ada_modulation(c, l0_w_ada, l0_b_ada)
    mod1 = ada_modulation(c, l1_w_ada, l1_b_ada)
    tn = 512 if d >= 2048 else 256

    h = modulate(x2, mod0, 0, 1, seq)
    n0 = l0_w_in.shape[1]
    rope0 = [(0, 2 * a_width), (3 * a_width, 3 * a_width + 2 * b_width)]
    proj = projection(h, l0_w_in.astype(BF16), jnp.zeros((n0,), F32), _flags(n0, tn, rope0), tabs, BF16,
                      tn_pref=tn)
    lam_vecs = jnp.stack([l0_lambda_q1, l0_lambda_k1, l0_lambda_q2, l0_lambda_k2]).astype(F32)
    lam_init = 0.8 - 0.6 * math.exp(-0.3 * 0)
    wa = 2 * HEAD_DIM
    o_a = diff_attention(proj, lam_vecs, l0_subln_g, batch=batch, seq=seq, n_heads=a_heads,
                         q_col=0, k_col=a_width // wa, v_col=2 * a_width // wa, lam_init=lam_init)
    bq0 = 3 * a_width // HEAD_DIM
    nbh = b_width // HEAD_DIM
    kmean = moba_kmean(proj, batch=batch, seq=seq, n_heads=b_heads, k_col=bq0 + nbh)
    o_b = moba_attention(proj, kmean, batch=batch, seq=seq, n_heads=b_heads,
                         q_col=bq0, k_col=bq0 + nbh, v_col=bq0 + 2 * nbh)
    o = jnp.concatenate([o_a, o_b], axis=1)
    y = projection(o, l0_w_out.astype(BF16), jnp.zeros((d,), F32), _flags(d, tn, []), tabs, F32, tn_pref=tn)
    x1, hp, sel_t, comb_t = ln_router(x2, y, mod0, 2, 4, 3, l0_ln1_g, l0_ln1_b, router_w, router_bias, seq)
    x2, h = _moe_block(x1, hp, sel_t, comb_t, mod0, l0_ln2_g, l0_ln2_b, l0_w_gate, l0_w_up, l0_w_down,
                       mod1, seq, True)

    n1 = l1_w_in.shape[1]
    cq = c_heads * HEAD_DIM
    ckv = c_kv * HEAD_DIM
    rope1 = [(0, cq + ckv)]
    proj = projection(h, l1_w_in.astype(BF16), l1_b_in, _flags(n1, tn, rope1), tabs, BF16, tn_pref=tn)
    o = swa_attention(proj, l1_sinks, batch=batch, seq=seq, n_q_heads=c_heads, n_kv_heads=c_kv,
                      k_col=cq // HEAD_DIM, v_col=(cq + ckv) // HEAD_DIM)
    y = projection(o, l1_w_out.astype(BF16), jnp.zeros((d,), F32), _flags(d, tn, []), tabs, F32, tn_pref=tn)
    x1, hp, sel_t, comb_t = ln_router(x2, y, mod1, 2, 4, 3, l1_ln1_g, l1_ln1_b, router_w, router_bias, seq)
    x2, _ = _moe_block(x1, hp, sel_t, comb_t, mod1, l1_ln2_g, l1_ln2_b, l1_w_gate, l1_w_up, l1_w_down,
                       mod1, seq, False)
    return x2.reshape(batch, seq, d)
```

```python
import functools
import math

import jax
import jax.numpy as jnp
from jax import lax
from jax.experimental import pallas as pl
from jax.experimental.pallas import tpu as pltpu

F32 = jnp.float32
BF16 = jnp.bfloat16
U32 = jnp.uint32

HEAD_DIM = 128
ROT_DIM = HEAD_DIM // 4
ROT_HALF = ROT_DIM // 2
ROPE_THETA = 500000.0
MOBA_BLOCK = 256
MOBA_TOPK = 3
WINDOW = 128
N_EXPERTS = 16
N_GROUPS = 4
EXPERTS_PER_GROUP = N_EXPERTS // N_GROUPS
DEPTH = 2
DN_ALPHA = float((2 * DEPTH) ** 0.25)
LN_EPS = 1e-5
RMS_EPS = 1e-5
ATTN_SCALE = HEAD_DIM ** -0.5
SCALE_LOG2E = ATTN_SCALE * math.log2(math.e)
NEG = -1e30

LANES = 128
V7X_VMEM_BYTES = 64 * 1024 * 1024
VMEM_LIMIT = 56 * 1024 * 1024


def _cparams(semantics):
    return pltpu.CompilerParams(dimension_semantics=semantics, vmem_limit_bytes=VMEM_LIMIT)


def _tile(n, pref):
    t = min(n, pref)
    assert n % t == 0, (n, t)
    return t


def _nt_dot(a, b):
    return lax.dot_general(a, b, (((1,), (1,)), ((), ())), preferred_element_type=F32)


def _ada_kernel(c_ref, w_ref, b_ref, o_ref):
    c = c_ref[...]
    a = (c * jax.nn.sigmoid(c)).astype(BF16)
    o_ref[...] = jnp.dot(a, w_ref[...].astype(BF16), preferred_element_type=F32) + b_ref[...]


def ada_modulation(c, w_ada, b_ada):
    bn, d = c.shape
    n = w_ada.shape[1]
    rows = 8
    cp = jnp.zeros((rows, d), F32).at[:bn].set(c)
    tn = _tile(n, 512)
    out = pl.pallas_call(
        _ada_kernel,
        grid=(n // tn,),
        in_specs=[pl.BlockSpec((rows, d), lambda j: (0, 0)),
                  pl.BlockSpec((d, tn), lambda j: (0, j)),
                  pl.BlockSpec((1, tn), lambda j: (0, j))],
        out_specs=pl.BlockSpec((rows, tn), lambda j: (0, j)),
        out_shape=jax.ShapeDtypeStruct((rows, n), F32),
        compiler_params=_cparams(("arbitrary",)),
        name="ada_modulation",
    )(cp, w_ada, b_ada.reshape(1, n))
    return out[:bn].reshape(bn, 6, 1, d)


def _modulate_kernel(x_ref, sc_ref, sh_ref, o_ref):
    o_ref[...] = (x_ref[...] * (1.0 + sc_ref[...]) + sh_ref[...]).astype(o_ref.dtype)


def modulate(x2, mod, which_shift, which_scale, seq):
    t, d = x2.shape
    tm = _tile(seq, 512)

    def vec(which):
        return pl.BlockSpec((None, None, 1, d), lambda i: ((i * tm) // seq, which, 0, 0))

    return pl.pallas_call(
        _modulate_kernel,
        grid=(t // tm,),
        in_specs=[pl.BlockSpec((tm, d), lambda i: (i, 0)), vec(which_scale), vec(which_shift)],
        out_specs=pl.BlockSpec((tm, d), lambda i: (i, 0)),
        out_shape=jax.ShapeDtypeStruct((t, d), BF16),
        compiler_params=_cparams(("arbitrary",)),
        name="modulate",
    )(x2, mod, mod)


def _proj_kernel(flag_ref, a_ref, w_ref, b_ref, c_ref, s1_ref, s2_ref, o_ref):
    j = pl.program_id(1)
    acc = jnp.dot(a_ref[...], w_ref[...], preferred_element_type=F32) + b_ref[...]
    tn = acc.shape[1]

    @pl.when(flag_ref[j] == 0)
    def _():
        o_ref[...] = acc.astype(o_ref.dtype)

    @pl.when(flag_ref[j] != 0)
    def _():
        cm, s1, s2 = c_ref[...], s1_ref[...], s2_ref[...]
        for hh in range(tn // HEAD_DIM):
            sl = slice(hh * HEAD_DIM, (hh + 1) * HEAD_DIM)
            xh = acc[:, sl]
            up = pltpu.roll(xh, HEAD_DIM - ROT_HALF, axis=1)
            dn = pltpu.roll(xh, ROT_HALF, axis=1)
            o_ref[:, sl] = (xh * cm + up * s1 + dn * s2).astype(o_ref.dtype)


def projection(a, w, bias, rope_flags, rope_tabs, out_dtype, tm_pref=1024, tn_pref=512):
    m, k = a.shape
    n = w.shape[1]
    tm = _tile(m, tm_pref)
    tn = _tile(n, tn_pref)
    assert rope_flags.shape == (n // tn,)
    cm, s1, s2 = rope_tabs
    tab = pl.BlockSpec((tm, HEAD_DIM), lambda i, j, f: (i, 0))
    return pl.pallas_call(
        _proj_kernel,
        grid_spec=pltpu.PrefetchScalarGridSpec(
            num_scalar_prefetch=1,
            grid=(m // tm, n // tn),
            in_specs=[pl.BlockSpec((tm, k), lambda i, j, f: (i, 0)),
                      pl.BlockSpec((k, tn), lambda i, j, f: (0, j)),
                      pl.BlockSpec((1, tn), lambda i, j, f: (0, j)),
                      tab, tab, tab],
            out_specs=pl.BlockSpec((tm, tn), lambda i, j, f: (i, j)),
        ),
        out_shape=jax.ShapeDtypeStruct((m, n), out_dtype),
        compiler_params=_cparams(("arbitrary", "arbitrary")),
        name="projection",
    )(rope_flags, a, w, bias.reshape(1, n).astype(F32), cm, s1, s2)


def rope_tables(positions):
    inv_freq = ROPE_THETA ** (-jnp.arange(0, ROT_DIM, 2, dtype=F32) / ROT_DIM)
    ang = positions.reshape(-1).astype(F32)[:, None] * inv_freq
    cos, sin = jnp.cos(ang), jnp.sin(ang)
    t = ang.shape[0]
    rest = HEAD_DIM - ROT_DIM
    cm = jnp.concatenate([cos, cos, jnp.ones((t, rest), F32)], axis=1)
    s1 = jnp.concatenate([-sin, jnp.zeros((t, HEAD_DIM - ROT_HALF), F32)], axis=1)
    s2 = jnp.concatenate([jnp.zeros((t, ROT_HALF), F32), sin, jnp.zeros((t, rest), F32)], axis=1)
    return cm, s1, s2


def _lane_tile(x, width):
    return x if width == LANES else jnp.concatenate([x] * (width // LANES), axis=1)


def _flash_step(s, v, m_ref, l_ref, acc_ref):
    m_prev = m_ref[...]
    m_new = jnp.maximum(m_prev, jnp.max(s, axis=1, keepdims=True))
    alpha = jnp.exp2((m_prev - m_new) * SCALE_LOG2E)
    p = jnp.exp2((s - _lane_tile(m_new, s.shape[1])) * SCALE_LOG2E)
    if l_ref is not None:
        l_ref[...] = alpha * l_ref[...] + jnp.sum(p, axis=1, keepdims=True)
    acc_ref[...] = (_lane_tile(alpha, acc_ref.shape[-1]) * acc_ref[...]
                    + jnp.dot(p.astype(v.dtype), v, preferred_element_type=F32))
    m_ref[...] = m_new


def _flash_init(m_ref, l_ref, acc_ref):
    m_ref[...] = jnp.full(m_ref.shape, NEG, F32)
    if l_ref is not None:
        l_ref[...] = jnp.zeros(l_ref.shape, F32)
    acc_ref[...] = jnp.zeros(acc_ref.shape, F32)


def _causal_mask(tq, tk, key_offset):
    qpos = lax.broadcasted_iota(jnp.int32, (tq, tk), 0)
    kpos = lax.broadcasted_iota(jnp.int32, (tq, tk), 1) + key_offset
    return kpos <= qpos


def _diff_attn_kernel(lam_ref, q_ref, k_ref, v_ref, g_ref, o_ref, m_ref, l_ref, acc_ref,
                      *, tq, lam_init):
    qi = pl.program_id(2)
    _flash_init(m_ref, l_ref, acc_ref)

    def block(j, mask):
        start = pl.multiple_of(j * tq, tq)
        vb = v_ref[pl.ds(start, tq), :]
        for mp in range(2):
            sl = slice(mp * HEAD_DIM, (mp + 1) * HEAD_DIM)
            s = _nt_dot(q_ref[:, sl], k_ref[pl.ds(start, tq), sl])
            if mask is not None:
                s = jnp.where(mask, s, NEG)
            _flash_step(s, vb, m_ref.at[mp], l_ref.at[mp], acc_ref.at[mp])

    def body(j, carry):
        block(j, None)
        return carry

    lax.fori_loop(0, qi, body, 0)
    block(qi, _causal_mask(tq, tq, 0))

    lam_full = (jnp.exp(jnp.sum(lam_ref[0:1, :] * lam_ref[1:2, :], axis=1, keepdims=True))
                - jnp.exp(jnp.sum(lam_ref[2:3, :] * lam_ref[3:4, :], axis=1, keepdims=True)) + lam_init)
    width = acc_ref.shape[-1]
    o = (acc_ref[0] / _lane_tile(l_ref[0], width)
         - lam_full * (acc_ref[1] / _lane_tile(l_ref[1], width)))
    o = o * lax.rsqrt(jnp.mean(o * o, axis=1, keepdims=True) + RMS_EPS) * g_ref[...]
    o_ref[...] = (o * (1.0 - lam_init)).astype(o_ref.dtype)


def diff_attention(proj, lam_vecs, subln_g, *, batch, seq, n_heads, q_col, k_col, v_col, lam_init):
    t = proj.shape[0]
    width = 2 * HEAD_DIM
    tq = _tile(seq, 512)
    nq = seq // tq
    kern = functools.partial(_diff_attn_kernel, tq=tq, lam_init=lam_init)
    return pl.pallas_call(
        kern,
        grid=(batch, n_heads, nq),
        in_specs=[pl.BlockSpec((4, HEAD_DIM), lambda b, h, qi: (0, 0)),
                  pl.BlockSpec((tq, width), lambda b, h, qi: (b * nq + qi, q_col + h)),
                  pl.BlockSpec((seq, width), lambda b, h, qi: (b, k_col + h)),
                  pl.BlockSpec((seq, width), lambda b, h, qi: (b, v_col + h)),
                  pl.BlockSpec((1, width), lambda b, h, qi: (0, 0))],
        out_specs=pl.BlockSpec((tq, width), lambda b, h, qi: (b * nq + qi, h)),
        out_shape=jax.ShapeDtypeStruct((t, n_heads * width), BF16),
        scratch_shapes=[pltpu.VMEM((2, tq, LANES), F32), pltpu.VMEM((2, tq, LANES), F32),
                        pltpu.VMEM((2, tq, width), F32)],
        compiler_params=_cparams(("arbitrary", "arbitrary", "arbitrary")),
        name="diff_attention",
    )(lam_vecs, proj, proj, proj, subln_g.reshape(1, width).astype(F32))


def _kmean_kernel(k_ref, o_ref, *, nb):
    k = k_ref[...].astype(F32)
    o_ref[...] = jnp.mean(k.reshape(nb, MOBA_BLOCK, HEAD_DIM), axis=1)


def moba_kmean(proj, *, batch, seq, n_heads, k_col):
    nb = seq // MOBA_BLOCK
    return pl.pallas_call(
        functools.partial(_kmean_kernel, nb=nb),
        grid=(batch, n_heads),
        in_specs=[pl.BlockSpec((seq, HEAD_DIM), lambda b, h: (b, k_col + h))],
        out_specs=pl.BlockSpec((None, None, nb, HEAD_DIM), lambda b, h: (b, h, 0, 0)),
        out_shape=jax.ShapeDtypeStruct((batch, n_heads, nb, HEAD_DIM), F32),
        compiler_params=_cparams(("arbitrary", "arbitrary")),
        name="moba_kmean",
    )(proj)


def _moba_block_bias(q, km, qi, *, nb, tq):
    km_hi = km.astype(BF16)
    km_lo = (km - km_hi.astype(F32)).astype(BF16)
    gate = _nt_dot(km_hi, q) + _nt_dot(km_lo, q)
    blk = lax.broadcasted_iota(jnp.int32, (nb, tq), 0)
    own = qi * (tq // MOBA_BLOCK) + lax.broadcasted_iota(jnp.int32, (nb, tq), 1) // MOBA_BLOCK
    rank = jnp.zeros((nb, tq), jnp.int32)
    for jp in range(nb):
        gj = gate[jp:jp + 1, :]
        before = jnp.logical_or(gj > gate, jnp.logical_and(gj == gate, blk > jp))
        rank = rank + jnp.where(jnp.logical_and(before, own > jp), 1, 0)
    visible = jnp.logical_or(jnp.logical_and(blk < own, rank < MOBA_TOPK), blk == own)
    bias = jnp.where(visible, 0.0, NEG)
    if nb < LANES:
        bias = jnp.concatenate([bias, jnp.zeros((LANES - nb, tq), F32)], axis=0)
    return bias.T


def _moba_kernel(q_ref, k_ref, v_ref, km_ref, o_ref, kaug_ref, vaug_ref, qaug_ref, m_ref, acc_ref,
                 *, nb, tq):
    qi = pl.program_id(2)
    seq = k_ref.shape[0]

    @pl.when(qi == 0)
    def _():
        kaug_ref[:, :HEAD_DIM] = k_ref[...]
        key_blk = lax.broadcasted_iota(jnp.int32, (seq, LANES), 0) // MOBA_BLOCK
        lane = lax.broadcasted_iota(jnp.int32, (seq, LANES), 1)
        kaug_ref[:, HEAD_DIM:] = jnp.where(key_blk == lane, 1.0, 0.0).astype(BF16)
        vaug_ref[:, :HEAD_DIM] = v_ref[...]
        vaug_ref[:, HEAD_DIM:] = jnp.ones((seq, LANES), BF16)

    _flash_init(m_ref, None, acc_ref)
    q = q_ref[...]
    qaug_ref[:, :HEAD_DIM] = q
    qaug_ref[:, HEAD_DIM:] = _moba_block_bias(q, km_ref[...], qi, nb=nb, tq=tq).astype(BF16)

    def block(j, mask):
        start = pl.multiple_of(j * tq, tq)
        s = _nt_dot(qaug_ref[...], kaug_ref[pl.ds(start, tq), :])
        if mask is not None:
            s = jnp.where(mask, s, NEG)
        _flash_step(s, vaug_ref[pl.ds(start, tq), :], m_ref, None, acc_ref)

    def body(j, carry):
        block(j, None)
        return carry

    lax.fori_loop(0, qi, body, 0)
    block(qi, _causal_mask(tq, tq, 0))
    o_ref[...] = (acc_ref[:, :HEAD_DIM] / acc_ref[:, HEAD_DIM:]).astype(o_ref.dtype)


def moba_attention(proj, kmean, *, batch, seq, n_heads, q_col, k_col, v_col):
    t = proj.shape[0]
    nb = seq // MOBA_BLOCK
    assert nb <= LANES
    tq = _tile(seq, 2 * MOBA_BLOCK)
    assert tq % MOBA_BLOCK == 0
    nq = seq // tq
    return pl.pallas_call(
        functools.partial(_moba_kernel, nb=nb, tq=tq),
        grid=(batch, n_heads, nq),
        in_specs=[pl.BlockSpec((tq, HEAD_DIM), lambda b, h, qi: (b * nq + qi, q_col + h)),
                  pl.BlockSpec((seq, HEAD_DIM), lambda b, h, qi: (b, k_col + h)),
                  pl.BlockSpec((seq, HEAD_DIM), lambda b, h, qi: (b, v_col + h)),
                  pl.BlockSpec((None, None, nb, HEAD_DIM), lambda b, h, qi: (b, h, 0, 0))],
        out_specs=pl.BlockSpec((tq, HEAD_DIM), lambda b, h, qi: (b * nq + qi, h)),
        out_shape=jax.ShapeDtypeStruct((t, n_heads * HEAD_DIM), BF16),
        scratch_shapes=[pltpu.VMEM((seq, 2 * HEAD_DIM), BF16), pltpu.VMEM((seq, HEAD_DIM + LANES), BF16),
                        pltpu.VMEM((tq, 2 * HEAD_DIM), BF16),
                        pltpu.VMEM((tq, LANES), F32), pltpu.VMEM((tq, HEAD_DIM + LANES), F32)],
        compiler_params=_cparams(("arbitrary", "arbitrary", "arbitrary")),
        name="moba_attention",
    )(proj, proj, proj, kmean)


def _swa_kernel(sink_ref, q_ref, kc_ref, kp_ref, vc_ref, vp_ref, o_ref, *, group, nsub):
    g = pl.program_id(1)
    i = pl.program_id(2)
    w = WINDOW
    rows = group * w
    qrow = lax.broadcasted_iota(jnp.int32, (rows, 2 * w), 0) % w + w
    kcol = lax.broadcasted_iota(jnp.int32, (rows, 2 * w), 1)
    rel = qrow - kcol
    band = jnp.logical_and(rel >= 0, rel < w)
    head_of_row = lax.broadcasted_iota(jnp.int32, (rows, 1), 0) // w
    sink = jnp.zeros((rows, 1), F32)
    for jh in range(group):
        sink = jnp.where(head_of_row == jh, sink_ref[g * group + jh], sink)

    for n in range(nsub):
        qs = jnp.concatenate(
            [q_ref[n * w:(n + 1) * w, jh * HEAD_DIM:(jh + 1) * HEAD_DIM] for jh in range(group)], axis=0)
        if n == 0:
            kk = jnp.concatenate([kp_ref[...], kc_ref[0:w, :]], axis=0)
            vv = jnp.concatenate([vp_ref[...], vc_ref[0:w, :]], axis=0)
            mask = jnp.logical_and(band, kcol >= jnp.where(i > 0, 0, w))
        else:
            kk = kc_ref[(n - 1) * w:(n + 1) * w, :]
            vv = vc_ref[(n - 1) * w:(n + 1) * w, :]
            mask = band
        s = jnp.where(mask, _nt_dot(qs, kk) * ATTN_SCALE, NEG)
        m = jnp.maximum(jnp.max(s, axis=1, keepdims=True), sink)
        p = jnp.exp(s - m)
        denom = jnp.sum(p, axis=1, keepdims=True) + jnp.exp(sink - m)
        o = jnp.dot(p.astype(vv.dtype), vv, preferred_element_type=F32) / denom
        for jh in range(group):
            o_ref[n * w:(n + 1) * w, jh * HEAD_DIM:(jh + 1) * HEAD_DIM] = (
                o[jh * w:(jh + 1) * w, :].astype(o_ref.dtype))


def swa_attention(proj, sinks, *, batch, seq, n_q_heads, n_kv_heads, k_col, v_col):
    t = proj.shape[0]
    group = n_q_heads // n_kv_heads
    tq = _tile(seq, 4 * WINDOW)
    nsub = tq // WINDOW
    nq = seq // tq
    gw = group * HEAD_DIM

    def prev_map(col):
        def f(b, g, i):
            return (b * (seq // WINDOW) + jnp.maximum(i * nsub - 1, 0), col + g)
        return f

    return pl.pallas_call(
        functools.partial(_swa_kernel, group=group, nsub=nsub),
        grid_spec=pltpu.PrefetchScalarGridSpec(
            num_scalar_prefetch=1,
            grid=(batch, n_kv_heads, nq),
            in_specs=[pl.BlockSpec((tq, gw), lambda b, g, i, s: (b * nq + i, g)),
                      pl.BlockSpec((tq, HEAD_DIM), lambda b, g, i, s: (b * nq + i, k_col + g)),
                      pl.BlockSpec((WINDOW, HEAD_DIM), lambda b, g, i, s: prev_map(k_col)(b, g, i)),
                      pl.BlockSpec((tq, HEAD_DIM), lambda b, g, i, s: (b * nq + i, v_col + g)),
                      pl.BlockSpec((WINDOW, HEAD_DIM), lambda b, g, i, s: prev_map(v_col)(b, g, i))],
            out_specs=pl.BlockSpec((tq, gw), lambda b, g, i, s: (b * nq + i, g)),
        ),
        out_shape=jax.ShapeDtypeStruct((t, n_q_heads * HEAD_DIM), BF16),
        compiler_params=_cparams(("arbitrary", "arbitrary", "arbitrary")),
        name="swa_attention",
    )(sinks.astype(F32), proj, proj, proj, proj, proj)


def _layer_norm_rows(z, g, b):
    mu = jnp.mean(z, axis=1, keepdims=True)
    zc = z - mu
    var = jnp.mean(zc * zc, axis=1, keepdims=True)
    return zc * lax.rsqrt(var + LN_EPS) * g + b


def _pack_bf16_halves(h):
    half = h.shape[1] // 2
    lo = lax.bitcast_convert_type(h[:, :half].astype(BF16).astype(F32), U32)
    hi = lax.bitcast_convert_type(h[:, half:].astype(BF16).astype(F32), U32)
    return jnp.bitwise_or(jnp.right_shift(lo, jnp.uint32(16)), hi)


def _unpack_bf16_halves(w):
    lo = lax.bitcast_convert_type(jnp.left_shift(w, jnp.uint32(16)), F32).astype(BF16)
    hi = lax.bitcast_convert_type(jnp.bitwise_and(w, jnp.uint32(0xFFFF0000)), F32).astype(BF16)
    return lo, hi


def _route(scores, biased):
    e, tm = scores.shape
    epg = EXPERTS_PER_GROUP
    gs = []
    for gi in range(N_GROUPS):
        rows = [biased[gi * epg + r:gi * epg + r + 1, :] for r in range(epg)]
        best = None
        for a in range(epg):
            for b in range(a + 1, epg):
                pair = rows[a] + rows[b]
                best = pair if best is None else jnp.maximum(best, pair)
        gs.append(best)
    gsel = []
    for gi in range(N_GROUPS):
        ok = None
        for go in range(N_GROUPS):
            if go == gi:
                continue
            c = (gs[gi] > gs[go]) if go < gi else (gs[gi] >= gs[go])
            ok = c if ok is None else jnp.logical_and(ok, c)
        gsel.append(ok)
    sel_rows, comb_rows = [], []
    for gi in range(N_GROUPS):
        rows = [biased[gi * epg + r:gi * epg + r + 1, :] for r in range(epg)]
        for r in range(epg):
            rank = jnp.zeros((1, tm), jnp.int32)
            for o in range(epg):
                if o == r:
                    continue
                before = (rows[o] >= rows[r]) if o < r else (rows[o] > rows[r])
                rank = rank + jnp.where(before, 1, 0)
            s = jnp.logical_and(gsel[gi], rank < 2)
            sel_rows.append(jnp.where(s, 1.0, 0.0))
            comb_rows.append(jnp.where(s, scores[gi * epg + r:gi * epg + r + 1, :], 0.0))
    sel = jnp.concatenate(sel_rows, axis=0)
    comb = jnp.concatenate(comb_rows, axis=0)
    comb = comb / jnp.sum(comb, axis=0, keepdims=True)
    return sel, comb


def _ln_router_kernel(x_ref, y_ref, gate_ref, g_ref, b_ref, sc_ref, sh_ref, rwh_ref, rwl_ref, rb_ref,
                      x1_ref, hp_ref, sel_ref, comb_ref):
    z = DN_ALPHA * x_ref[...] + gate_ref[...] * y_ref[...]
    x1 = _layer_norm_rows(z, g_ref[...], b_ref[...])
    x1_ref[...] = x1
    h = x1 * (1.0 + sc_ref[...]) + sh_ref[...]
    hp_ref[...] = _pack_bf16_halves(h)
    h_hi = h.astype(BF16)
    h_lo = (h - h_hi.astype(F32)).astype(BF16)
    logits = _nt_dot(rwh_ref[...], h_hi) + _nt_dot(rwl_ref[...], h_hi) + _nt_dot(rwh_ref[...], h_lo)
    scores = jax.nn.sigmoid(logits)
    sel, comb = _route(scores, scores + rb_ref[...])
    sel_ref[...] = sel
    comb_ref[...] = comb


def ln_router(x2, y, mod, which_gate, which_scale, which_shift, ln_g, ln_b, router_w, router_bias, seq):
    t, d = x2.shape
    tm = _tile(seq, 256)
    e = router_w.shape[1]
    rwt = router_w.T.astype(F32)
    rw_hi = rwt.astype(BF16)
    rw_lo = (rwt - rw_hi.astype(F32)).astype(BF16)

    def vec(which):
        return pl.BlockSpec((None, None, 1, d), lambda i: ((i * tm) // seq, which, 0, 0))

    row = pl.BlockSpec((tm, d), lambda i: (i, 0))
    const = lambda shape: pl.BlockSpec(shape, lambda i: (0, 0))
    return pl.pallas_call(
        _ln_router_kernel,
        grid=(t // tm,),
        in_specs=[row, row, vec(which_gate), const((1, d)), const((1, d)), vec(which_scale), vec(which_shift),
                  const((e, d)), const((e, d)), const((e, 1))],
        out_specs=[row, pl.BlockSpec((tm, d // 2), lambda i: (i, 0)),
                   pl.BlockSpec((e, tm), lambda i: (0, i)), pl.BlockSpec((e, tm), lambda i: (0, i))],
        out_shape=[jax.ShapeDtypeStruct((t, d), F32), jax.ShapeDtypeStruct((t, d // 2), U32),
                   jax.ShapeDtypeStruct((e, t), F32), jax.ShapeDtypeStruct((e, t), F32)],
        compiler_params=_cparams(("arbitrary",)),
        name="ln_router",
    )(x2, y, mod, ln_g.reshape(1, d), ln_b.reshape(1, d), mod, mod, rw_hi, rw_lo,
      router_bias.reshape(e, 1).astype(F32))


DMA_LOOP_UNROLL = 8


def _gather_rows_kernel(idx_ref, src_ref, out_ref, sem, *, tg):
    def row_copy(r):
        return pltpu.make_async_copy(src_ref.at[pl.ds(idx_ref[0, 0, r], 1)], out_ref.at[pl.ds(r, 1)], sem)

    def start(r, c):
        row_copy(r).start()
        return c

    def wait(r, c):
        row_copy(r).wait()
        return c

    lax.fori_loop(0, tg, start, 0, unroll=DMA_LOOP_UNROLL)
    lax.fori_loop(0, tg, wait, 0, unroll=DMA_LOOP_UNROLL)


def gather_rows(src, idx, tg_pref=256):
    r = idx.shape[0]
    tg = _tile(r, tg_pref)
    return pl.pallas_call(
        functools.partial(_gather_rows_kernel, tg=tg),
        grid=(r // tg,),
        in_specs=[pl.BlockSpec((1, 1, tg), lambda i: (i, 0, 0), memory_space=pltpu.SMEM),
                  pl.BlockSpec(memory_space=pl.ANY)],
        out_specs=pl.BlockSpec((tg, src.shape[1]), lambda i: (i, 0)),
        out_shape=jax.ShapeDtypeStruct((r, src.shape[1]), src.dtype),
        scratch_shapes=[pltpu.SemaphoreType.DMA(())],
        compiler_params=_cparams(("arbitrary",)),
        name="gather_rows",
    )(idx.reshape(r // tg, 1, tg), src)


def _moe_ffn_kernel(te_ref, nv_ref, xs_ref, wg_ref, wu_ref, wd_ref, rw_ref, ys_ref, xb_ref):
    i = pl.program_id(0)
    f = pl.program_id(1)
    half = xs_ref.shape[1]

    @pl.when(i < nv_ref[0])
    def _():
        @pl.when(f == 0)
        def _():
            lo, hi = _unpack_bf16_halves(xs_ref[...])
            xb_ref[:, :half] = lo
            xb_ref[:, half:] = hi

        xb = xb_ref[...]
        gt = jnp.dot(xb, wg_ref[...], preferred_element_type=F32)
        up = jnp.dot(xb, wu_ref[...], preferred_element_type=F32)
        hm = (gt * jax.nn.sigmoid(gt) * up).astype(BF16)
        contrib = jnp.dot(hm, wd_ref[...], preferred_element_type=F32) * rw_ref[...]

        @pl.when(f == 0)
        def _():
            ys_ref[...] = contrib

        @pl.when(f != 0)
        def _():
            ys_ref[...] += contrib

    @pl.when(jnp.logical_and(i >= nv_ref[0], f == 0))
    def _():
        ys_ref[...] = jnp.zeros(ys_ref.shape, ys_ref.dtype)


def moe_ffn(xs, w_gate, w_up, w_down, row_weight, tile_expert, n_valid, tm):
    r, half = xs.shape
    e, d, ff = w_gate.shape
    tf = _tile(ff, 256)
    nff = ff // tf
    n_tiles = r // tm

    def row_i(i, nv):
        return jnp.minimum(i, nv[0] - 1)

    def f_eff(i, f, nv):
        return jnp.where(i < nv[0], f, nff - 1)

    return pl.pallas_call(
        _moe_ffn_kernel,
        grid_spec=pltpu.PrefetchScalarGridSpec(
            num_scalar_prefetch=2,
            grid=(n_tiles, nff),
            in_specs=[pl.BlockSpec((tm, half), lambda i, f, te, nv: (row_i(i, nv), 0)),
                      pl.BlockSpec((None, d, tf), lambda i, f, te, nv: (te[i], 0, f_eff(i, f, nv))),
                      pl.BlockSpec((None, d, tf), lambda i, f, te, nv: (te[i], 0, f_eff(i, f, nv))),
                      pl.BlockSpec((None, tf, d), lambda i, f, te, nv: (te[i], f_eff(i, f, nv), 0)),
                      pl.BlockSpec((tm, 1), lambda i, f, te, nv: (row_i(i, nv), 0))],
            out_specs=pl.BlockSpec((tm, d), lambda i, f, te, nv: (i, 0)),
            scratch_shapes=[pltpu.VMEM((tm, d), BF16)],
        ),
        out_shape=jax.ShapeDtypeStruct((r, d), F32),
        compiler_params=_cparams(("arbitrary", "arbitrary")),
        name="moe_ffn",
    )(tile_expert, n_valid, xs, w_gate, w_up, w_down, row_weight)


def moe_dispatch(sel_t, comb_t, tm):
    e, t = sel_t.shape
    k = 2
    sel = sel_t.T > 0.5
    seli = sel.astype(jnp.int32)
    counts = jnp.sum(seli, axis=0)
    padded = ((counts + tm - 1) // tm) * tm
    ends = jnp.cumsum(padded)
    starts = ends - padded
    pos = jnp.cumsum(seli, axis=0) - seli
    dest_all = starts[None, :] + pos
    nth = jnp.cumsum(seli, axis=1) * seli
    pick = [(nth == n + 1) for n in range(k)]
    dest = jnp.stack([jnp.sum(jnp.where(p, dest_all, 0), axis=1) for p in pick], axis=1)
    wts = jnp.stack([jnp.sum(jnp.where(p, comb_t.T, 0.0), axis=1) for p in pick], axis=1)
    r_max = k * t + e * tm
    flat = dest.reshape(-1)
    row_token = jnp.zeros((r_max,), jnp.int32).at[flat].set(jnp.repeat(jnp.arange(t, dtype=jnp.int32), k))
    row_weight = jnp.zeros((r_max,), F32).at[flat].set(wts.reshape(-1))
    n_tiles = r_max // tm
    tile_start = jnp.arange(n_tiles, dtype=jnp.int32) * tm
    n_valid = (ends[-1] // tm).astype(jnp.int32)
    tile_expert = jnp.sum((tile_start[:, None] >= ends[None, :]).astype(jnp.int32), axis=1)
    tile_expert = jnp.minimum(tile_expert, e - 1)
    last_valid_expert = tile_expert[jnp.maximum(n_valid - 1, 0)]
    tile_expert = jnp.where(jnp.arange(n_tiles) < n_valid, tile_expert, last_valid_expert)
    return row_token, row_weight.reshape(r_max, 1), tile_expert, n_valid.reshape(1), dest


def _ln_combine_kernel(dest_ref, x_ref, ys_ref, gate_ref, g_ref, b_ref, sc_ref, sh_ref,
                       xo_ref, h_ref, buf_ref, sem, *, tm, emit_h):
    def row_copy(k, r):
        return pltpu.make_async_copy(ys_ref.at[pl.ds(dest_ref[0, k, r], 1)], buf_ref.at[k, pl.ds(r, 1)], sem)

    for k in range(2):
        def start(r, c, k=k):
            row_copy(k, r).start()
            return c
        lax.fori_loop(0, tm, start, 0, unroll=DMA_LOOP_UNROLL)
    for k in range(2):
        def wait(r, c, k=k):
            row_copy(k, r).wait()
            return c
        lax.fori_loop(0, tm, wait, 0, unroll=DMA_LOOP_UNROLL)
    y = buf_ref[0] + buf_ref[1]
    z = DN_ALPHA * x_ref[...] + gate_ref[...] * y
    xo = _layer_norm_rows(z, g_ref[...], b_ref[...])
    xo_ref[...] = xo
    if emit_h:
        h_ref[...] = (xo * (1.0 + sc_ref[...]) + sh_ref[...]).astype(h_ref.dtype)
    else:
        h_ref[...] = jnp.zeros(h_ref.shape, h_ref.dtype)


def ln_combine(x1, ys, dest, mod, which_gate, ln_g, ln_b, next_mod, seq, emit_h):
    t, d = x1.shape
    tm = _tile(seq, 256)

    def vec(which):
        return pl.BlockSpec((None, None, 1, d), lambda i: ((i * tm) // seq, which, 0, 0))

    row = pl.BlockSpec((tm, d), lambda i: (i, 0))
    const = lambda shape: pl.BlockSpec(shape, lambda i: (0, 0))
    h_rows = tm if emit_h else 8
    h_spec = pl.BlockSpec((h_rows, d), (lambda i: (i, 0)) if emit_h else (lambda i: (0, 0)))
    xo, h = pl.pallas_call(
        functools.partial(_ln_combine_kernel, tm=tm, emit_h=emit_h),
        grid=(t // tm,),
        in_specs=[pl.BlockSpec((1, 2, tm), lambda i: (i, 0, 0), memory_space=pltpu.SMEM),
                  row, pl.BlockSpec(memory_space=pl.ANY), vec(which_gate), const((1, d)), const((1, d)),
                  pl.BlockSpec((None, None, 1, d), lambda i: ((i * tm) // seq, 1, 0, 0)),
                  pl.BlockSpec((None, None, 1, d), lambda i: ((i * tm) // seq, 0, 0, 0))],
        out_specs=[row, h_spec],
        out_shape=[jax.ShapeDtypeStruct((t, d), F32),
                   jax.ShapeDtypeStruct((t if emit_h else 8, d), BF16)],
        scratch_shapes=[pltpu.VMEM((2, tm, d), F32), pltpu.SemaphoreType.DMA(())],
        compiler_params=_cparams(("arbitrary",)),
        name="ln_combine",
    )(dest.reshape(t // tm, tm, 2).transpose(0, 2, 1), x1, ys, mod, ln_g.reshape(1, d), ln_b.reshape(1, d), next_mod, next_mod)
    return xo, h


MOE_TILE = 512


def _moe_block(x1, hp, sel_t, comb_t, mod, ln_g, ln_b, w_gate, w_up, w_down, next_mod, seq, emit_h):
    tm = min(MOE_TILE, x1.shape[0])
    row_token, row_weight, tile_expert, n_valid, dest = moe_dispatch(sel_t, comb_t, tm)
    xs = gather_rows(hp, row_token)
    ys = moe_ffn(xs, w_gate.astype(BF16), w_up.astype(BF16), w_down.astype(BF16),
                 row_weight, tile_expert, n_valid, tm)
    return ln_combine(x1, ys, dest, mod, 5, ln_g, ln_b, next_mod, seq, emit_h)


def _flags(n, tn, rope_ranges):
    out = []
    for j in range(n // tn):
        c = j * tn
        out.append(1 if any(lo <= c < hi for lo, hi in rope_ranges) else 0)
    return jnp.asarray(out, jnp.int32)


def kernel(x, c, positions, router_w, router_bias, l0_w_ada, l0_b_ada, l0_w_in, l0_lambda_q1, l0_lambda_k1, l0_lambda_q2, l0_lambda_k2, l0_subln_g, l0_w_out, l0_ln1_g, l0_ln1_b, l0_w_gate, l0_w_up, l0_w_down, l0_ln2_g, l0_ln2_b, l1_w_ada, l1_b_ada, l1_w_in, l1_b_in, l1_sinks, l1_w_out, l1_ln1_g, l1_ln1_b, l1_w_gate, l1_w_up, l1_w_down, l1_ln2_g, l1_ln2_b):
    batch, seq, d = x.shape
    t = batch * seq
    a_heads = d // (4 * HEAD_DIM)
    a_width = a_heads * 2 * HEAD_DIM
    b_heads = d // (2 * HEAD_DIM)
    b_width = b_heads * HEAD_DIM
    c_heads = d // HEAD_DIM
    c_kv = c_heads // 4
    assert l0_w_in.shape[1] == 3 * a_width + 3 * b_width
    assert seq % MOBA_BLOCK == 0 and seq % WINDOW == 0

    x2 = x.reshape(t, d)
    tabs = rope_tables(positions)
    mod0 = ada_modulation(c, l0_w_ada, l0_b_ada)
    mod1 = ada_modulation(c, l1_w_ada, l1_b_ada)
    tn = 512 if d >= 2048 else 256

    h = modulate(x2, mod0, 0, 1, seq)
    n0 = l0_w_in.shape[1]
    rope0 = [(0, 2 * a_width), (3 * a_width, 3 * a_width + 2 * b_width)]
    proj = projection(h, l0_w_in.astype(BF16), jnp.zeros((n0,), F32), _flags(n0, tn, rope0), tabs, BF16,
                      tn_pref=tn)
    lam_vecs = jnp.stack([l0_lambda_q1, l0_lambda_k1, l0_lambda_q2, l0_lambda_k2]).astype(F32)
    lam_init = 0.8 - 0.6 * math.exp(-0.3 * 0)
    wa = 2 * HEAD_DIM
    o_a = diff_attention(proj, lam_vecs, l0_subln_g, batch=batch, seq=seq, n_heads=a_heads,
                         q_col=0, k_col=a_width // wa, v_col=2 * a_width // wa, lam_init=lam_init)
    bq0 = 3 * a_width // HEAD_DIM
    nbh = b_width // HEAD_DIM
    kmean = moba_kmean(proj, batch=batch, seq=seq, n_heads=b_heads, k_col=bq0 + nbh)
    o_b = moba_attention(proj, kmean, batch=batch, seq=seq, n_heads=b_heads,
                         q_col=bq0, k_col=bq0 + nbh, v_col=bq0 + 2 * nbh)
    o = jnp.concatenate([o_a, o_b], axis=1)
    y = projection(o, l0_w_out.astype(BF16), jnp.zeros((d,), F32), _flags(d, tn, []), tabs, F32, tn_pref=tn)
    x1, hp, sel_t, comb_t = ln_router(x2, y, mod0, 2, 4, 3, l0_ln1_g, l0_ln1_b, router_w, router_bias, seq)
    x2, h = _moe_block(x1, hp, sel_t, comb_t, mod0, l0_ln2_g, l0_ln2_b, l0_w_gate, l0_w_up, l0_w_down,
                       mod1, seq, True)

    n1 = l1_w_in.shape[1]
    cq = c_heads * HEAD_DIM
    ckv = c_kv * HEAD_DIM
    rope1 = [(0, cq + ckv)]
    proj = projection(h, l1_w_in.astype(BF16), l1_b_in, _flags(n1, tn, rope1), tabs, BF16, tn_pref=tn)
    o = swa_attention(proj, l1_sinks, batch=batch, seq=seq, n_q_heads=c_heads, n_kv_heads=c_kv,
                      k_col=cq // HEAD_DIM, v_col=(cq + ckv) // HEAD_DIM)
    y = projection(o, l1_w_out.astype(BF16), jnp.zeros((d,), F32), _flags(d, tn, []), tabs, F32, tn_pref=tn)
    x1, hp, sel_t, comb_t = ln_router(x2, y, mod1, 2, 4, 3, l1_ln1_g, l1_ln1_b, router_w, router_bias, seq)
    x2, _ = _moe_block(x1, hp, sel_t, comb_t, mod1, l1_ln2_g, l1_ln2_b, l1_w_gate, l1_w_up, l1_w_down,
                       mod1, seq, False)
    return x2.reshape(batch, seq, d)
```

```python
import functools
import math

import jax
import jax.numpy as jnp
from jax import lax
from jax.experimental import pallas as pl
from jax.experimental.pallas import tpu as pltpu

F32 = jnp.float32
BF16 = jnp.bfloat16
U32 = jnp.uint32

HEAD_DIM = 128
ROT_DIM = HEAD_DIM // 4
ROT_HALF = ROT_DIM // 2
ROPE_THETA = 500000.0
MOBA_BLOCK = 256
MOBA_TOPK = 3
WINDOW = 128
N_EXPERTS = 16
N_GROUPS = 4
EXPERTS_PER_GROUP = N_EXPERTS // N_GROUPS
DEPTH = 2
DN_ALPHA = float((2 * DEPTH) ** 0.25)
LN_EPS = 1e-5
RMS_EPS = 1e-5
ATTN_SCALE = HEAD_DIM ** -0.5
SCALE_LOG2E = ATTN_SCALE * math.log2(math.e)
NEG = -1e30

LANES = 128
V7X_VMEM_BYTES = 64 * 1024 * 1024
VMEM_LIMIT = 56 * 1024 * 1024


def _cparams(semantics):
    return pltpu.CompilerParams(dimension_semantics=semantics, vmem_limit_bytes=VMEM_LIMIT)


def _tile(n, pref):
    t = min(n, pref)
    assert n % t == 0, (n, t)
    return t


def _nt_dot(a, b):
    return lax.dot_general(a, b, (((1,), (1,)), ((), ())), preferred_element_type=F32)


def _ada_kernel(c_ref, w_ref, b_ref, o_ref):
    c = c_ref[...]
    a = (c * jax.nn.sigmoid(c)).astype(BF16)
    o_ref[...] = jnp.dot(a, w_ref[...].astype(BF16), preferred_element_type=F32) + b_ref[...]


def ada_modulation(c, w_ada, b_ada):
    bn, d = c.shape
    n = w_ada.shape[1]
    rows = 8
    cp = jnp.zeros((rows, d), F32).at[:bn].set(c)
    tn = _tile(n, 512)
    out = pl.pallas_call(
        _ada_kernel,
        grid=(n // tn,),
        in_specs=[pl.BlockSpec((rows, d), lambda j: (0, 0)),
                  pl.BlockSpec((d, tn), lambda j: (0, j)),
                  pl.BlockSpec((1, tn), lambda j: (0, j))],
        out_specs=pl.BlockSpec((rows, tn), lambda j: (0, j)),
        out_shape=jax.ShapeDtypeStruct((rows, n), F32),
        compiler_params=_cparams(("arbitrary",)),
        name="ada_modulation",
    )(cp, w_ada, b_ada.reshape(1, n))
    return out[:bn].reshape(bn, 6, 1, d)


def _modulate_kernel(x_ref, sc_ref, sh_ref, o_ref):
    o_ref[...] = (x_ref[...] * (1.0 + sc_ref[...]) + sh_ref[...]).astype(o_ref.dtype)


def modulate(x2, mod, which_shift, which_scale, seq):
    t, d = x2.shape
    tm = _tile(seq, 512)

    def vec(which):
        return pl.BlockSpec((None, None, 1, d), lambda i: ((i * tm) // seq, which, 0, 0))

    return pl.pallas_call(
        _modulate_kernel,
        grid=(t // tm,),
        in_specs=[pl.BlockSpec((tm, d), lambda i: (i, 0)), vec(which_scale), vec(which_shift)],
        out_specs=pl.BlockSpec((tm, d), lambda i: (i, 0)),
        out_shape=jax.ShapeDtypeStruct((t, d), BF16),
        compiler_params=_cparams(("arbitrary",)),
        name="modulate",
    )(x2, mod, mod)


def _proj_kernel(flag_ref, a_ref, w_ref, b_ref, c_ref, s1_ref, s2_ref, o_ref):
    j = pl.program_id(1)
    acc = jnp.dot(a_ref[...], w_ref[...], preferred_element_type=F32) + b_ref[...]
    tn = acc.shape[1]

    @pl.when(flag_ref[j] == 0)
    def _():
        o_ref[...] = acc.astype(o_ref.dtype)

    @pl.when(flag_ref[j] != 0)
    def _():
        cm, s1, s2 = c_ref[...], s1_ref[...], s2_ref[...]
        for hh in range(tn // HEAD_DIM):
            sl = slice(hh * HEAD_DIM, (hh + 1) * HEAD_DIM)
            xh = acc[:, sl]
            up = pltpu.roll(xh, HEAD_DIM - ROT_HALF, axis=1)
            dn = pltpu.roll(xh, ROT_HALF, axis=1)
            o_ref[:, sl] = (xh * cm + up * s1 + dn * s2).astype(o_ref.dtype)


def projection(a, w, bias, rope_flags, rope_tabs, out_dtype, tm_pref=1024, tn_pref=512):
    m, k = a.shape
    n = w.shape[1]
    tm = _tile(m, tm_pref)
    tn = _tile(n, tn_pref)
    assert rope_flags.shape == (n // tn,)
    cm, s1, s2 = rope_tabs
    tab = pl.BlockSpec((tm, HEAD_DIM), lambda i, j, f: (i, 0))
    return pl.pallas_call(
        _proj_kernel,
        grid_spec=pltpu.PrefetchScalarGridSpec(
            num_scalar_prefetch=1,
            grid=(m // tm, n // tn),
            in_specs=[pl.BlockSpec((tm, k), lambda i, j, f: (i, 0)),
                      pl.BlockSpec((k, tn), lambda i, j, f: (0, j)),
                      pl.BlockSpec((1, tn), lambda i, j, f: (0, j)),
                      tab, tab, tab],
            out_specs=pl.BlockSpec((tm, tn), lambda i, j, f: (i, j)),
        ),
        out_shape=jax.ShapeDtypeStruct((m, n), out_dtype),
        compiler_params=_cparams(("arbitrary", "arbitrary")),
        name="projection",
    )(rope_flags, a, w, bias.reshape(1, n).astype(F32), cm, s1, s2)


def rope_tables(positions):
    inv_freq = ROPE_THETA ** (-jnp.arange(0, ROT_DIM, 2, dtype=F32) / ROT_DIM)
    ang = positions.reshape(-1).astype(F32)[:, None] * inv_freq
    cos, sin = jnp.cos(ang), jnp.sin(ang)
    t = ang.shape[0]
    rest = HEAD_DIM - ROT_DIM
    cm = jnp.concatenate([cos, cos, jnp.ones((t, rest), F32)], axis=1)
    s1 = jnp.concatenate([-sin, jnp.zeros((t, HEAD_DIM - ROT_HALF), F32)], axis=1)
    s2 = jnp.concatenate([jnp.zeros((t, ROT_HALF), F32), sin, jnp.zeros((t, rest), F32)], axis=1)
    return cm, s1, s2


def _lane_tile(x, width):
    return x if width == LANES else jnp.concatenate([x] * (width // LANES), axis=1)


def _flash_step(s, v, m_ref, l_ref, acc_ref):
    m_prev = m_ref[...]
    m_new = jnp.maximum(m_prev, jnp.max(s, axis=1, keepdims=True))
    alpha = jnp.exp2((m_prev - m_new) * SCALE_LOG2E)
    p = jnp.exp2((s - _lane_tile(m_new, s.shape[1])) * SCALE_LOG2E)
    if l_ref is not None:
        l_ref[...] = alpha * l_ref[...] + jnp.sum(p, axis=1, keepdims=True)
    acc_ref[...] = (_lane_tile(alpha, acc_ref.shape[-1]) * acc_ref[...]
                    + jnp.dot(p.astype(v.dtype), v, preferred_element_type=F32))
    m_ref[...] = m_new


def _flash_init(m_ref, l_ref, acc_ref):
    m_ref[...] = jnp.full(m_ref.shape, NEG, F32)
    if l_ref is not None:
        l_ref[...] = jnp.zeros(l_ref.shape, F32)
    acc_ref[...] = jnp.zeros(acc_ref.shape, F32)


def _causal_mask(tq, tk, key_offset):
    qpos = lax.broadcasted_iota(jnp.int32, (tq, tk), 0)
    kpos = lax.broadcasted_iota(jnp.int32, (tq, tk), 1) + key_offset
    return kpos <= qpos


def _diff_attn_kernel(lam_ref, q_ref, k_ref, v_ref, g_ref, o_ref, m_ref, l_ref, acc_ref,
                      *, tq, lam_init):
    qi = pl.program_id(2)
    _flash_init(m_ref, l_ref, acc_ref)

    def block(j, mask):
        start = pl.multiple_of(j * tq, tq)
        vb = v_ref[pl.ds(start, tq), :]
        for mp in range(2):
            sl = slice(mp * HEAD_DIM, (mp + 1) * HEAD_DIM)
            s = _nt_dot(q_ref[:, sl], k_ref[pl.ds(start, tq), sl])
            if mask is not None:
                s = jnp.where(mask, s, NEG)
            _flash_step(s, vb, m_ref.at[mp], l_ref.at[mp], acc_ref.at[mp])

    def body(j, carry):
        block(j, None)
        return carry

    lax.fori_loop(0, qi, body, 0)
    block(qi, _causal_mask(tq, tq, 0))

    lam_full = (jnp.exp(jnp.sum(lam_ref[0:1, :] * lam_ref[1:2, :], axis=1, keepdims=True))
                - jnp.exp(jnp.sum(lam_ref[2:3, :] * lam_ref[3:4, :], axis=1, keepdims=True)) + lam_init)
    width = acc_ref.shape[-1]
    o = (acc_ref[0] / _lane_tile(l_ref[0], width)
         - lam_full * (acc_ref[1] / _lane_tile(l_ref[1], width)))
    o = o * lax.rsqrt(jnp.mean(o * o, axis=1, keepdims=True) + RMS_EPS) * g_ref[...]
    o_ref[...] = (o * (1.0 - lam_init)).astype(o_ref.dtype)


def diff_attention(proj, lam_vecs, subln_g, *, batch, seq, n_heads, q_col, k_col, v_col, lam_init):
    t = proj.shape[0]
    width = 2 * HEAD_DIM
    tq = _tile(seq, 512)
    nq = seq // tq
    kern = functools.partial(_diff_attn_kernel, tq=tq, lam_init=lam_init)
    return pl.pallas_call(
        kern,
        grid=(batch, n_heads, nq),
        in_specs=[pl.BlockSpec((4, HEAD_DIM), lambda b, h, qi: (0, 0)),
                  pl.BlockSpec((tq, width), lambda b, h, qi: (b * nq + qi, q_col + h)),
                  pl.BlockSpec((seq, width), lambda b, h, qi: (b, k_col + h)),
                  pl.BlockSpec((seq, width), lambda b, h, qi: (b, v_col + h)),
                  pl.BlockSpec((1, width), lambda b, h, qi: (0, 0))],
        out_specs=pl.BlockSpec((tq, width), lambda b, h, qi: (b * nq + qi, h)),
        out_shape=jax.ShapeDtypeStruct((t, n_heads * width), BF16),
        scratch_shapes=[pltpu.VMEM((2, tq, LANES), F32), pltpu.VMEM((2, tq, LANES), F32),
                        pltpu.VMEM((2, tq, width), F32)],
        compiler_params=_cparams(("arbitrary", "arbitrary", "arbitrary")),
        name="diff_attention",
    )(lam_vecs, proj, proj, proj, subln_g.reshape(1, width).astype(F32))


def _kmean_kernel(k_ref, o_ref, *, nb):
    k = k_ref[...].astype(F32)
    o_ref[...] = jnp.mean(k.reshape(nb, MOBA_BLOCK, HEAD_DIM), axis=1)


def moba_kmean(proj, *, batch, seq, n_heads, k_col):
    nb = seq // MOBA_BLOCK
    return pl.pallas_call(
        functools.partial(_kmean_kernel, nb=nb),
        grid=(batch, n_heads),
        in_specs=[pl.BlockSpec((seq, HEAD_DIM), lambda b, h: (b, k_col + h))],
        out_specs=pl.BlockSpec((None, None, nb, HEAD_DIM), lambda b, h: (b, h, 0, 0)),
        out_shape=jax.ShapeDtypeStruct((batch, n_heads, nb, HEAD_DIM), F32),
        compiler_params=_cparams(("arbitrary", "arbitrary")),
        name="moba_kmean",
    )(proj)


def _moba_block_bias(q, km, qi, *, nb, tq):
    km_hi = km.astype(BF16)
    km_lo = (km - km_hi.astype(F32)).astype(BF16)
    gate = _nt_dot(km_hi, q) + _nt_dot(km_lo, q)
    blk = lax.broadcasted_iota(jnp.int32, (nb, tq), 0).astype(F32)
    own = (qi * (tq // MOBA_BLOCK)
           + lax.broadcasted_iota(jnp.int32, (nb, tq), 1) // MOBA_BLOCK).astype(F32)
    valid = blk < own
    g = jnp.where(valid, gate, -jnp.inf)
    visible = blk == own
    for _ in range(MOBA_TOPK):
        best = jnp.max(g, axis=0, keepdims=True)
        cand = jnp.logical_and(g == best, valid)
        first = jnp.min(jnp.where(cand, blk, float(nb)), axis=0, keepdims=True)
        pick = blk == first
        visible = jnp.logical_or(visible, pick)
        valid = jnp.logical_and(valid, jnp.logical_not(pick))
        g = jnp.where(pick, -jnp.inf, g)
    bias = jnp.where(visible, 0.0, NEG)
    if nb < LANES:
        bias = jnp.concatenate([bias, jnp.zeros((LANES - nb, tq), F32)], axis=0)
    return bias.T


MOBA_HEADS_PER_STEP = 2


def _moba_kernel(q_ref, k_ref, v_ref, km_ref, o_ref, kaug_ref, vaug_ref, qaug_ref, m_ref, acc_ref,
                 *, nb, tq):
    qi = pl.program_id(2)
    seq = k_ref.shape[0]
    heads = [slice(hh * HEAD_DIM, (hh + 1) * HEAD_DIM) for hh in range(MOBA_HEADS_PER_STEP)]

    @pl.when(qi == 0)
    def _():
        key_blk = lax.broadcasted_iota(jnp.int32, (seq, LANES), 0) // MOBA_BLOCK
        lane = lax.broadcasted_iota(jnp.int32, (seq, LANES), 1)
        onehot = jnp.where(key_blk == lane, 1.0, 0.0).astype(BF16)
        for hh, sl in enumerate(heads):
            kaug_ref[hh, :, :HEAD_DIM] = k_ref[:, sl]
            kaug_ref[hh, :, HEAD_DIM:] = onehot
            vaug_ref[hh, :, :HEAD_DIM] = v_ref[:, sl]
            vaug_ref[hh, :, HEAD_DIM:] = jnp.ones((seq, LANES), BF16)

    _flash_init(m_ref, None, acc_ref)
    for hh, sl in enumerate(heads):
        q = q_ref[:, sl]
        qaug_ref[hh, :, :HEAD_DIM] = q
        qaug_ref[hh, :, HEAD_DIM:] = _moba_block_bias(q, km_ref[hh], qi, nb=nb, tq=tq).astype(BF16)

    def block(j, mask):
        start = pl.multiple_of(j * tq, tq)
        for hh in range(MOBA_HEADS_PER_STEP):
            s = _nt_dot(qaug_ref[hh], kaug_ref[hh, pl.ds(start, tq), :])
            if mask is not None:
                s = jnp.where(mask, s, NEG)
            _flash_step(s, vaug_ref[hh, pl.ds(start, tq), :], m_ref.at[hh], None, acc_ref.at[hh])

    def body(j, carry):
        block(j, None)
        return carry

    lax.fori_loop(0, qi, body, 0)
    block(qi, _causal_mask(tq, tq, 0))
    for hh, sl in enumerate(heads):
        o_ref[:, sl] = (acc_ref[hh, :, :HEAD_DIM] / acc_ref[hh, :, HEAD_DIM:]).astype(o_ref.dtype)


def moba_attention(proj, kmean, *, batch, seq, n_heads, q_col, k_col, v_col):
    t = proj.shape[0]
    nb = seq // MOBA_BLOCK
    assert nb <= LANES
    tq = _tile(seq, 2 * MOBA_BLOCK)
    assert tq % MOBA_BLOCK == 0
    nq = seq // tq
    hps = MOBA_HEADS_PER_STEP
    assert n_heads % hps == 0 and q_col % hps == 0 and k_col % hps == 0 and v_col % hps == 0
    width = hps * HEAD_DIM
    return pl.pallas_call(
        functools.partial(_moba_kernel, nb=nb, tq=tq),
        grid=(batch, n_heads // hps, nq),
        in_specs=[pl.BlockSpec((tq, width), lambda b, h, qi: (b * nq + qi, q_col // hps + h)),
                  pl.BlockSpec((seq, width), lambda b, h, qi: (b, k_col // hps + h)),
                  pl.BlockSpec((seq, width), lambda b, h, qi: (b, v_col // hps + h)),
                  pl.BlockSpec((None, hps, nb, HEAD_DIM), lambda b, h, qi: (b, h, 0, 0))],
        out_specs=pl.BlockSpec((tq, width), lambda b, h, qi: (b * nq + qi, h)),
        out_shape=jax.ShapeDtypeStruct((t, n_heads * HEAD_DIM), BF16),
        scratch_shapes=[pltpu.VMEM((hps, seq, 2 * HEAD_DIM), BF16),
                        pltpu.VMEM((hps, seq, HEAD_DIM + LANES), BF16),
                        pltpu.VMEM((hps, tq, 2 * HEAD_DIM), BF16),
                        pltpu.VMEM((hps, tq, LANES), F32), pltpu.VMEM((hps, tq, HEAD_DIM + LANES), F32)],
        compiler_params=_cparams(("arbitrary", "arbitrary", "arbitrary")),
        name="moba_attention",
    )(proj, proj, proj, kmean)


def _swa_kernel(sink_ref, q_ref, kc_ref, kp_ref, vc_ref, vp_ref, o_ref, *, group, nsub):
    g = pl.program_id(1)
    i = pl.program_id(2)
    w = WINDOW
    rows = group * w
    qrow = lax.broadcasted_iota(jnp.int32, (rows, 2 * w), 0) % w + w
    kcol = lax.broadcasted_iota(jnp.int32, (rows, 2 * w), 1)
    rel = qrow - kcol
    band = jnp.logical_and(rel >= 0, rel < w)
    head_of_row = lax.broadcasted_iota(jnp.int32, (rows, 1), 0) // w
    sink = jnp.zeros((rows, 1), F32)
    for jh in range(group):
        sink = jnp.where(head_of_row == jh, sink_ref[g * group + jh], sink)

    for n in range(nsub):
        qs = jnp.concatenate(
            [q_ref[n * w:(n + 1) * w, jh * HEAD_DIM:(jh + 1) * HEAD_DIM] for jh in range(group)], axis=0)
        if n == 0:
            kk = jnp.concatenate([kp_ref[...], kc_ref[0:w, :]], axis=0)
            vv = jnp.concatenate([vp_ref[...], vc_ref[0:w, :]], axis=0)
            mask = jnp.logical_and(band, kcol >= jnp.where(i > 0, 0, w))
        else:
            kk = kc_ref[(n - 1) * w:(n + 1) * w, :]
            vv = vc_ref[(n - 1) * w:(n + 1) * w, :]
            mask = band
        s = jnp.where(mask, _nt_dot(qs, kk) * ATTN_SCALE, NEG)
        m = jnp.maximum(jnp.max(s, axis=1, keepdims=True), sink)
        p = jnp.exp(s - m)
        denom = jnp.sum(p, axis=1, keepdims=True) + jnp.exp(sink - m)
        o = jnp.dot(p.astype(vv.dtype), vv, preferred_element_type=F32) / denom
        for jh in range(group):
            o_ref[n * w:(n + 1) * w, jh * HEAD_DIM:(jh + 1) * HEAD_DIM] = (
                o[jh * w:(jh + 1) * w, :].astype(o_ref.dtype))


def swa_attention(proj, sinks, *, batch, seq, n_q_heads, n_kv_heads, k_col, v_col):
    t = proj.shape[0]
    group = n_q_heads // n_kv_heads
    tq = _tile(seq, 4 * WINDOW)
    nsub = tq // WINDOW
    nq = seq // tq
    gw = group * HEAD_DIM

    def prev_map(col):
        def f(b, g, i):
            return (b * (seq // WINDOW) + jnp.maximum(i * nsub - 1, 0), col + g)
        return f

    return pl.pallas_call(
        functools.partial(_swa_kernel, group=group, nsub=nsub),
        grid_spec=pltpu.PrefetchScalarGridSpec(
            num_scalar_prefetch=1,
            grid=(batch, n_kv_heads, nq),
            in_specs=[pl.BlockSpec((tq, gw), lambda b, g, i, s: (b * nq + i, g)),
                      pl.BlockSpec((tq, HEAD_DIM), lambda b, g, i, s: (b * nq + i, k_col + g)),
                      pl.BlockSpec((WINDOW, HEAD_DIM), lambda b, g, i, s: prev_map(k_col)(b, g, i)),
                      pl.BlockSpec((tq, HEAD_DIM), lambda b, g, i, s: (b * nq + i, v_col + g)),
                      pl.BlockSpec((WINDOW, HEAD_DIM), lambda b, g, i, s: prev_map(v_col)(b, g, i))],
            out_specs=pl.BlockSpec((tq, gw), lambda b, g, i, s: (b * nq + i, g)),
        ),
        out_shape=jax.ShapeDtypeStruct((t, n_q_heads * HEAD_DIM), BF16),
        compiler_params=_cparams(("arbitrary", "arbitrary", "arbitrary")),
        name="swa_attention",
    )(sinks.astype(F32), proj, proj, proj, proj, proj)


def _layer_norm_rows(z, g, b):
    mu = jnp.mean(z, axis=1, keepdims=True)
    zc = z - mu
    var = jnp.mean(zc * zc, axis=1, keepdims=True)
    return zc * lax.rsqrt(var + LN_EPS) * g + b


def _pack_bf16_halves(h):
    half = h.shape[1] // 2
    lo = lax.bitcast_convert_type(h[:, :half].astype(BF16).astype(F32), U32)
    hi = lax.bitcast_convert_type(h[:, half:].astype(BF16).astype(F32), U32)
    return jnp.bitwise_or(jnp.right_shift(lo, jnp.uint32(16)), hi)


def _unpack_bf16_halves(w):
    lo = lax.bitcast_convert_type(jnp.left_shift(w, jnp.uint32(16)), F32).astype(BF16)
    hi = lax.bitcast_convert_type(jnp.bitwise_and(w, jnp.uint32(0xFFFF0000)), F32).astype(BF16)
    return lo, hi


def _route(scores, biased):
    e, tm = scores.shape
    epg = EXPERTS_PER_GROUP
    gs = []
    for gi in range(N_GROUPS):
        rows = [biased[gi * epg + r:gi * epg + r + 1, :] for r in range(epg)]
        best = None
        for a in range(epg):
            for b in range(a + 1, epg):
                pair = rows[a] + rows[b]
                best = pair if best is None else jnp.maximum(best, pair)
        gs.append(best)
    gsel = []
    for gi in range(N_GROUPS):
        ok = None
        for go in range(N_GROUPS):
            if go == gi:
                continue
            c = (gs[gi] > gs[go]) if go < gi else (gs[gi] >= gs[go])
            ok = c if ok is None else jnp.logical_and(ok, c)
        gsel.append(ok)
    sel_rows, comb_rows = [], []
    for gi in range(N_GROUPS):
        rows = [biased[gi * epg + r:gi * epg + r + 1, :] for r in range(epg)]
        for r in range(epg):
            rank = jnp.zeros((1, tm), jnp.int32)
            for o in range(epg):
                if o == r:
                    continue
                before = (rows[o] >= rows[r]) if o < r else (rows[o] > rows[r])
                rank = rank + jnp.where(before, 1, 0)
            s = jnp.logical_and(gsel[gi], rank < 2)
            sel_rows.append(jnp.where(s, 1.0, 0.0))
            comb_rows.append(jnp.where(s, scores[gi * epg + r:gi * epg + r + 1, :], 0.0))
    sel = jnp.concatenate(sel_rows, axis=0)
    comb = jnp.concatenate(comb_rows, axis=0)
    comb = comb / jnp.sum(comb, axis=0, keepdims=True)
    return sel, comb


def _ln_router_kernel(x_ref, y_ref, gate_ref, g_ref, b_ref, sc_ref, sh_ref, rwh_ref, rwl_ref, rb_ref,
                      x1_ref, hp_ref, sel_ref, comb_ref):
    z = DN_ALPHA * x_ref[...] + gate_ref[...] * y_ref[...]
    x1 = _layer_norm_rows(z, g_ref[...], b_ref[...])
    x1_ref[...] = x1
    h = x1 * (1.0 + sc_ref[...]) + sh_ref[...]
    hp_ref[...] = _pack_bf16_halves(h)
    h_hi = h.astype(BF16)
    h_lo = (h - h_hi.astype(F32)).astype(BF16)
    logits = _nt_dot(rwh_ref[...], h_hi) + _nt_dot(rwl_ref[...], h_hi) + _nt_dot(rwh_ref[...], h_lo)
    scores = jax.nn.sigmoid(logits)
    sel, comb = _route(scores, scores + rb_ref[...])
    sel_ref[...] = sel
    comb_ref[...] = comb


def ln_router(x2, y, mod, which_gate, which_scale, which_shift, ln_g, ln_b, router_w, router_bias, seq):
    t, d = x2.shape
    tm = _tile(seq, 256)
    e = router_w.shape[1]
    rwt = router_w.T.astype(F32)
    rw_hi = rwt.astype(BF16)
    rw_lo = (rwt - rw_hi.astype(F32)).astype(BF16)

    def vec(which):
        return pl.BlockSpec((None, None, 1, d), lambda i: ((i * tm) // seq, which, 0, 0))

    row = pl.BlockSpec((tm, d), lambda i: (i, 0))
    const = lambda shape: pl.BlockSpec(shape, lambda i: (0, 0))
    return pl.pallas_call(
        _ln_router_kernel,
        grid=(t // tm,),
        in_specs=[row, row, vec(which_gate), const((1, d)), const((1, d)), vec(which_scale), vec(which_shift),
                  const((e, d)), const((e, d)), const((e, 1))],
        out_specs=[row, pl.BlockSpec((tm, d // 2), lambda i: (i, 0)),
                   pl.BlockSpec((e, tm), lambda i: (0, i)), pl.BlockSpec((e, tm), lambda i: (0, i))],
        out_shape=[jax.ShapeDtypeStruct((t, d), F32), jax.ShapeDtypeStruct((t, d // 2), U32),
                   jax.ShapeDtypeStruct((e, t), F32), jax.ShapeDtypeStruct((e, t), F32)],
        compiler_params=_cparams(("arbitrary",)),
        name="ln_router",
    )(x2, y, mod, ln_g.reshape(1, d), ln_b.reshape(1, d), mod, mod, rw_hi, rw_lo,
      router_bias.reshape(e, 1).astype(F32))


DMA_LOOP_UNROLL = 8


def _gather_rows(idx_of_row, src_ref, dst_ref, sem, n_rows, *, wait):
    def step(r, c):
        cp = pltpu.make_async_copy(src_ref.at[pl.ds(idx_of_row(r), 1)], dst_ref.at[pl.ds(r, 1)], sem)
        if wait:
            cp.wait()
        else:
            cp.start()
        return c

    lax.fori_loop(0, n_rows, step, 0, unroll=DMA_LOOP_UNROLL)


def _moe_ffn_kernel(te_ref, nv_ref, idx_ref, nxt_ref, hp_ref, wg_ref, wu_ref, wd_ref, rw_ref, ys_ref,
                    xs_ref, xb_ref, sem):
    i = pl.program_id(0)
    f = pl.program_id(1)
    tm, half = xs_ref.shape[1], xs_ref.shape[2]

    @pl.when(i < nv_ref[0])
    def _():
        @pl.when(f == 0)
        def _():
            slot = i % 2

            @pl.when(i == 0)
            def _():
                _gather_rows(lambda r: idx_ref[0, 0, r], hp_ref, xs_ref.at[0], sem.at[0], tm, wait=False)

            @pl.when(i + 1 < nv_ref[0])
            def _():
                _gather_rows(lambda r: nxt_ref[0, 0, r], hp_ref, xs_ref.at[1 - slot], sem.at[1 - slot], tm,
                             wait=False)

            _gather_rows(lambda r: idx_ref[0, 0, r], hp_ref, xs_ref.at[slot], sem.at[slot], tm, wait=True)
            lo, hi = _unpack_bf16_halves(xs_ref[slot])
            xb_ref[:, :half] = lo
            xb_ref[:, half:] = hi

        xb = xb_ref[...]
        gt = jnp.dot(xb, wg_ref[...], preferred_element_type=F32)
        up = jnp.dot(xb, wu_ref[...], preferred_element_type=F32)
        hm = (gt * jax.nn.sigmoid(gt) * up).astype(BF16)
        contrib = jnp.dot(hm, wd_ref[...], preferred_element_type=F32) * rw_ref[...]

        @pl.when(f == 0)
        def _():
            ys_ref[...] = contrib

        @pl.when(f != 0)
        def _():
            ys_ref[...] += contrib

    @pl.when(jnp.logical_and(i >= nv_ref[0], f == 0))
    def _():
        ys_ref[...] = jnp.zeros(ys_ref.shape, ys_ref.dtype)


def moe_ffn(hp, row_token, w_gate, w_up, w_down, row_weight, tile_expert, n_valid, tm):
    r = row_token.shape[0]
    half = hp.shape[1]
    e, d, ff = w_gate.shape
    tf = _tile(ff, 256)
    nff = ff // tf
    n_tiles = r // tm

    def row_i(i, nv):
        return jnp.minimum(i, nv[0] - 1)

    def f_eff(i, f, nv):
        return jnp.where(i < nv[0], f, nff - 1)

    idx_spec = lambda shift: pl.BlockSpec(
        (1, 1, tm), lambda i, f, te, nv: (jnp.minimum(i + shift, n_tiles - 1), 0, 0), memory_space=pltpu.SMEM)
    return pl.pallas_call(
        _moe_ffn_kernel,
        grid_spec=pltpu.PrefetchScalarGridSpec(
            num_scalar_prefetch=2,
            grid=(n_tiles, nff),
            in_specs=[idx_spec(0), idx_spec(1), pl.BlockSpec(memory_space=pl.ANY),
                      pl.BlockSpec((None, d, tf), lambda i, f, te, nv: (te[i], 0, f_eff(i, f, nv))),
                      pl.BlockSpec((None, d, tf), lambda i, f, te, nv: (te[i], 0, f_eff(i, f, nv))),
                      pl.BlockSpec((None, tf, d), lambda i, f, te, nv: (te[i], f_eff(i, f, nv), 0)),
                      pl.BlockSpec((tm, 1), lambda i, f, te, nv: (row_i(i, nv), 0))],
            out_specs=pl.BlockSpec((tm, d), lambda i, f, te, nv: (i, 0)),
            scratch_shapes=[pltpu.VMEM((2, tm, half), U32), pltpu.VMEM((tm, d), BF16),
                            pltpu.SemaphoreType.DMA((2,))],
        ),
        out_shape=jax.ShapeDtypeStruct((r, d), F32),
        compiler_params=_cparams(("arbitrary", "arbitrary")),
        name="moe_ffn",
    )(tile_expert, n_valid, row_token.reshape(n_tiles, 1, tm), row_token.reshape(n_tiles, 1, tm), hp,
      w_gate, w_up, w_down, row_weight)


def moe_dispatch(sel_t, comb_t, tm):
    e, t = sel_t.shape
    k = 2
    sel = sel_t.T > 0.5
    seli = sel.astype(jnp.int32)
    counts = jnp.sum(seli, axis=0)
    padded = ((counts + tm - 1) // tm) * tm
    ends = jnp.cumsum(padded)
    starts = ends - padded
    pos = jnp.cumsum(seli, axis=0) - seli
    dest_all = starts[None, :] + pos
    nth = jnp.cumsum(seli, axis=1) * seli
    pick = [(nth == n + 1) for n in range(k)]
    dest = jnp.stack([jnp.sum(jnp.where(p, dest_all, 0), axis=1) for p in pick], axis=1)
    wts = jnp.stack([jnp.sum(jnp.where(p, comb_t.T, 0.0), axis=1) for p in pick], axis=1)
    r_max = k * t + e * tm
    flat = dest.reshape(-1)
    row_token = jnp.zeros((r_max,), jnp.int32).at[flat].set(jnp.repeat(jnp.arange(t, dtype=jnp.int32), k))
    row_weight = jnp.zeros((r_max,), F32).at[flat].set(wts.reshape(-1))
    n_tiles = r_max // tm
    tile_start = jnp.arange(n_tiles, dtype=jnp.int32) * tm
    n_valid = (ends[-1] // tm).astype(jnp.int32)
    tile_expert = jnp.sum((tile_start[:, None] >= ends[None, :]).astype(jnp.int32), axis=1)
    tile_expert = jnp.minimum(tile_expert, e - 1)
    last_valid_expert = tile_expert[jnp.maximum(n_valid - 1, 0)]
    tile_expert = jnp.where(jnp.arange(n_tiles) < n_valid, tile_expert, last_valid_expert)
    return row_token, row_weight.reshape(r_max, 1), tile_expert, n_valid.reshape(1), dest


def _ln_combine_kernel(dest_ref, nxt_ref, x_ref, ys_ref, gate_ref, g_ref, b_ref, sc_ref, sh_ref,
                       xo_ref, h_ref, buf_ref, sem, *, tm, emit_h):
    i = pl.program_id(0)
    slot = i % 2

    def gather(idx_ref, to_slot, wait):
        for k in range(2):
            _gather_rows(lambda r, k=k: idx_ref[0, k, r], ys_ref, buf_ref.at[to_slot, k], sem.at[to_slot], tm,
                         wait=wait)

    @pl.when(i == 0)
    def _():
        gather(dest_ref, 0, False)

    @pl.when(i + 1 < pl.num_programs(0))
    def _():
        gather(nxt_ref, 1 - slot, False)

    gather(dest_ref, slot, True)
    y = buf_ref[slot, 0] + buf_ref[slot, 1]
    z = DN_ALPHA * x_ref[...] + gate_ref[...] * y
    xo = _layer_norm_rows(z, g_ref[...], b_ref[...])
    xo_ref[...] = xo
    if emit_h:
        h_ref[...] = (xo * (1.0 + sc_ref[...]) + sh_ref[...]).astype(h_ref.dtype)
    else:
        h_ref[...] = jnp.zeros(h_ref.shape, h_ref.dtype)


def ln_combine(x1, ys, dest, mod, which_gate, ln_g, ln_b, next_mod, seq, emit_h):
    t, d = x1.shape
    tm = _tile(seq, 256)

    def vec(which):
        return pl.BlockSpec((None, None, 1, d), lambda i: ((i * tm) // seq, which, 0, 0))

    row = pl.BlockSpec((tm, d), lambda i: (i, 0))
    const = lambda shape: pl.BlockSpec(shape, lambda i: (0, 0))
    h_rows = tm if emit_h else 8
    h_spec = pl.BlockSpec((h_rows, d), (lambda i: (i, 0)) if emit_h else (lambda i: (0, 0)))
    n_steps = t // tm
    dest_t = dest.reshape(n_steps, tm, 2).transpose(0, 2, 1)
    xo, h = pl.pallas_call(
        functools.partial(_ln_combine_kernel, tm=tm, emit_h=emit_h),
        grid=(n_steps,),
        in_specs=[pl.BlockSpec((1, 2, tm), lambda i: (i, 0, 0), memory_space=pltpu.SMEM),
                  pl.BlockSpec((1, 2, tm), lambda i: (jnp.minimum(i + 1, n_steps - 1), 0, 0),
                               memory_space=pltpu.SMEM),
                  row, pl.BlockSpec(memory_space=pl.ANY), vec(which_gate), const((1, d)), const((1, d)),
                  pl.BlockSpec((None, None, 1, d), lambda i: ((i * tm) // seq, 1, 0, 0)),
                  pl.BlockSpec((None, None, 1, d), lambda i: ((i * tm) // seq, 0, 0, 0))],
        out_specs=[row, h_spec],
        out_shape=[jax.ShapeDtypeStruct((t, d), F32),
                   jax.ShapeDtypeStruct((t if emit_h else 8, d), BF16)],
        scratch_shapes=[pltpu.VMEM((2, 2, tm, d), F32), pltpu.SemaphoreType.DMA((2,))],
        compiler_params=_cparams(("arbitrary",)),
        name="ln_combine",
    )(dest_t, dest_t, x1, ys, mod, ln_g.reshape(1, d), ln_b.reshape(1, d), next_mod, next_mod)
    return xo, h


MOE_TILE = 512


def _moe_block(x1, hp, sel_t, comb_t, mod, ln_g, ln_b, w_gate, w_up, w_down, next_mod, seq, emit_h):
    tm = min(MOE_TILE, x1.shape[0])
    row_token, row_weight, tile_expert, n_valid, dest = moe_dispatch(sel_t, comb_t, tm)
    ys = moe_ffn(hp, row_token, w_gate.astype(BF16), w_up.astype(BF16), w_down.astype(BF16),
                 row_weight, tile_expert, n_valid, tm)
    return ln_combine(x1, ys, dest, mod, 5, ln_g, ln_b, next_mod, seq, emit_h)


def _flags(n, tn, rope_ranges):
    out = []
    for j in range(n // tn):
        c = j * tn
        out.append(1 if any(lo <= c < hi for lo, hi in rope_ranges) else 0)
    return jnp.asarray(out, jnp.int32)


def kernel(x, c, positions, router_w, router_bias, l0_w_ada, l0_b_ada, l0_w_in, l0_lambda_q1, l0_lambda_k1, l0_lambda_q2, l0_lambda_k2, l0_subln_g, l0_w_out, l0_ln1_g, l0_ln1_b, l0_w_gate, l0_w_up, l0_w_down, l0_ln2_g, l0_ln2_b, l1_w_ada, l1_b_ada, l1_w_in, l1_b_in, l1_sinks, l1_w_out, l1_ln1_g, l1_ln1_b, l1_w_gate, l1_w_up, l1_w_down, l1_ln2_g, l1_ln2_b):
    batch, seq, d = x.shape
    t = batch * seq
    a_heads = d // (4 * HEAD_DIM)
    a_width = a_heads * 2 * HEAD_DIM
    b_heads = d // (2 * HEAD_DIM)
    b_width = b_heads * HEAD_DIM
    c_heads = d // HEAD_DIM
    c_kv = c_heads // 4
    assert l0_w_in.shape[1] == 3 * a_width + 3 * b_width
    assert seq % MOBA_BLOCK == 0 and seq % WINDOW == 0

    x2 = x.reshape(t, d)
    tabs = rope_tables(positions)
    mod0 = ada_modulation(c, l0_w_ada, l0_b_ada)
    mod1 = ada_modulation(c, l1_w_ada, l1_b_ada)
    tn = 1024 if d >= 4096 else 256

    h = modulate(x2, mod0, 0, 1, seq)
    n0 = l0_w_in.shape[1]
    rope0 = [(0, 2 * a_width), (3 * a_width, 3 * a_width + 2 * b_width)]
    proj = projection(h, l0_w_in.astype(BF16), jnp.zeros((n0,), F32), _flags(n0, tn, rope0), tabs, BF16,
                      tn_pref=tn)
    lam_vecs = jnp.stack([l0_lambda_q1, l0_lambda_k1, l0_lambda_q2, l0_lambda_k2]).astype(F32)
    lam_init = 0.8 - 0.6 * math.exp(-0.3 * 0)
    wa = 2 * HEAD_DIM
    o_a = diff_attention(proj, lam_vecs, l0_subln_g, batch=batch, seq=seq, n_heads=a_heads,
                         q_col=0, k_col=a_width // wa, v_col=2 * a_width // wa, lam_init=lam_init)
    bq0 = 3 * a_width // HEAD_DIM
    nbh = b_width // HEAD_DIM
    kmean = moba_kmean(proj, batch=batch, seq=seq, n_heads=b_heads, k_col=bq0 + nbh)
    o_b = moba_attention(proj, kmean, batch=batch, seq=seq, n_heads=b_heads,
                         q_col=bq0, k_col=bq0 + nbh, v_col=bq0 + 2 * nbh)
    o = jnp.concatenate([o_a, o_b], axis=1)
    y = projection(o, l0_w_out.astype(BF16), jnp.zeros((d,), F32), _flags(d, tn, []), tabs, F32, tn_pref=tn)
    x1, hp, sel_t, comb_t = ln_router(x2, y, mod0, 2, 4, 3, l0_ln1_g, l0_ln1_b, router_w, router_bias, seq)
    x2, h = _moe_block(x1, hp, sel_t, comb_t, mod0, l0_ln2_g, l0_ln2_b, l0_w_gate, l0_w_up, l0_w_down,
                       mod1, seq, True)

    n1 = l1_w_in.shape[1]
    cq = c_heads * HEAD_DIM
    ckv = c_kv * HEAD_DIM
    rope1 = [(0, cq + ckv)]
    proj = projection(h, l1_w_in.astype(BF16), l1_b_in, _flags(n1, tn, rope1), tabs, BF16, tn_pref=tn)
    o = swa_attention(proj, l1_sinks, batch=batch, seq=seq, n_q_heads=c_heads, n_kv_heads=c_kv,
                      k_col=cq // HEAD_DIM, v_col=(cq + ckv) // HEAD_DIM)
    y = projection(o, l1_w_out.astype(BF16), jnp.zeros((d,), F32), _flags(d, tn, []), tabs, F32, tn_pref=tn)
    x1, hp, sel_t, comb_t = ln_router(x2, y, mod1, 2, 4, 3, l1_ln1_g, l1_ln1_b, router_w, router_bias, seq)
    x2, _ = _moe_block(x1, hp, sel_t, comb_t, mod1, l1_ln2_g, l1_ln2_b, l1_w_gate, l1_w_up, l1_w_down,
                       mod1, seq, False)
    return x2.reshape(batch, seq, d)
```

```python
import functools
import math

import jax
import jax.numpy as jnp
from jax import lax
from jax.experimental import pallas as pl
from jax.experimental.pallas import tpu as pltpu

F32 = jnp.float32
BF16 = jnp.bfloat16
U32 = jnp.uint32

HEAD_DIM = 128
ROT_DIM = HEAD_DIM // 4
ROT_HALF = ROT_DIM // 2
ROPE_THETA = 500000.0
MOBA_BLOCK = 256
MOBA_TOPK = 3
WINDOW = 128
N_EXPERTS = 16
N_GROUPS = 4
EXPERTS_PER_GROUP = N_EXPERTS // N_GROUPS
DEPTH = 2
DN_ALPHA = float((2 * DEPTH) ** 0.25)
LN_EPS = 1e-5
RMS_EPS = 1e-5
ATTN_SCALE = HEAD_DIM ** -0.5
NEG = -1e30

LANES = 128
V7X_VMEM_BYTES = 64 * 1024 * 1024
VMEM_LIMIT = 56 * 1024 * 1024


def _cparams(semantics):
    return pltpu.CompilerParams(dimension_semantics=semantics, vmem_limit_bytes=VMEM_LIMIT)


def _tile(n, pref):
    t = min(n, pref)
    assert n % t == 0, (n, t)
    return t


def _nt_dot(a, b):
    return lax.dot_general(a, b, (((1,), (1,)), ((), ())), preferred_element_type=F32)


def _ada_kernel(c_ref, w_ref, b_ref, o_ref):
    c = c_ref[...]
    a = (c * jax.nn.sigmoid(c)).astype(BF16)
    o_ref[...] = jnp.dot(a, w_ref[...].astype(BF16), preferred_element_type=F32) + b_ref[...]


def ada_modulation(c, w_ada, b_ada):
    bn, d = c.shape
    n = w_ada.shape[1]
    rows = 8
    cp = jnp.zeros((rows, d), F32).at[:bn].set(c)
    tn = _tile(n, 512)
    out = pl.pallas_call(
        _ada_kernel,
        grid=(n // tn,),
        in_specs=[pl.BlockSpec((rows, d), lambda j: (0, 0)),
                  pl.BlockSpec((d, tn), lambda j: (0, j)),
                  pl.BlockSpec((1, tn), lambda j: (0, j))],
        out_specs=pl.BlockSpec((rows, tn), lambda j: (0, j)),
        out_shape=jax.ShapeDtypeStruct((rows, n), F32),
        compiler_params=_cparams(("arbitrary",)),
        name="ada_modulation",
    )(cp, w_ada, b_ada.reshape(1, n))
    return out[:bn].reshape(bn, 6, 1, d)


def _modulate_kernel(x_ref, sc_ref, sh_ref, o_ref):
    o_ref[...] = (x_ref[...] * (1.0 + sc_ref[...]) + sh_ref[...]).astype(o_ref.dtype)


def modulate(x2, mod, which_shift, which_scale, seq):
    t, d = x2.shape
    tm = _tile(seq, 512)

    def vec(which):
        return pl.BlockSpec((None, None, 1, d), lambda i: ((i * tm) // seq, which, 0, 0))

    return pl.pallas_call(
        _modulate_kernel,
        grid=(t // tm,),
        in_specs=[pl.BlockSpec((tm, d), lambda i: (i, 0)), vec(which_scale), vec(which_shift)],
        out_specs=pl.BlockSpec((tm, d), lambda i: (i, 0)),
        out_shape=jax.ShapeDtypeStruct((t, d), BF16),
        compiler_params=_cparams(("arbitrary",)),
        name="modulate",
    )(x2, mod, mod)


def _proj_kernel(flag_ref, a_ref, w_ref, b_ref, c_ref, s1_ref, s2_ref, o_ref):
    j = pl.program_id(1)
    acc = jnp.dot(a_ref[...], w_ref[...], preferred_element_type=F32) + b_ref[...]
    tn = acc.shape[1]

    @pl.when(flag_ref[j] == 0)
    def _():
        o_ref[...] = acc.astype(o_ref.dtype)

    @pl.when(flag_ref[j] != 0)
    def _():
        cm, s1, s2 = c_ref[...], s1_ref[...], s2_ref[...]
        for hh in range(tn // HEAD_DIM):
            sl = slice(hh * HEAD_DIM, (hh + 1) * HEAD_DIM)
            xh = acc[:, sl]
            up = pltpu.roll(xh, HEAD_DIM - ROT_HALF, axis=1)
            dn = pltpu.roll(xh, ROT_HALF, axis=1)
            o_ref[:, sl] = (xh * cm + up * s1 + dn * s2).astype(o_ref.dtype)


def projection(a, w, bias, rope_flags, rope_tabs, out_dtype, tm_pref=1024, tn_pref=512):
    m, k = a.shape
    n = w.shape[1]
    tm = _tile(m, tm_pref)
    tn = _tile(n, tn_pref)
    assert rope_flags.shape == (n // tn,)
    cm, s1, s2 = rope_tabs
    tab = pl.BlockSpec((tm, HEAD_DIM), lambda i, j, f: (i, 0))
    return pl.pallas_call(
        _proj_kernel,
        grid_spec=pltpu.PrefetchScalarGridSpec(
            num_scalar_prefetch=1,
            grid=(m // tm, n // tn),
            in_specs=[pl.BlockSpec((tm, k), lambda i, j, f: (i, 0)),
                      pl.BlockSpec((k, tn), lambda i, j, f: (0, j)),
                      pl.BlockSpec((1, tn), lambda i, j, f: (0, j)),
                      tab, tab, tab],
            out_specs=pl.BlockSpec((tm, tn), lambda i, j, f: (i, j)),
        ),
        out_shape=jax.ShapeDtypeStruct((m, n), out_dtype),
        compiler_params=_cparams(("arbitrary", "arbitrary")),
        name="projection",
    )(rope_flags, a, w, bias.reshape(1, n).astype(F32), cm, s1, s2)


def rope_tables(positions):
    inv_freq = ROPE_THETA ** (-jnp.arange(0, ROT_DIM, 2, dtype=F32) / ROT_DIM)
    ang = positions.reshape(-1).astype(F32)[:, None] * inv_freq
    cos, sin = jnp.cos(ang), jnp.sin(ang)
    t = ang.shape[0]
    rest = HEAD_DIM - ROT_DIM
    cm = jnp.concatenate([cos, cos, jnp.ones((t, rest), F32)], axis=1)
    s1 = jnp.concatenate([-sin, jnp.zeros((t, HEAD_DIM - ROT_HALF), F32)], axis=1)
    s2 = jnp.concatenate([jnp.zeros((t, ROT_HALF), F32), sin, jnp.zeros((t, rest), F32)], axis=1)
    return cm, s1, s2


def _lane_tile(x, width):
    return x if width == LANES else jnp.concatenate([x] * (width // LANES), axis=1)


def _flash_step(s, v, m_ref, l_ref, acc_ref):
    m_prev = m_ref[...]
    m_new = jnp.maximum(m_prev, jnp.max(s, axis=1, keepdims=True))
    alpha = jnp.exp((m_prev - m_new) * ATTN_SCALE)
    p = jnp.exp((s - _lane_tile(m_new, s.shape[1])) * ATTN_SCALE)
    if l_ref is not None:
        l_ref[...] = alpha * l_ref[...] + jnp.sum(p, axis=1, keepdims=True)
    acc_ref[...] = (_lane_tile(alpha, acc_ref.shape[-1]) * acc_ref[...]
                    + jnp.dot(p.astype(v.dtype), v, preferred_element_type=F32))
    m_ref[...] = m_new


def _flash_init(m_ref, l_ref, acc_ref):
    m_ref[...] = jnp.full(m_ref.shape, NEG, F32)
    if l_ref is not None:
        l_ref[...] = jnp.zeros(l_ref.shape, F32)
    acc_ref[...] = jnp.zeros(acc_ref.shape, F32)


def _causal_mask(tq, tk, key_offset):
    qpos = lax.broadcasted_iota(jnp.int32, (tq, tk), 0)
    kpos = lax.broadcasted_iota(jnp.int32, (tq, tk), 1) + key_offset
    return kpos <= qpos


def _diff_attn_kernel(lam_ref, q_ref, k_ref, v_ref, g_ref, o_ref, m_ref, l_ref, acc_ref,
                      *, tq, lam_init):
    qi = pl.program_id(2)
    _flash_init(m_ref, l_ref, acc_ref)

    def block(j, mask):
        start = pl.multiple_of(j * tq, tq)
        vb = v_ref[pl.ds(start, tq), :]
        for mp in range(2):
            sl = slice(mp * HEAD_DIM, (mp + 1) * HEAD_DIM)
            s = _nt_dot(q_ref[:, sl], k_ref[pl.ds(start, tq), sl])
            if mask is not None:
                s = jnp.where(mask, s, NEG)
            _flash_step(s, vb, m_ref.at[mp], l_ref.at[mp], acc_ref.at[mp])

    def body(j, carry):
        block(j, None)
        return carry

    lax.fori_loop(0, qi, body, 0)
    block(qi, _causal_mask(tq, tq, 0))

    lam_full = (jnp.exp(jnp.sum(lam_ref[0:1, :] * lam_ref[1:2, :], axis=1, keepdims=True))
                - jnp.exp(jnp.sum(lam_ref[2:3, :] * lam_ref[3:4, :], axis=1, keepdims=True)) + lam_init)
    width = acc_ref.shape[-1]
    o = (acc_ref[0] / _lane_tile(l_ref[0], width)
         - lam_full * (acc_ref[1] / _lane_tile(l_ref[1], width)))
    o = o * lax.rsqrt(jnp.mean(o * o, axis=1, keepdims=True) + RMS_EPS) * g_ref[...]
    o_ref[...] = (o * (1.0 - lam_init)).astype(o_ref.dtype)


def diff_attention(proj, lam_vecs, subln_g, *, batch, seq, n_heads, q_col, k_col, v_col, lam_init):
    t = proj.shape[0]
    width = 2 * HEAD_DIM
    tq = _tile(seq, 512)
    nq = seq // tq
    kern = functools.partial(_diff_attn_kernel, tq=tq, lam_init=lam_init)
    return pl.pallas_call(
        kern,
        grid=(batch, n_heads, nq),
        in_specs=[pl.BlockSpec((4, HEAD_DIM), lambda b, h, qi: (0, 0)),
                  pl.BlockSpec((tq, width), lambda b, h, qi: (b * nq + qi, q_col + h)),
                  pl.BlockSpec((seq, width), lambda b, h, qi: (b, k_col + h)),
                  pl.BlockSpec((seq, width), lambda b, h, qi: (b, v_col + h)),
                  pl.BlockSpec((1, width), lambda b, h, qi: (0, 0))],
        out_specs=pl.BlockSpec((tq, width), lambda b, h, qi: (b * nq + qi, h)),
        out_shape=jax.ShapeDtypeStruct((t, n_heads * width), BF16),
        scratch_shapes=[pltpu.VMEM((2, tq, LANES), F32), pltpu.VMEM((2, tq, LANES), F32),
                        pltpu.VMEM((2, tq, width), F32)],
        compiler_params=_cparams(("arbitrary", "arbitrary", "arbitrary")),
        name="diff_attention",
    )(lam_vecs, proj, proj, proj, subln_g.reshape(1, width).astype(F32))


def _kmean_kernel(k_ref, o_ref, *, nb):
    k = k_ref[...].astype(F32)
    o_ref[...] = jnp.mean(k.reshape(nb, MOBA_BLOCK, HEAD_DIM), axis=1)


def moba_kmean(proj, *, batch, seq, n_heads, k_col):
    nb = seq // MOBA_BLOCK
    return pl.pallas_call(
        functools.partial(_kmean_kernel, nb=nb),
        grid=(batch, n_heads),
        in_specs=[pl.BlockSpec((seq, HEAD_DIM), lambda b, h: (b, k_col + h))],
        out_specs=pl.BlockSpec((None, None, nb, HEAD_DIM), lambda b, h: (b, h, 0, 0)),
        out_shape=jax.ShapeDtypeStruct((batch, n_heads, nb, HEAD_DIM), F32),
        compiler_params=_cparams(("arbitrary", "arbitrary")),
        name="moba_kmean",
    )(proj)


def _moba_block_bias(q, km, qi, *, nb, tq):
    km_hi = km.astype(BF16)
    km_lo = (km - km_hi.astype(F32)).astype(BF16)
    gate = _nt_dot(km_hi, q) + _nt_dot(km_lo, q)
    blk = lax.broadcasted_iota(jnp.int32, (nb, tq), 0).astype(F32)
    own = (qi * (tq // MOBA_BLOCK)
           + lax.broadcasted_iota(jnp.int32, (nb, tq), 1) // MOBA_BLOCK).astype(F32)
    valid = blk < own
    g = jnp.where(valid, gate, -jnp.inf)
    visible = blk == own
    for _ in range(MOBA_TOPK):
        best = jnp.max(g, axis=0, keepdims=True)
        cand = jnp.logical_and(g == best, valid)
        first = jnp.min(jnp.where(cand, blk, float(nb)), axis=0, keepdims=True)
        pick = blk == first
        visible = jnp.logical_or(visible, pick)
        valid = jnp.logical_and(valid, jnp.logical_not(pick))
        g = jnp.where(pick, -jnp.inf, g)
    bias = jnp.where(visible, 0.0, NEG)
    if nb < LANES:
        bias = jnp.concatenate([bias, jnp.zeros((LANES - nb, tq), F32)], axis=0)
    return bias.T


MOBA_HEADS_PER_STEP = 2


def _moba_kernel(q_ref, k_ref, v_ref, km_ref, o_ref, kaug_ref, vaug_ref, qaug_ref, m_ref, acc_ref,
                 *, nb, tq):
    qi = pl.program_id(2)
    seq = k_ref.shape[0]
    heads = [slice(hh * HEAD_DIM, (hh + 1) * HEAD_DIM) for hh in range(MOBA_HEADS_PER_STEP)]

    @pl.when(qi == 0)
    def _():
        key_blk = lax.broadcasted_iota(jnp.int32, (seq, LANES), 0) // MOBA_BLOCK
        lane = lax.broadcasted_iota(jnp.int32, (seq, LANES), 1)
        onehot = jnp.where(key_blk == lane, 1.0, 0.0).astype(BF16)
        for hh, sl in enumerate(heads):
            kaug_ref[hh, :, :HEAD_DIM] = k_ref[:, sl]
            kaug_ref[hh, :, HEAD_DIM:] = onehot
            vaug_ref[hh, :, :HEAD_DIM] = v_ref[:, sl]
            vaug_ref[hh, :, HEAD_DIM:] = jnp.ones((seq, LANES), BF16)

    _flash_init(m_ref, None, acc_ref)
    for hh, sl in enumerate(heads):
        q = q_ref[:, sl]
        qaug_ref[hh, :, :HEAD_DIM] = q
        qaug_ref[hh, :, HEAD_DIM:] = _moba_block_bias(q, km_ref[hh], qi, nb=nb, tq=tq).astype(BF16)

    def block(j, mask):
        start = pl.multiple_of(j * tq, tq)
        for hh in range(MOBA_HEADS_PER_STEP):
            s = _nt_dot(qaug_ref[hh], kaug_ref[hh, pl.ds(start, tq), :])
            if mask is not None:
                s = jnp.where(mask, s, NEG)
            _flash_step(s, vaug_ref[hh, pl.ds(start, tq), :], m_ref.at[hh], None, acc_ref.at[hh])

    def body(j, carry):
        block(j, None)
        return carry

    lax.fori_loop(0, qi, body, 0)
    block(qi, _causal_mask(tq, tq, 0))
    for hh, sl in enumerate(heads):
        o_ref[:, sl] = (acc_ref[hh, :, :HEAD_DIM] / acc_ref[hh, :, HEAD_DIM:]).astype(o_ref.dtype)


def moba_attention(proj, kmean, *, batch, seq, n_heads, q_col, k_col, v_col):
    t = proj.shape[0]
    nb = seq // MOBA_BLOCK
    assert nb <= LANES
    tq = _tile(seq, 2 * MOBA_BLOCK)
    assert tq % MOBA_BLOCK == 0
    nq = seq // tq
    hps = MOBA_HEADS_PER_STEP
    assert n_heads % hps == 0 and q_col % hps == 0 and k_col % hps == 0 and v_col % hps == 0
    width = hps * HEAD_DIM
    return pl.pallas_call(
        functools.partial(_moba_kernel, nb=nb, tq=tq),
        grid=(batch, n_heads // hps, nq),
        in_specs=[pl.BlockSpec((tq, width), lambda b, h, qi: (b * nq + qi, q_col // hps + h)),
                  pl.BlockSpec((seq, width), lambda b, h, qi: (b, k_col // hps + h)),
                  pl.BlockSpec((seq, width), lambda b, h, qi: (b, v_col // hps + h)),
                  pl.BlockSpec((None, hps, nb, HEAD_DIM), lambda b, h, qi: (b, h, 0, 0))],
        out_specs=pl.BlockSpec((tq, width), lambda b, h, qi: (b * nq + qi, h)),
        out_shape=jax.ShapeDtypeStruct((t, n_heads * HEAD_DIM), BF16),
        scratch_shapes=[pltpu.VMEM((hps, seq, 2 * HEAD_DIM), BF16),
                        pltpu.VMEM((hps, seq, HEAD_DIM + LANES), BF16),
                        pltpu.VMEM((hps, tq, 2 * HEAD_DIM), BF16),
                        pltpu.VMEM((hps, tq, LANES), F32), pltpu.VMEM((hps, tq, HEAD_DIM + LANES), F32)],
        compiler_params=_cparams(("arbitrary", "arbitrary", "arbitrary")),
        name="moba_attention",
    )(proj, proj, proj, kmean)


def _swa_kernel(sink_ref, q_ref, kc_ref, kp_ref, vc_ref, vp_ref, o_ref, *, group, nsub):
    g = pl.program_id(1)
    i = pl.program_id(2)
    w = WINDOW
    rows = group * w
    qrow = lax.broadcasted_iota(jnp.int32, (rows, 2 * w), 0) % w + w
    kcol = lax.broadcasted_iota(jnp.int32, (rows, 2 * w), 1)
    rel = qrow - kcol
    band = jnp.logical_and(rel >= 0, rel < w)
    head_of_row = lax.broadcasted_iota(jnp.int32, (rows, 1), 0) // w
    sink = jnp.zeros((rows, 1), F32)
    for jh in range(group):
        sink = jnp.where(head_of_row == jh, sink_ref[g * group + jh], sink)

    for n in range(nsub):
        qs = jnp.concatenate(
            [q_ref[n * w:(n + 1) * w, jh * HEAD_DIM:(jh + 1) * HEAD_DIM] for jh in range(group)], axis=0)
        if n == 0:
            kk = jnp.concatenate([kp_ref[...], kc_ref[0:w, :]], axis=0)
            vv = jnp.concatenate([vp_ref[...], vc_ref[0:w, :]], axis=0)
            mask = jnp.logical_and(band, kcol >= jnp.where(i > 0, 0, w))
        else:
            kk = kc_ref[(n - 1) * w:(n + 1) * w, :]
            vv = vc_ref[(n - 1) * w:(n + 1) * w, :]
            mask = band
        s = jnp.where(mask, _nt_dot(qs, kk) * ATTN_SCALE, NEG)
        m = jnp.maximum(jnp.max(s, axis=1, keepdims=True), sink)
        p = jnp.exp(s - m)
        denom = jnp.sum(p, axis=1, keepdims=True) + jnp.exp(sink - m)
        o = jnp.dot(p.astype(vv.dtype), vv, preferred_element_type=F32) / denom
        for jh in range(group):
            o_ref[n * w:(n + 1) * w, jh * HEAD_DIM:(jh + 1) * HEAD_DIM] = (
                o[jh * w:(jh + 1) * w, :].astype(o_ref.dtype))


def swa_attention(proj, sinks, *, batch, seq, n_q_heads, n_kv_heads, k_col, v_col):
    t = proj.shape[0]
    group = n_q_heads // n_kv_heads
    tq = _tile(seq, 4 * WINDOW)
    nsub = tq // WINDOW
    nq = seq // tq
    gw = group * HEAD_DIM

    def prev_map(col):
        def f(b, g, i):
            return (b * (seq // WINDOW) + jnp.maximum(i * nsub - 1, 0), col + g)
        return f

    return pl.pallas_call(
        functools.partial(_swa_kernel, group=group, nsub=nsub),
        grid_spec=pltpu.PrefetchScalarGridSpec(
            num_scalar_prefetch=1,
            grid=(batch, n_kv_heads, nq),
            in_specs=[pl.BlockSpec((tq, gw), lambda b, g, i, s: (b * nq + i, g)),
                      pl.BlockSpec((tq, HEAD_DIM), lambda b, g, i, s: (b * nq + i, k_col + g)),
                      pl.BlockSpec((WINDOW, HEAD_DIM), lambda b, g, i, s: prev_map(k_col)(b, g, i)),
                      pl.BlockSpec((tq, HEAD_DIM), lambda b, g, i, s: (b * nq + i, v_col + g)),
                      pl.BlockSpec((WINDOW, HEAD_DIM), lambda b, g, i, s: prev_map(v_col)(b, g, i))],
            out_specs=pl.BlockSpec((tq, gw), lambda b, g, i, s: (b * nq + i, g)),
        ),
        out_shape=jax.ShapeDtypeStruct((t, n_q_heads * HEAD_DIM), BF16),
        compiler_params=_cparams(("arbitrary", "arbitrary", "arbitrary")),
        name="swa_attention",
    )(sinks.astype(F32), proj, proj, proj, proj, proj)


def _layer_norm_rows(z, g, b):
    mu = jnp.mean(z, axis=1, keepdims=True)
    zc = z - mu
    var = jnp.mean(zc * zc, axis=1, keepdims=True)
    return zc * lax.rsqrt(var + LN_EPS) * g + b


def _pack_bf16_halves(h):
    half = h.shape[1] // 2
    lo = lax.bitcast_convert_type(h[:, :half].astype(BF16).astype(F32), U32)
    hi = lax.bitcast_convert_type(h[:, half:].astype(BF16).astype(F32), U32)
    return jnp.bitwise_or(jnp.right_shift(lo, jnp.uint32(16)), hi)


def _unpack_bf16_halves(w):
    lo = lax.bitcast_convert_type(jnp.left_shift(w, jnp.uint32(16)), F32).astype(BF16)
    hi = lax.bitcast_convert_type(jnp.bitwise_and(w, jnp.uint32(0xFFFF0000)), F32).astype(BF16)
    return lo, hi


def _route(scores, biased):
    e, tm = scores.shape
    epg = EXPERTS_PER_GROUP
    gs = []
    for gi in range(N_GROUPS):
        rows = [biased[gi * epg + r:gi * epg + r + 1, :] for r in range(epg)]
        best = None
        for a in range(epg):
            for b in range(a + 1, epg):
                pair = rows[a] + rows[b]
                best = pair if best is None else jnp.maximum(best, pair)
        gs.append(best)
    gsel = []
    for gi in range(N_GROUPS):
        ok = None
        for go in range(N_GROUPS):
            if go == gi:
                continue
            c = (gs[gi] > gs[go]) if go < gi else (gs[gi] >= gs[go])
            ok = c if ok is None else jnp.logical_and(ok, c)
        gsel.append(ok)
    sel_rows, comb_rows = [], []
    for gi in range(N_GROUPS):
        rows = [biased[gi * epg + r:gi * epg + r + 1, :] for r in range(epg)]
        for r in range(epg):
            rank = jnp.zeros((1, tm), jnp.int32)
            for o in range(epg):
                if o == r:
                    continue
                before = (rows[o] >= rows[r]) if o < r else (rows[o] > rows[r])
                rank = rank + jnp.where(before, 1, 0)
            s = jnp.logical_and(gsel[gi], rank < 2)
            sel_rows.append(jnp.where(s, 1.0, 0.0))
            comb_rows.append(jnp.where(s, scores[gi * epg + r:gi * epg + r + 1, :], 0.0))
    sel = jnp.concatenate(sel_rows, axis=0)
    comb = jnp.concatenate(comb_rows, axis=0)
    comb = comb / jnp.sum(comb, axis=0, keepdims=True)
    return sel, comb


def _ln_router_kernel(x_ref, y_ref, gate_ref, g_ref, b_ref, sc_ref, sh_ref, rwh_ref, rwl_ref, rb_ref,
                      x1_ref, hp_ref, sel_ref, comb_ref):
    z = DN_ALPHA * x_ref[...] + gate_ref[...] * y_ref[...]
    x1 = _layer_norm_rows(z, g_ref[...], b_ref[...])
    x1_ref[...] = x1
    h = x1 * (1.0 + sc_ref[...]) + sh_ref[...]
    hp_ref[...] = _pack_bf16_halves(h)
    h_hi = h.astype(BF16)
    h_lo = (h - h_hi.astype(F32)).astype(BF16)
    logits = _nt_dot(rwh_ref[...], h_hi) + _nt_dot(rwl_ref[...], h_hi) + _nt_dot(rwh_ref[...], h_lo)
    scores = jax.nn.sigmoid(logits)
    sel, comb = _route(scores, scores + rb_ref[...])
    sel_ref[...] = sel
    comb_ref[...] = comb


def ln_router(x2, y, mod, which_gate, which_scale, which_shift, ln_g, ln_b, router_w, router_bias, seq):
    t, d = x2.shape
    tm = _tile(seq, 256)
    e = router_w.shape[1]
    rwt = router_w.T.astype(F32)
    rw_hi = rwt.astype(BF16)
    rw_lo = (rwt - rw_hi.astype(F32)).astype(BF16)

    def vec(which):
        return pl.BlockSpec((None, None, 1, d), lambda i: ((i * tm) // seq, which, 0, 0))

    row = pl.BlockSpec((tm, d), lambda i: (i, 0))
    const = lambda shape: pl.BlockSpec(shape, lambda i: (0, 0))
    return pl.pallas_call(
        _ln_router_kernel,
        grid=(t // tm,),
        in_specs=[row, row, vec(which_gate), const((1, d)), const((1, d)), vec(which_scale), vec(which_shift),
                  const((e, d)), const((e, d)), const((e, 1))],
        out_specs=[row, pl.BlockSpec((tm, d // 2), lambda i: (i, 0)),
                   pl.BlockSpec((e, tm), lambda i: (0, i)), pl.BlockSpec((e, tm), lambda i: (0, i))],
        out_shape=[jax.ShapeDtypeStruct((t, d), F32), jax.ShapeDtypeStruct((t, d // 2), U32),
                   jax.ShapeDtypeStruct((e, t), F32), jax.ShapeDtypeStruct((e, t), F32)],
        compiler_params=_cparams(("arbitrary",)),
        name="ln_router",
    )(x2, y, mod, ln_g.reshape(1, d), ln_b.reshape(1, d), mod, mod, rw_hi, rw_lo,
      router_bias.reshape(e, 1).astype(F32))


DMA_LOOP_UNROLL = 8


def _gather_rows(idx_of_row, src_ref, dst_ref, sem, n_rows, *, wait):
    def step(r, c):
        cp = pltpu.make_async_copy(src_ref.at[pl.ds(idx_of_row(r), 1)], dst_ref.at[pl.ds(r, 1)], sem)
        if wait:
            cp.wait()
        else:
            cp.start()
        return c

    lax.fori_loop(0, n_rows, step, 0, unroll=DMA_LOOP_UNROLL)


def _moe_ffn_kernel(te_ref, nv_ref, idx_ref, nxt_ref, hp_ref, wg_ref, wu_ref, wd_ref, ys_ref,
                    xs_ref, xb_ref, hm_ref, sem):
    i = pl.program_id(0)
    f = pl.program_id(1)
    nff = pl.num_programs(1)
    tm, half = xs_ref.shape[1], xs_ref.shape[2]
    tf = wg_ref.shape[1]

    @pl.when(i < nv_ref[0])
    def _():
        @pl.when(f == 0)
        def _():
            slot = i % 2

            @pl.when(i == 0)
            def _():
                _gather_rows(lambda r: idx_ref[0, 0, r], hp_ref, xs_ref.at[0], sem.at[0], tm, wait=False)

            @pl.when(i + 1 < nv_ref[0])
            def _():
                _gather_rows(lambda r: nxt_ref[0, 0, r], hp_ref, xs_ref.at[1 - slot], sem.at[1 - slot], tm,
                             wait=False)

            _gather_rows(lambda r: idx_ref[0, 0, r], hp_ref, xs_ref.at[slot], sem.at[slot], tm, wait=True)
            lo, hi = _unpack_bf16_halves(xs_ref[slot])
            xb_ref[:, :half] = lo
            xb_ref[:, half:] = hi

        xb = xb_ref[...]
        gt = jnp.dot(xb, wg_ref[...], preferred_element_type=F32)
        up = jnp.dot(xb, wu_ref[...], preferred_element_type=F32)
        hm = (gt * jax.nn.sigmoid(gt) * up).astype(BF16)
        for c in range(hm_ref.shape[1] // tf):
            @pl.when(f == c)
            def _(c=c):
                hm_ref[:, c * tf:(c + 1) * tf] = hm

        @pl.when(f == nff - 1)
        def _():
            ys_ref[...] = jnp.dot(hm_ref[...], wd_ref[...], preferred_element_type=F32)

    @pl.when(jnp.logical_and(i >= nv_ref[0], f == 0))
    def _():
        ys_ref[...] = jnp.zeros(ys_ref.shape, ys_ref.dtype)


def moe_ffn(hp, row_token, w_gate, w_up, w_down, tile_expert, n_valid, tm):
    r = row_token.shape[0]
    half = hp.shape[1]
    e, d, ff = w_gate.shape
    tf = _tile(ff, 256)
    nff = ff // tf
    n_tiles = r // tm

    def f_eff(i, f, nv):
        return jnp.where(i < nv[0], f, nff - 1)

    idx_spec = lambda shift: pl.BlockSpec(
        (1, 1, tm), lambda i, f, te, nv: (jnp.minimum(i + shift, n_tiles - 1), 0, 0), memory_space=pltpu.SMEM)
    return pl.pallas_call(
        _moe_ffn_kernel,
        grid_spec=pltpu.PrefetchScalarGridSpec(
            num_scalar_prefetch=2,
            grid=(n_tiles, nff),
            in_specs=[idx_spec(0), idx_spec(1), pl.BlockSpec(memory_space=pl.ANY),
                      pl.BlockSpec((None, d, tf), lambda i, f, te, nv: (te[i], 0, f_eff(i, f, nv))),
                      pl.BlockSpec((None, d, tf), lambda i, f, te, nv: (te[i], 0, f_eff(i, f, nv))),
                      pl.BlockSpec((None, ff, d), lambda i, f, te, nv: (te[i], 0, 0),
                                   pipeline_mode=pl.Buffered(1))],
            out_specs=pl.BlockSpec((tm, d), lambda i, f, te, nv: (i, 0)),
            scratch_shapes=[pltpu.VMEM((2, tm, half), U32), pltpu.VMEM((tm, d), BF16),
                            pltpu.VMEM((tm, ff), BF16), pltpu.SemaphoreType.DMA((2,))],
        ),
        out_shape=jax.ShapeDtypeStruct((r, d), F32),
        compiler_params=_cparams(("arbitrary", "arbitrary")),
        name="moe_ffn",
    )(tile_expert, n_valid, row_token.reshape(n_tiles, 1, tm), row_token.reshape(n_tiles, 1, tm), hp,
      w_gate, w_up, w_down)


def moe_dispatch(sel_t, comb_t, tm):
    e, t = sel_t.shape
    k = 2
    sel = sel_t.T > 0.5
    seli = sel.astype(jnp.int32)
    counts = jnp.sum(seli, axis=0)
    padded = ((counts + tm - 1) // tm) * tm
    ends = jnp.cumsum(padded)
    starts = ends - padded
    pos = jnp.cumsum(seli, axis=0) - seli
    dest_all = starts[None, :] + pos
    nth = jnp.cumsum(seli, axis=1) * seli
    pick = [(nth == n + 1) for n in range(k)]
    dest = jnp.stack([jnp.sum(jnp.where(p, dest_all, 0), axis=1) for p in pick], axis=1)
    wts = jnp.stack([jnp.sum(jnp.where(p, comb_t.T, 0.0), axis=1) for p in pick], axis=1)
    r_max = k * t + e * tm
    flat = dest.reshape(-1)
    row_token = jnp.zeros((r_max,), jnp.int32).at[flat].set(jnp.repeat(jnp.arange(t, dtype=jnp.int32), k))
    n_tiles = r_max // tm
    tile_start = jnp.arange(n_tiles, dtype=jnp.int32) * tm
    n_valid = (ends[-1] // tm).astype(jnp.int32)
    tile_expert = jnp.sum((tile_start[:, None] >= ends[None, :]).astype(jnp.int32), axis=1)
    tile_expert = jnp.minimum(tile_expert, e - 1)
    last_valid_expert = tile_expert[jnp.maximum(n_valid - 1, 0)]
    tile_expert = jnp.where(jnp.arange(n_tiles) < n_valid, tile_expert, last_valid_expert)
    return row_token, tile_expert, n_valid.reshape(1), dest, wts


def _ln_combine_kernel(dest_ref, nxt_ref, x_ref, w_ref, ys_ref, gate_ref, g_ref, b_ref, sc_ref, sh_ref,
                       xo_ref, h_ref, buf_ref, sem, *, tm, emit_h):
    i = pl.program_id(0)
    slot = i % 2

    def gather(idx_ref, to_slot, wait):
        for k in range(2):
            _gather_rows(lambda r, k=k: idx_ref[0, k, r], ys_ref, buf_ref.at[to_slot, k], sem.at[to_slot], tm,
                         wait=wait)

    @pl.when(i == 0)
    def _():
        gather(dest_ref, 0, False)

    @pl.when(i + 1 < pl.num_programs(0))
    def _():
        gather(nxt_ref, 1 - slot, False)

    gather(dest_ref, slot, True)
    w = w_ref[...]
    y = w[:, 0:1] * buf_ref[slot, 0] + w[:, 1:2] * buf_ref[slot, 1]
    z = DN_ALPHA * x_ref[...] + gate_ref[...] * y
    xo = _layer_norm_rows(z, g_ref[...], b_ref[...])
    xo_ref[...] = xo
    if emit_h:
        h_ref[...] = (xo * (1.0 + sc_ref[...]) + sh_ref[...]).astype(h_ref.dtype)
    else:
        h_ref[...] = jnp.zeros(h_ref.shape, h_ref.dtype)


def ln_combine(x1, ys, dest, wts, mod, which_gate, ln_g, ln_b, next_mod, seq, emit_h):
    t, d = x1.shape
    tm = _tile(seq, 256)

    def vec(which):
        return pl.BlockSpec((None, None, 1, d), lambda i: ((i * tm) // seq, which, 0, 0))

    row = pl.BlockSpec((tm, d), lambda i: (i, 0))
    const = lambda shape: pl.BlockSpec(shape, lambda i: (0, 0))
    h_rows = tm if emit_h else 8
    h_spec = pl.BlockSpec((h_rows, d), (lambda i: (i, 0)) if emit_h else (lambda i: (0, 0)))
    n_steps = t // tm
    dest_t = dest.reshape(n_steps, tm, 2).transpose(0, 2, 1)
    xo, h = pl.pallas_call(
        functools.partial(_ln_combine_kernel, tm=tm, emit_h=emit_h),
        grid=(n_steps,),
        in_specs=[pl.BlockSpec((1, 2, tm), lambda i: (i, 0, 0), memory_space=pltpu.SMEM),
                  pl.BlockSpec((1, 2, tm), lambda i: (jnp.minimum(i + 1, n_steps - 1), 0, 0),
                               memory_space=pltpu.SMEM),
                  row, pl.BlockSpec((tm, 2), lambda i: (i, 0)), pl.BlockSpec(memory_space=pl.ANY),
                  vec(which_gate), const((1, d)), const((1, d)),
                  pl.BlockSpec((None, None, 1, d), lambda i: ((i * tm) // seq, 1, 0, 0)),
                  pl.BlockSpec((None, None, 1, d), lambda i: ((i * tm) // seq, 0, 0, 0))],
        out_specs=[row, h_spec],
        out_shape=[jax.ShapeDtypeStruct((t, d), F32),
                   jax.ShapeDtypeStruct((t if emit_h else 8, d), BF16)],
        scratch_shapes=[pltpu.VMEM((2, 2, tm, d), F32), pltpu.SemaphoreType.DMA((2,))],
        compiler_params=_cparams(("arbitrary",)),
        name="ln_combine",
    )(dest_t, dest_t, x1, wts, ys, mod, ln_g.reshape(1, d), ln_b.reshape(1, d), next_mod, next_mod)
    return xo, h


MOE_TILE = 512


def _moe_block(x1, hp, sel_t, comb_t, mod, ln_g, ln_b, w_gate, w_up, w_down, next_mod, seq, emit_h):
    tm = min(MOE_TILE, x1.shape[0])
    row_token, tile_expert, n_valid, dest, wts = moe_dispatch(sel_t, comb_t, tm)
    ys = moe_ffn(hp, row_token, w_gate.astype(BF16), w_up.astype(BF16), w_down.astype(BF16),
                 tile_expert, n_valid, tm)
    return ln_combine(x1, ys, dest, wts, mod, 5, ln_g, ln_b, next_mod, seq, emit_h)


def _flags(n, tn, rope_ranges):
    out = []
    for j in range(n // tn):
        c = j * tn
        out.append(1 if any(lo <= c < hi for lo, hi in rope_ranges) else 0)
    return jnp.asarray(out, jnp.int32)


def kernel(x, c, positions, router_w, router_bias, l0_w_ada, l0_b_ada, l0_w_in, l0_lambda_q1, l0_lambda_k1, l0_lambda_q2, l0_lambda_k2, l0_subln_g, l0_w_out, l0_ln1_g, l0_ln1_b, l0_w_gate, l0_w_up, l0_w_down, l0_ln2_g, l0_ln2_b, l1_w_ada, l1_b_ada, l1_w_in, l1_b_in, l1_sinks, l1_w_out, l1_ln1_g, l1_ln1_b, l1_w_gate, l1_w_up, l1_w_down, l1_ln2_g, l1_ln2_b):
    batch, seq, d = x.shape
    t = batch * seq
    a_heads = d // (4 * HEAD_DIM)
    a_width = a_heads * 2 * HEAD_DIM
    b_heads = d // (2 * HEAD_DIM)
    b_width = b_heads * HEAD_DIM
    c_heads = d // HEAD_DIM
    c_kv = c_heads // 4
    assert l0_w_in.shape[1] == 3 * a_width + 3 * b_width
    assert seq % MOBA_BLOCK == 0 and seq % WINDOW == 0

    x2 = x.reshape(t, d)
    tabs = rope_tables(positions)
    mod0 = ada_modulation(c, l0_w_ada, l0_b_ada)
    mod1 = ada_modulation(c, l1_w_ada, l1_b_ada)
    tn = 1024 if d >= 4096 else 256

    h = modulate(x2, mod0, 0, 1, seq)
    n0 = l0_w_in.shape[1]
    rope0 = [(0, 2 * a_width), (3 * a_width, 3 * a_width + 2 * b_width)]
    proj = projection(h, l0_w_in.astype(BF16), jnp.zeros((n0,), F32), _flags(n0, tn, rope0), tabs, BF16,
                      tn_pref=tn)
    lam_vecs = jnp.stack([l0_lambda_q1, l0_lambda_k1, l0_lambda_q2, l0_lambda_k2]).astype(F32)
    lam_init = 0.8 - 0.6 * math.exp(-0.3 * 0)
    wa = 2 * HEAD_DIM
    o_a = diff_attention(proj, lam_vecs, l0_subln_g, batch=batch, seq=seq, n_heads=a_heads,
                         q_col=0, k_col=a_width // wa, v_col=2 * a_width // wa, lam_init=lam_init)
    bq0 = 3 * a_width // HEAD_DIM
    nbh = b_width // HEAD_DIM
    kmean = moba_kmean(proj, batch=batch, seq=seq, n_heads=b_heads, k_col=bq0 + nbh)
    o_b = moba_attention(proj, kmean, batch=batch, seq=seq, n_heads=b_heads,
                         q_col=bq0, k_col=bq0 + nbh, v_col=bq0 + 2 * nbh)
    o = jnp.concatenate([o_a, o_b], axis=1)
    y = projection(o, l0_w_out.astype(BF16), jnp.zeros((d,), F32), _flags(d, tn, []), tabs, F32, tn_pref=tn)
    x1, hp, sel_t, comb_t = ln_router(x2, y, mod0, 2, 4, 3, l0_ln1_g, l0_ln1_b, router_w, router_bias, seq)
    x2, h = _moe_block(x1, hp, sel_t, comb_t, mod0, l0_ln2_g, l0_ln2_b, l0_w_gate, l0_w_up, l0_w_down,
                       mod1, seq, True)

    n1 = l1_w_in.shape[1]
    cq = c_heads * HEAD_DIM
    ckv = c_kv * HEAD_DIM
    rope1 = [(0, cq + ckv)]
    proj = projection(h, l1_w_in.astype(BF16), l1_b_in, _flags(n1, tn, rope1), tabs, BF16, tn_pref=tn)
    o = swa_attention(proj, l1_sinks, batch=batch, seq=seq, n_q_heads=c_heads, n_kv_heads=c_kv,
                      k_col=cq // HEAD_DIM, v_col=(cq + ckv) // HEAD_DIM)
    y = projection(o, l1_w_out.astype(BF16), jnp.zeros((d,), F32), _flags(d, tn, []), tabs, F32, tn_pref=tn)
    x1, hp, sel_t, comb_t = ln_router(x2, y, mod1, 2, 4, 3, l1_ln1_g, l1_ln1_b, router_w, router_bias, seq)
    x2, _ = _moe_block(x1, hp, sel_t, comb_t, mod1, l1_ln2_g, l1_ln2_b, l1_w_gate, l1_w_up, l1_w_down,
                       mod1, seq, False)
    return x2.reshape(batch, seq, d)
```

```python
import functools
import math

import jax
import jax.numpy as jnp
from jax import lax
from jax.experimental import pallas as pl
from jax.experimental.pallas import tpu as pltpu

F32 = jnp.float32
BF16 = jnp.bfloat16
U32 = jnp.uint32

HEAD_DIM = 128
ROT_DIM = HEAD_DIM // 4
ROT_HALF = ROT_DIM // 2
ROPE_THETA = 500000.0
MOBA_BLOCK = 256
MOBA_TOPK = 3
WINDOW = 128
N_EXPERTS = 16
N_GROUPS = 4
EXPERTS_PER_GROUP = N_EXPERTS // N_GROUPS
DEPTH = 2
DN_ALPHA = float((2 * DEPTH) ** 0.25)
LN_EPS = 1e-5
RMS_EPS = 1e-5
ATTN_SCALE = HEAD_DIM ** -0.5
NEG = -1e30

LANES = 128
V7X_VMEM_BYTES = 64 * 1024 * 1024
VMEM_LIMIT = 56 * 1024 * 1024


def _cparams(semantics):
    return pltpu.CompilerParams(dimension_semantics=semantics, vmem_limit_bytes=VMEM_LIMIT)


def _tile(n, pref):
    t = min(n, pref)
    assert n % t == 0, (n, t)
    return t


def _nt_dot(a, b):
    return lax.dot_general(a, b, (((1,), (1,)), ((), ())), preferred_element_type=F32)


def _ada_kernel(c_ref, w_ref, b_ref, o_ref):
    c = c_ref[...]
    a = (c * jax.nn.sigmoid(c)).astype(BF16)
    o_ref[...] = jnp.dot(a, w_ref[...].astype(BF16), preferred_element_type=F32) + b_ref[...]


def ada_modulation(c, w_ada, b_ada):
    bn, d = c.shape
    n = w_ada.shape[1]
    rows = 8
    cp = jnp.zeros((rows, d), F32).at[:bn].set(c)
    tn = _tile(n, 512)
    out = pl.pallas_call(
        _ada_kernel,
        grid=(n // tn,),
        in_specs=[pl.BlockSpec((rows, d), lambda j: (0, 0)),
                  pl.BlockSpec((d, tn), lambda j: (0, j)),
                  pl.BlockSpec((1, tn), lambda j: (0, j))],
        out_specs=pl.BlockSpec((rows, tn), lambda j: (0, j)),
        out_shape=jax.ShapeDtypeStruct((rows, n), F32),
        compiler_params=_cparams(("arbitrary",)),
        name="ada_modulation",
    )(cp, w_ada, b_ada.reshape(1, n))
    return out[:bn].reshape(bn, 6, 1, d)


def _modulate_kernel(x_ref, sc_ref, sh_ref, o_ref):
    o_ref[...] = (x_ref[...] * (1.0 + sc_ref[...]) + sh_ref[...]).astype(o_ref.dtype)


def modulate(x2, mod, which_shift, which_scale, seq):
    t, d = x2.shape
    tm = _tile(seq, 512)

    def vec(which):
        return pl.BlockSpec((None, None, 1, d), lambda i: ((i * tm) // seq, which, 0, 0))

    return pl.pallas_call(
        _modulate_kernel,
        grid=(t // tm,),
        in_specs=[pl.BlockSpec((tm, d), lambda i: (i, 0)), vec(which_scale), vec(which_shift)],
        out_specs=pl.BlockSpec((tm, d), lambda i: (i, 0)),
        out_shape=jax.ShapeDtypeStruct((t, d), BF16),
        compiler_params=_cparams(("arbitrary",)),
        name="modulate",
    )(x2, mod, mod)


def _proj_kernel(flag_ref, *refs, n_lhs):
    a_refs = refs[:n_lhs]
    w_ref, b_ref, c_ref, s1_ref, s2_ref, o_ref = refs[n_lhs:]
    j = pl.program_id(1)
    acc = b_ref[...]
    k0 = 0
    for a_ref in a_refs:
        kw = a_ref.shape[1]
        acc = acc + jnp.dot(a_ref[...], w_ref[k0:k0 + kw, :], preferred_element_type=F32)
        k0 += kw
    tn = acc.shape[1]

    @pl.when(flag_ref[j] == 0)
    def _():
        o_ref[...] = acc.astype(o_ref.dtype)

    @pl.when(flag_ref[j] != 0)
    def _():
        cm, s1, s2 = c_ref[...], s1_ref[...], s2_ref[...]
        for hh in range(tn // HEAD_DIM):
            sl = slice(hh * HEAD_DIM, (hh + 1) * HEAD_DIM)
            xh = acc[:, sl]
            up = pltpu.roll(xh, HEAD_DIM - ROT_HALF, axis=1)
            dn = pltpu.roll(xh, ROT_HALF, axis=1)
            o_ref[:, sl] = (xh * cm + up * s1 + dn * s2).astype(o_ref.dtype)


def projection(lhs, w, bias, rope_flags, rope_tabs, out_dtype, tm_pref=1024, tn_pref=512):
    m = lhs[0].shape[0]
    k, n = w.shape
    assert sum(a.shape[1] for a in lhs) == k
    tm = _tile(m, tm_pref)
    tn = _tile(n, tn_pref)
    assert rope_flags.shape == (n // tn,)
    cm, s1, s2 = rope_tabs
    tab = pl.BlockSpec((tm, HEAD_DIM), lambda i, j, f: (i, 0))
    return pl.pallas_call(
        functools.partial(_proj_kernel, n_lhs=len(lhs)),
        grid_spec=pltpu.PrefetchScalarGridSpec(
            num_scalar_prefetch=1,
            grid=(m // tm, n // tn),
            in_specs=[pl.BlockSpec((tm, a.shape[1]), lambda i, j, f: (i, 0)) for a in lhs]
                     + [pl.BlockSpec((k, tn), lambda i, j, f: (0, j)),
                        pl.BlockSpec((1, tn), lambda i, j, f: (0, j)),
                        tab, tab, tab],
            out_specs=pl.BlockSpec((tm, tn), lambda i, j, f: (i, j)),
        ),
        out_shape=jax.ShapeDtypeStruct((m, n), out_dtype),
        compiler_params=_cparams(("arbitrary", "arbitrary")),
        name="projection",
    )(rope_flags, *lhs, w, bias.reshape(1, n).astype(F32), cm, s1, s2)


def rope_tables(positions):
    inv_freq = ROPE_THETA ** (-jnp.arange(0, ROT_DIM, 2, dtype=F32) / ROT_DIM)
    ang = positions.reshape(-1).astype(F32)[:, None] * inv_freq
    cos, sin = jnp.cos(ang), jnp.sin(ang)
    t = ang.shape[0]
    rest = HEAD_DIM - ROT_DIM
    cm = jnp.concatenate([cos, cos, jnp.ones((t, rest), F32)], axis=1)
    s1 = jnp.concatenate([-sin, jnp.zeros((t, HEAD_DIM - ROT_HALF), F32)], axis=1)
    s2 = jnp.concatenate([jnp.zeros((t, ROT_HALF), F32), sin, jnp.zeros((t, rest), F32)], axis=1)
    return cm, s1, s2


ROW_CHUNK = 256


def _lane_tile(x, width):
    return x if width == LANES else jnp.concatenate([x] * (width // LANES), axis=1)


def _flash_step(s, v, m_ref, l_ref, acc_ref):
    m_prev = m_ref[...]
    m_new = jnp.maximum(m_prev, jnp.max(s, axis=1, keepdims=True))
    alpha = jnp.exp((m_prev - m_new) * ATTN_SCALE)
    p = jnp.exp((s - _lane_tile(m_new, s.shape[1])) * ATTN_SCALE)
    if l_ref is not None:
        l_ref[...] = alpha * l_ref[...] + jnp.sum(p, axis=1, keepdims=True)
    acc_ref[...] = (_lane_tile(alpha, acc_ref.shape[-1]) * acc_ref[...]
                    + jnp.dot(p.astype(v.dtype), v, preferred_element_type=F32))
    m_ref[...] = m_new


def _flash_init(m_ref, l_ref, acc_ref):
    m_ref[...] = jnp.full(m_ref.shape, NEG, F32)
    if l_ref is not None:
        l_ref[...] = jnp.zeros(l_ref.shape, F32)
    acc_ref[...] = jnp.zeros(acc_ref.shape, F32)


def _causal_mask(tq, tk, key_offset):
    qpos = lax.broadcasted_iota(jnp.int32, (tq, tk), 0)
    kpos = lax.broadcasted_iota(jnp.int32, (tq, tk), 1) + key_offset
    return kpos <= qpos


def _diff_attn_kernel(lam_ref, q_ref, k_ref, v_ref, g_ref, o_ref, m_ref, l_ref, acc_ref,
                      *, tq, lam_init):
    qi = pl.program_id(2)
    _flash_init(m_ref, l_ref, acc_ref)

    def block(j, mask):
        start = pl.multiple_of(j * tq, tq)
        vb = v_ref[pl.ds(start, tq), :]
        for mp in range(2):
            sl = slice(mp * HEAD_DIM, (mp + 1) * HEAD_DIM)
            kb = k_ref[pl.ds(start, tq), sl]
            for rc in range(tq // ROW_CHUNK):
                rows = slice(rc * ROW_CHUNK, (rc + 1) * ROW_CHUNK)
                s = _nt_dot(q_ref[rows, sl], kb)
                if mask is not None:
                    s = jnp.where(mask[rows], s, NEG)
                _flash_step(s, vb, m_ref.at[mp, rows], l_ref.at[mp, rows], acc_ref.at[mp, rows])

    def body(j, carry):
        block(j, None)
        return carry

    lax.fori_loop(0, qi, body, 0)
    block(qi, _causal_mask(tq, tq, 0))

    lam_full = (jnp.exp(jnp.sum(lam_ref[0:1, :] * lam_ref[1:2, :], axis=1, keepdims=True))
                - jnp.exp(jnp.sum(lam_ref[2:3, :] * lam_ref[3:4, :], axis=1, keepdims=True)) + lam_init)
    width = acc_ref.shape[-1]
    o = (acc_ref[0] / _lane_tile(l_ref[0], width)
         - lam_full * (acc_ref[1] / _lane_tile(l_ref[1], width)))
    o = o * lax.rsqrt(jnp.mean(o * o, axis=1, keepdims=True) + RMS_EPS) * g_ref[...]
    o_ref[...] = (o * (1.0 - lam_init)).astype(o_ref.dtype)


def diff_attention(proj, lam_vecs, subln_g, *, batch, seq, n_heads, q_col, k_col, v_col, lam_init):
    t = proj.shape[0]
    width = 2 * HEAD_DIM
    tq = _tile(seq, 512)
    assert tq % ROW_CHUNK == 0
    nq = seq // tq
    kern = functools.partial(_diff_attn_kernel, tq=tq, lam_init=lam_init)
    return pl.pallas_call(
        kern,
        grid=(batch, n_heads, nq),
        in_specs=[pl.BlockSpec((4, HEAD_DIM), lambda b, h, qi: (0, 0)),
                  pl.BlockSpec((tq, width), lambda b, h, qi: (b * nq + qi, q_col + h)),
                  pl.BlockSpec((seq, width), lambda b, h, qi: (b, k_col + h)),
                  pl.BlockSpec((seq, width), lambda b, h, qi: (b, v_col + h)),
                  pl.BlockSpec((1, width), lambda b, h, qi: (0, 0))],
        out_specs=pl.BlockSpec((tq, width), lambda b, h, qi: (b * nq + qi, h)),
        out_shape=jax.ShapeDtypeStruct((t, n_heads * width), BF16),
        scratch_shapes=[pltpu.VMEM((2, tq, LANES), F32), pltpu.VMEM((2, tq, LANES), F32),
                        pltpu.VMEM((2, tq, width), F32)],
        compiler_params=_cparams(("arbitrary", "arbitrary", "arbitrary")),
        name="diff_attention",
    )(lam_vecs, proj, proj, proj, subln_g.reshape(1, width).astype(F32))


def _kmean_kernel(k_ref, o_ref, *, nb):
    k = k_ref[...].astype(F32)
    o_ref[...] = jnp.mean(k.reshape(nb, MOBA_BLOCK, HEAD_DIM), axis=1)


def moba_kmean(proj, *, batch, seq, n_heads, k_col):
    nb = seq // MOBA_BLOCK
    return pl.pallas_call(
        functools.partial(_kmean_kernel, nb=nb),
        grid=(batch, n_heads),
        in_specs=[pl.BlockSpec((seq, HEAD_DIM), lambda b, h: (b, k_col + h))],
        out_specs=pl.BlockSpec((None, None, nb, HEAD_DIM), lambda b, h: (b, h, 0, 0)),
        out_shape=jax.ShapeDtypeStruct((batch, n_heads, nb, HEAD_DIM), F32),
        compiler_params=_cparams(("arbitrary", "arbitrary")),
        name="moba_kmean",
    )(proj)


def _moba_block_bias(q, km, qi, *, nb, tq):
    km_hi = km.astype(BF16)
    km_lo = (km - km_hi.astype(F32)).astype(BF16)
    gate = _nt_dot(km_hi, q) + _nt_dot(km_lo, q)
    blk = lax.broadcasted_iota(jnp.int32, (nb, tq), 0).astype(F32)
    own = (qi * (tq // MOBA_BLOCK)
           + lax.broadcasted_iota(jnp.int32, (nb, tq), 1) // MOBA_BLOCK).astype(F32)
    valid = blk < own
    g = jnp.where(valid, gate, -jnp.inf)
    visible = blk == own
    for _ in range(MOBA_TOPK):
        best = jnp.max(g, axis=0, keepdims=True)
        cand = jnp.logical_and(g == best, valid)
        first = jnp.min(jnp.where(cand, blk, float(nb)), axis=0, keepdims=True)
        pick = blk == first
        visible = jnp.logical_or(visible, pick)
        valid = jnp.logical_and(valid, jnp.logical_not(pick))
        g = jnp.where(pick, -jnp.inf, g)
    bias = jnp.where(visible, 0.0, NEG)
    if nb < LANES:
        bias = jnp.concatenate([bias, jnp.zeros((LANES - nb, tq), F32)], axis=0)
    return bias.T


MOBA_HEADS_PER_STEP = 2


def _moba_kernel(q_ref, k_ref, v_ref, km_ref, o_ref, kaug_ref, vaug_ref, qaug_ref, m_ref, acc_ref,
                 *, nb, tq):
    qi = pl.program_id(2)
    seq = k_ref.shape[0]
    heads = [slice(hh * HEAD_DIM, (hh + 1) * HEAD_DIM) for hh in range(MOBA_HEADS_PER_STEP)]

    @pl.when(qi == 0)
    def _():
        key_blk = lax.broadcasted_iota(jnp.int32, (seq, LANES), 0) // MOBA_BLOCK
        lane = lax.broadcasted_iota(jnp.int32, (seq, LANES), 1)
        onehot = jnp.where(key_blk == lane, 1.0, 0.0).astype(BF16)
        for hh, sl in enumerate(heads):
            kaug_ref[hh, :, :HEAD_DIM] = k_ref[:, sl]
            kaug_ref[hh, :, HEAD_DIM:] = onehot
            vaug_ref[hh, :, :HEAD_DIM] = v_ref[:, sl]
            vaug_ref[hh, :, HEAD_DIM:] = jnp.ones((seq, LANES), BF16)

    _flash_init(m_ref, None, acc_ref)
    for hh, sl in enumerate(heads):
        q = q_ref[:, sl]
        qaug_ref[hh, :, :HEAD_DIM] = q
        qaug_ref[hh, :, HEAD_DIM:] = _moba_block_bias(q, km_ref[hh], qi, nb=nb, tq=tq).astype(BF16)

    def block(j, mask):
        start = pl.multiple_of(j * tq, tq)
        for hh in range(MOBA_HEADS_PER_STEP):
            kb = kaug_ref[hh, pl.ds(start, tq), :]
            vb = vaug_ref[hh, pl.ds(start, tq), :]
            for rc in range(tq // ROW_CHUNK):
                rows = slice(rc * ROW_CHUNK, (rc + 1) * ROW_CHUNK)
                s = _nt_dot(qaug_ref[hh, rows], kb)
                if mask is not None:
                    s = jnp.where(mask[rows], s, NEG)
                _flash_step(s, vb, m_ref.at[hh, rows], None, acc_ref.at[hh, rows])

    def body(j, carry):
        block(j, None)
        return carry

    lax.fori_loop(0, qi, body, 0)
    block(qi, _causal_mask(tq, tq, 0))
    for hh, sl in enumerate(heads):
        o_ref[:, sl] = (acc_ref[hh, :, :HEAD_DIM] / acc_ref[hh, :, HEAD_DIM:]).astype(o_ref.dtype)


def moba_attention(proj, kmean, *, batch, seq, n_heads, q_col, k_col, v_col):
    t = proj.shape[0]
    nb = seq // MOBA_BLOCK
    assert nb <= LANES
    tq = _tile(seq, 2 * MOBA_BLOCK)
    assert tq % MOBA_BLOCK == 0 and tq % ROW_CHUNK == 0
    nq = seq // tq
    hps = MOBA_HEADS_PER_STEP
    assert n_heads % hps == 0 and q_col % hps == 0 and k_col % hps == 0 and v_col % hps == 0
    width = hps * HEAD_DIM
    return pl.pallas_call(
        functools.partial(_moba_kernel, nb=nb, tq=tq),
        grid=(batch, n_heads // hps, nq),
        in_specs=[pl.BlockSpec((tq, width), lambda b, h, qi: (b * nq + qi, q_col // hps + h)),
                  pl.BlockSpec((seq, width), lambda b, h, qi: (b, k_col // hps + h)),
                  pl.BlockSpec((seq, width), lambda b, h, qi: (b, v_col // hps + h)),
                  pl.BlockSpec((None, hps, nb, HEAD_DIM), lambda b, h, qi: (b, h, 0, 0))],
        out_specs=pl.BlockSpec((tq, width), lambda b, h, qi: (b * nq + qi, h)),
        out_shape=jax.ShapeDtypeStruct((t, n_heads * HEAD_DIM), BF16),
        scratch_shapes=[pltpu.VMEM((hps, seq, 2 * HEAD_DIM), BF16),
                        pltpu.VMEM((hps, seq, HEAD_DIM + LANES), BF16),
                        pltpu.VMEM((hps, tq, 2 * HEAD_DIM), BF16),
                        pltpu.VMEM((hps, tq, LANES), F32), pltpu.VMEM((hps, tq, HEAD_DIM + LANES), F32)],
        compiler_params=_cparams(("arbitrary", "arbitrary", "arbitrary")),
        name="moba_attention",
    )(proj, proj, proj, kmean)


def _swa_kernel(sink_ref, q_ref, kc_ref, kp_ref, vc_ref, vp_ref, o_ref, *, group, nsub):
    g = pl.program_id(1)
    i = pl.program_id(2)
    w = WINDOW
    rows = group * w
    qrow = lax.broadcasted_iota(jnp.int32, (rows, 2 * w), 0) % w + w
    kcol = lax.broadcasted_iota(jnp.int32, (rows, 2 * w), 1)
    rel = qrow - kcol
    band = jnp.logical_and(rel >= 0, rel < w)
    head_of_row = lax.broadcasted_iota(jnp.int32, (rows, 1), 0) // w
    sink = jnp.zeros((rows, 1), F32)
    for jh in range(group):
        sink = jnp.where(head_of_row == jh, sink_ref[g * group + jh], sink)

    for n in range(nsub):
        qs = jnp.concatenate(
            [q_ref[n * w:(n + 1) * w, jh * HEAD_DIM:(jh + 1) * HEAD_DIM] for jh in range(group)], axis=0)
        if n == 0:
            kk = jnp.concatenate([kp_ref[...], kc_ref[0:w, :]], axis=0)
            vv = jnp.concatenate([vp_ref[...], vc_ref[0:w, :]], axis=0)
            mask = jnp.logical_and(band, kcol >= jnp.where(i > 0, 0, w))
        else:
            kk = kc_ref[(n - 1) * w:(n + 1) * w, :]
            vv = vc_ref[(n - 1) * w:(n + 1) * w, :]
            mask = band
        s = jnp.where(mask, _nt_dot(qs, kk) * ATTN_SCALE, NEG)
        m = jnp.maximum(jnp.max(s, axis=1, keepdims=True), sink)
        p = jnp.exp(s - m)
        denom = jnp.sum(p, axis=1, keepdims=True) + jnp.exp(sink - m)
        o = jnp.dot(p.astype(vv.dtype), vv, preferred_element_type=F32) / denom
        for jh in range(group):
            o_ref[n * w:(n + 1) * w, jh * HEAD_DIM:(jh + 1) * HEAD_DIM] = (
                o[jh * w:(jh + 1) * w, :].astype(o_ref.dtype))


def swa_attention(proj, sinks, *, batch, seq, n_q_heads, n_kv_heads, k_col, v_col):
    t = proj.shape[0]
    group = n_q_heads // n_kv_heads
    tq = _tile(seq, 4 * WINDOW)
    nsub = tq // WINDOW
    nq = seq // tq
    gw = group * HEAD_DIM

    def prev_map(col):
        def f(b, g, i):
            return (b * (seq // WINDOW) + jnp.maximum(i * nsub - 1, 0), col + g)
        return f

    return pl.pallas_call(
        functools.partial(_swa_kernel, group=group, nsub=nsub),
        grid_spec=pltpu.PrefetchScalarGridSpec(
            num_scalar_prefetch=1,
            grid=(batch, n_kv_heads, nq),
            in_specs=[pl.BlockSpec((tq, gw), lambda b, g, i, s: (b * nq + i, g)),
                      pl.BlockSpec((tq, HEAD_DIM), lambda b, g, i, s: (b * nq + i, k_col + g)),
                      pl.BlockSpec((WINDOW, HEAD_DIM), lambda b, g, i, s: prev_map(k_col)(b, g, i)),
                      pl.BlockSpec((tq, HEAD_DIM), lambda b, g, i, s: (b * nq + i, v_col + g)),
                      pl.BlockSpec((WINDOW, HEAD_DIM), lambda b, g, i, s: prev_map(v_col)(b, g, i))],
            out_specs=pl.BlockSpec((tq, gw), lambda b, g, i, s: (b * nq + i, g)),
        ),
        out_shape=jax.ShapeDtypeStruct((t, n_q_heads * HEAD_DIM), BF16),
        compiler_params=_cparams(("arbitrary", "arbitrary", "arbitrary")),
        name="swa_attention",
    )(sinks.astype(F32), proj, proj, proj, proj, proj)


def _layer_norm_rows(z, g, b):
    mu = jnp.mean(z, axis=1, keepdims=True)
    zc = z - mu
    var = jnp.mean(zc * zc, axis=1, keepdims=True)
    return zc * lax.rsqrt(var + LN_EPS) * g + b


def _pack_bf16_halves(h):
    half = h.shape[1] // 2
    lo = lax.bitcast_convert_type(h[:, :half].astype(BF16).astype(F32), U32)
    hi = lax.bitcast_convert_type(h[:, half:].astype(BF16).astype(F32), U32)
    return jnp.bitwise_or(jnp.right_shift(lo, jnp.uint32(16)), hi)


def _unpack_bf16_halves(w):
    lo = lax.bitcast_convert_type(jnp.left_shift(w, jnp.uint32(16)), F32).astype(BF16)
    hi = lax.bitcast_convert_type(jnp.bitwise_and(w, jnp.uint32(0xFFFF0000)), F32).astype(BF16)
    return lo, hi


def _route(scores, biased):
    e, tm = scores.shape
    epg = EXPERTS_PER_GROUP
    gs = []
    for gi in range(N_GROUPS):
        rows = [biased[gi * epg + r:gi * epg + r + 1, :] for r in range(epg)]
        best = None
        for a in range(epg):
            for b in range(a + 1, epg):
                pair = rows[a] + rows[b]
                best = pair if best is None else jnp.maximum(best, pair)
        gs.append(best)
    gsel = []
    for gi in range(N_GROUPS):
        ok = None
        for go in range(N_GROUPS):
            if go == gi:
                continue
            c = (gs[gi] > gs[go]) if go < gi else (gs[gi] >= gs[go])
            ok = c if ok is None else jnp.logical_and(ok, c)
        gsel.append(ok)
    sel_rows, comb_rows = [], []
    for gi in range(N_GROUPS):
        rows = [biased[gi * epg + r:gi * epg + r + 1, :] for r in range(epg)]
        for r in range(epg):
            rank = jnp.zeros((1, tm), jnp.int32)
            for o in range(epg):
                if o == r:
                    continue
                before = (rows[o] >= rows[r]) if o < r else (rows[o] > rows[r])
                rank = rank + jnp.where(before, 1, 0)
            s = jnp.logical_and(gsel[gi], rank < 2)
            sel_rows.append(jnp.where(s, 1.0, 0.0))
            comb_rows.append(jnp.where(s, scores[gi * epg + r:gi * epg + r + 1, :], 0.0))
    sel = jnp.concatenate(sel_rows, axis=0)
    comb = jnp.concatenate(comb_rows, axis=0)
    comb = comb / jnp.sum(comb, axis=0, keepdims=True)
    return sel, comb


def _ln_router_kernel(x_ref, y_ref, gate_ref, g_ref, b_ref, sc_ref, sh_ref, rwh_ref, rwl_ref, rb_ref,
                      x1_ref, hp_ref, sel_ref, comb_ref):
    z = DN_ALPHA * x_ref[...] + gate_ref[...] * y_ref[...]
    x1 = _layer_norm_rows(z, g_ref[...], b_ref[...])
    x1_ref[...] = x1
    h = x1 * (1.0 + sc_ref[...]) + sh_ref[...]
    hp_ref[...] = _pack_bf16_halves(h)
    h_hi = h.astype(BF16)
    h_lo = (h - h_hi.astype(F32)).astype(BF16)
    logits = _nt_dot(rwh_ref[...], h_hi) + _nt_dot(rwl_ref[...], h_hi) + _nt_dot(rwh_ref[...], h_lo)
    scores = jax.nn.sigmoid(logits)
    sel, comb = _route(scores, scores + rb_ref[...])
    sel_ref[...] = sel
    comb_ref[...] = comb


def ln_router(x2, y, mod, which_gate, which_scale, which_shift, ln_g, ln_b, router_w, router_bias, seq):
    t, d = x2.shape
    tm = _tile(seq, 256)
    e = router_w.shape[1]
    rwt = router_w.T.astype(F32)
    rw_hi = rwt.astype(BF16)
    rw_lo = (rwt - rw_hi.astype(F32)).astype(BF16)

    def vec(which):
        return pl.BlockSpec((None, None, 1, d), lambda i: ((i * tm) // seq, which, 0, 0))

    row = pl.BlockSpec((tm, d), lambda i: (i, 0))
    const = lambda shape: pl.BlockSpec(shape, lambda i: (0, 0))
    return pl.pallas_call(
        _ln_router_kernel,
        grid=(t // tm,),
        in_specs=[row, row, vec(which_gate), const((1, d)), const((1, d)), vec(which_scale), vec(which_shift),
                  const((e, d)), const((e, d)), const((e, 1))],
        out_specs=[row, pl.BlockSpec((tm, d // 2), lambda i: (i, 0)),
                   pl.BlockSpec((e, tm), lambda i: (0, i)), pl.BlockSpec((e, tm), lambda i: (0, i))],
        out_shape=[jax.ShapeDtypeStruct((t, d), F32), jax.ShapeDtypeStruct((t, d // 2), U32),
                   jax.ShapeDtypeStruct((e, t), F32), jax.ShapeDtypeStruct((e, t), F32)],
        compiler_params=_cparams(("arbitrary",)),
        name="ln_router",
    )(x2, y, mod, ln_g.reshape(1, d), ln_b.reshape(1, d), mod, mod, rw_hi, rw_lo,
      router_bias.reshape(e, 1).astype(F32))


DMA_LOOP_UNROLL = 8


def _gather_rows(idx_of_row, src_ref, dst_ref, sem, n_rows, *, wait):
    def step(r, c):
        cp = pltpu.make_async_copy(src_ref.at[pl.ds(idx_of_row(r), 1)], dst_ref.at[pl.ds(r, 1)], sem)
        if wait:
            cp.wait()
        else:
            cp.start()
        return c

    lax.fori_loop(0, n_rows, step, 0, unroll=DMA_LOOP_UNROLL)


def _moe_ffn_kernel(te_ref, nv_ref, idx_ref, nxt_ref, hp_ref, wg_ref, wu_ref, wd_ref, ys_ref,
                    xs_ref, xb_ref, hm_ref, sem):
    i = pl.program_id(0)
    f = pl.program_id(1)
    nff = pl.num_programs(1)
    tm, half = xs_ref.shape[1], xs_ref.shape[2]
    tf = wg_ref.shape[1]

    @pl.when(i < nv_ref[0])
    def _():
        @pl.when(f == 0)
        def _():
            slot = i % 2

            @pl.when(i == 0)
            def _():
                _gather_rows(lambda r: idx_ref[0, 0, r], hp_ref, xs_ref.at[0], sem.at[0], tm, wait=False)

            @pl.when(i + 1 < nv_ref[0])
            def _():
                _gather_rows(lambda r: nxt_ref[0, 0, r], hp_ref, xs_ref.at[1 - slot], sem.at[1 - slot], tm,
                             wait=False)

            _gather_rows(lambda r: idx_ref[0, 0, r], hp_ref, xs_ref.at[slot], sem.at[slot], tm, wait=True)
            lo, hi = _unpack_bf16_halves(xs_ref[slot])
            xb_ref[:, :half] = lo
            xb_ref[:, half:] = hi

        xb = xb_ref[...]
        gt = jnp.dot(xb, wg_ref[...], preferred_element_type=F32)
        up = jnp.dot(xb, wu_ref[...], preferred_element_type=F32)
        hm = (gt * jax.nn.sigmoid(gt) * up).astype(BF16)
        for c in range(hm_ref.shape[1] // tf):
            @pl.when(f == c)
            def _(c=c):
                hm_ref[:, c * tf:(c + 1) * tf] = hm

        @pl.when(f == nff - 1)
        def _():
            ys_ref[...] = jnp.dot(hm_ref[...], wd_ref[...], preferred_element_type=F32)

    @pl.when(jnp.logical_and(i >= nv_ref[0], f == 0))
    def _():
        ys_ref[...] = jnp.zeros(ys_ref.shape, ys_ref.dtype)


def moe_ffn(hp, row_token, w_gate, w_up, w_down, tile_expert, n_valid, tm):
    r = row_token.shape[0]
    half = hp.shape[1]
    e, d, ff = w_gate.shape
    tf = _tile(ff, 256)
    nff = ff // tf
    n_tiles = r // tm

    def f_eff(i, f, nv):
        return jnp.where(i < nv[0], f, nff - 1)

    idx_spec = lambda shift: pl.BlockSpec(
        (1, 1, tm), lambda i, f, te, nv: (jnp.minimum(i + shift, n_tiles - 1), 0, 0), memory_space=pltpu.SMEM)
    return pl.pallas_call(
        _moe_ffn_kernel,
        grid_spec=pltpu.PrefetchScalarGridSpec(
            num_scalar_prefetch=2,
            grid=(n_tiles, nff),
            in_specs=[idx_spec(0), idx_spec(1), pl.BlockSpec(memory_space=pl.ANY),
                      pl.BlockSpec((None, d, tf), lambda i, f, te, nv: (te[i], 0, f_eff(i, f, nv))),
                      pl.BlockSpec((None, d, tf), lambda i, f, te, nv: (te[i], 0, f_eff(i, f, nv))),
                      pl.BlockSpec((None, ff, d), lambda i, f, te, nv: (te[i], 0, 0),
                                   pipeline_mode=pl.Buffered(1))],
            out_specs=pl.BlockSpec((tm, d), lambda i, f, te, nv: (i, 0)),
            scratch_shapes=[pltpu.VMEM((2, tm, half), U32), pltpu.VMEM((tm, d), BF16),
                            pltpu.VMEM((tm, ff), BF16), pltpu.SemaphoreType.DMA((2,))],
        ),
        out_shape=jax.ShapeDtypeStruct((r, d), F32),
        compiler_params=_cparams(("arbitrary", "arbitrary")),
        name="moe_ffn",
    )(tile_expert, n_valid, row_token.reshape(n_tiles, 1, tm), row_token.reshape(n_tiles, 1, tm), hp,
      w_gate, w_up, w_down)


def moe_dispatch(sel_t, comb_t, tm):
    e, t = sel_t.shape
    k = 2
    sel = sel_t.T > 0.5
    seli = sel.astype(jnp.int32)
    counts = jnp.sum(seli, axis=0)
    padded = ((counts + tm - 1) // tm) * tm
    ends = jnp.cumsum(padded)
    starts = ends - padded
    pos = jnp.cumsum(seli, axis=0) - seli
    dest_all = starts[None, :] + pos
    nth = jnp.cumsum(seli, axis=1) * seli
    pick = [(nth == n + 1) for n in range(k)]
    dest = jnp.stack([jnp.sum(jnp.where(p, dest_all, 0), axis=1) for p in pick], axis=1)
    wts = jnp.stack([jnp.sum(jnp.where(p, comb_t.T, 0.0), axis=1) for p in pick], axis=1)
    r_max = k * t + e * tm
    flat = dest.reshape(-1)
    row_token = jnp.zeros((r_max,), jnp.int32).at[flat].set(jnp.repeat(jnp.arange(t, dtype=jnp.int32), k))
    n_tiles = r_max // tm
    tile_start = jnp.arange(n_tiles, dtype=jnp.int32) * tm
    n_valid = (ends[-1] // tm).astype(jnp.int32)
    tile_expert = jnp.sum((tile_start[:, None] >= ends[None, :]).astype(jnp.int32), axis=1)
    tile_expert = jnp.minimum(tile_expert, e - 1)
    last_valid_expert = tile_expert[jnp.maximum(n_valid - 1, 0)]
    tile_expert = jnp.where(jnp.arange(n_tiles) < n_valid, tile_expert, last_valid_expert)
    return row_token, tile_expert, n_valid.reshape(1), dest, wts


def _ln_combine_kernel(dest_ref, nxt_ref, x_ref, w_ref, ys_ref, gate_ref, g_ref, b_ref, sc_ref, sh_ref,
                       xo_ref, h_ref, buf_ref, sem, *, tm, emit_h):
    i = pl.program_id(0)
    slot = i % 2

    def gather(idx_ref, to_slot, wait):
        for k in range(2):
            _gather_rows(lambda r, k=k: idx_ref[0, k, r], ys_ref, buf_ref.at[to_slot, k], sem.at[to_slot], tm,
                         wait=wait)

    @pl.when(i == 0)
    def _():
        gather(dest_ref, 0, False)

    @pl.when(i + 1 < pl.num_programs(0))
    def _():
        gather(nxt_ref, 1 - slot, False)

    gather(dest_ref, slot, True)
    w = w_ref[...]
    y = w[:, 0:1] * buf_ref[slot, 0] + w[:, 1:2] * buf_ref[slot, 1]
    z = DN_ALPHA * x_ref[...] + gate_ref[...] * y
    xo = _layer_norm_rows(z, g_ref[...], b_ref[...])
    xo_ref[...] = xo
    if emit_h:
        h_ref[...] = (xo * (1.0 + sc_ref[...]) + sh_ref[...]).astype(h_ref.dtype)
    else:
        h_ref[...] = jnp.zeros(h_ref.shape, h_ref.dtype)


def ln_combine(x1, ys, dest, wts, mod, which_gate, ln_g, ln_b, next_mod, seq, emit_h):
    t, d = x1.shape
    tm = _tile(seq, 256)

    def vec(which):
        return pl.BlockSpec((None, None, 1, d), lambda i: ((i * tm) // seq, which, 0, 0))

    row = pl.BlockSpec((tm, d), lambda i: (i, 0))
    const = lambda shape: pl.BlockSpec(shape, lambda i: (0, 0))
    h_rows = tm if emit_h else 8
    h_spec = pl.BlockSpec((h_rows, d), (lambda i: (i, 0)) if emit_h else (lambda i: (0, 0)))
    n_steps = t // tm
    dest_t = dest.reshape(n_steps, tm, 2).transpose(0, 2, 1)
    xo, h = pl.pallas_call(
        functools.partial(_ln_combine_kernel, tm=tm, emit_h=emit_h),
        grid=(n_steps,),
        in_specs=[pl.BlockSpec((1, 2, tm), lambda i: (i, 0, 0), memory_space=pltpu.SMEM),
                  pl.BlockSpec((1, 2, tm), lambda i: (jnp.minimum(i + 1, n_steps - 1), 0, 0),
                               memory_space=pltpu.SMEM),
                  row, pl.BlockSpec((tm, 2), lambda i: (i, 0)), pl.BlockSpec(memory_space=pl.ANY),
                  vec(which_gate), const((1, d)), const((1, d)),
                  pl.BlockSpec((None, None, 1, d), lambda i: ((i * tm) // seq, 1, 0, 0)),
                  pl.BlockSpec((None, None, 1, d), lambda i: ((i * tm) // seq, 0, 0, 0))],
        out_specs=[row, h_spec],
        out_shape=[jax.ShapeDtypeStruct((t, d), F32),
                   jax.ShapeDtypeStruct((t if emit_h else 8, d), BF16)],
        scratch_shapes=[pltpu.VMEM((2, 2, tm, d), F32), pltpu.SemaphoreType.DMA((2,))],
        compiler_params=_cparams(("arbitrary",)),
        name="ln_combine",
    )(dest_t, dest_t, x1, wts, ys, mod, ln_g.reshape(1, d), ln_b.reshape(1, d), next_mod, next_mod)
    return xo, h


MOE_TILE = 512


def _moe_block(x1, hp, sel_t, comb_t, mod, ln_g, ln_b, w_gate, w_up, w_down, next_mod, seq, emit_h):
    tm = min(MOE_TILE, x1.shape[0])
    row_token, tile_expert, n_valid, dest, wts = moe_dispatch(sel_t, comb_t, tm)
    ys = moe_ffn(hp, row_token, w_gate.astype(BF16), w_up.astype(BF16), w_down.astype(BF16),
                 tile_expert, n_valid, tm)
    return ln_combine(x1, ys, dest, wts, mod, 5, ln_g, ln_b, next_mod, seq, emit_h)


def _flags(n, tn, rope_ranges):
    out = []
    for j in range(n // tn):
        c = j * tn
        out.append(1 if any(lo <= c < hi for lo, hi in rope_ranges) else 0)
    return jnp.asarray(out, jnp.int32)


def kernel(x, c, positions, router_w, router_bias, l0_w_ada, l0_b_ada, l0_w_in, l0_lambda_q1, l0_lambda_k1, l0_lambda_q2, l0_lambda_k2, l0_subln_g, l0_w_out, l0_ln1_g, l0_ln1_b, l0_w_gate, l0_w_up, l0_w_down, l0_ln2_g, l0_ln2_b, l1_w_ada, l1_b_ada, l1_w_in, l1_b_in, l1_sinks, l1_w_out, l1_ln1_g, l1_ln1_b, l1_w_gate, l1_w_up, l1_w_down, l1_ln2_g, l1_ln2_b):
    batch, seq, d = x.shape
    t = batch * seq
    a_heads = d // (4 * HEAD_DIM)
    a_width = a_heads * 2 * HEAD_DIM
    b_heads = d // (2 * HEAD_DIM)
    b_width = b_heads * HEAD_DIM
    c_heads = d // HEAD_DIM
    c_kv = c_heads // 4
    assert l0_w_in.shape[1] == 3 * a_width + 3 * b_width
    assert seq % MOBA_BLOCK == 0 and seq % WINDOW == 0

    x2 = x.reshape(t, d)
    tabs = rope_tables(positions)
    mod0 = ada_modulation(c, l0_w_ada, l0_b_ada)
    mod1 = ada_modulation(c, l1_w_ada, l1_b_ada)
    tn = 1024 if d >= 4096 else 256

    h = modulate(x2, mod0, 0, 1, seq)
    n0 = l0_w_in.shape[1]
    rope0 = [(0, 2 * a_width), (3 * a_width, 3 * a_width + 2 * b_width)]
    proj = projection([h], l0_w_in.astype(BF16), jnp.zeros((n0,), F32), _flags(n0, tn, rope0), tabs, BF16,
                      tn_pref=tn)
    lam_vecs = jnp.stack([l0_lambda_q1, l0_lambda_k1, l0_lambda_q2, l0_lambda_k2]).astype(F32)
    lam_init = 0.8 - 0.6 * math.exp(-0.3 * 0)
    wa = 2 * HEAD_DIM
    o_a = diff_attention(proj, lam_vecs, l0_subln_g, batch=batch, seq=seq, n_heads=a_heads,
                         q_col=0, k_col=a_width // wa, v_col=2 * a_width // wa, lam_init=lam_init)
    bq0 = 3 * a_width // HEAD_DIM
    nbh = b_width // HEAD_DIM
    kmean = moba_kmean(proj, batch=batch, seq=seq, n_heads=b_heads, k_col=bq0 + nbh)
    o_b = moba_attention(proj, kmean, batch=batch, seq=seq, n_heads=b_heads,
                         q_col=bq0, k_col=bq0 + nbh, v_col=bq0 + 2 * nbh)
    y = projection([o_a, o_b], l0_w_out.astype(BF16), jnp.zeros((d,), F32), _flags(d, tn, []), tabs, F32,
                   tn_pref=tn)
    x1, hp, sel_t, comb_t = ln_router(x2, y, mod0, 2, 4, 3, l0_ln1_g, l0_ln1_b, router_w, router_bias, seq)
    x2, h = _moe_block(x1, hp, sel_t, comb_t, mod0, l0_ln2_g, l0_ln2_b, l0_w_gate, l0_w_up, l0_w_down,
                       mod1, seq, True)

    n1 = l1_w_in.shape[1]
    cq = c_heads * HEAD_DIM
    ckv = c_kv * HEAD_DIM
    rope1 = [(0, cq + ckv)]
    proj = projection([h], l1_w_in.astype(BF16), l1_b_in, _flags(n1, tn, rope1), tabs, BF16, tn_pref=tn)
    o = swa_attention(proj, l1_sinks, batch=batch, seq=seq, n_q_heads=c_heads, n_kv_heads=c_kv,
                      k_col=cq // HEAD_DIM, v_col=(cq + ckv) // HEAD_DIM)
    y = projection([o], l1_w_out.astype(BF16), jnp.zeros((d,), F32), _flags(d, tn, []), tabs, F32, tn_pref=tn)
    x1, hp, sel_t, comb_t = ln_router(x2, y, mod1, 2, 4, 3, l1_ln1_g, l1_ln1_b, router_w, router_bias, seq)
    x2, _ = _moe_block(x1, hp, sel_t, comb_t, mod1, l1_ln2_g, l1_ln2_b, l1_w_gate, l1_w_up, l1_w_down,
                       mod1, seq, False)
    return x2.reshape(batch, seq, d)
```

```python
import functools
import math

import jax
import jax.numpy as jnp
from jax import lax
from jax.experimental import pallas as pl
from jax.experimental.pallas import tpu as pltpu

F32 = jnp.float32
BF16 = jnp.bfloat16
U32 = jnp.uint32

HEAD_DIM = 128
ROT_DIM = HEAD_DIM // 4
ROT_HALF = ROT_DIM // 2
ROPE_THETA = 500000.0
MOBA_BLOCK = 256
MOBA_TOPK = 3
WINDOW = 128
N_EXPERTS = 16
N_GROUPS = 4
EXPERTS_PER_GROUP = N_EXPERTS // N_GROUPS
DEPTH = 2
DN_ALPHA = float((2 * DEPTH) ** 0.25)
LN_EPS = 1e-5
RMS_EPS = 1e-5
ATTN_SCALE = HEAD_DIM ** -0.5
NEG = -1e30

LANES = 128
V7X_VMEM_BYTES = 64 * 1024 * 1024
VMEM_LIMIT = 56 * 1024 * 1024


def _cparams(semantics):
    return pltpu.CompilerParams(dimension_semantics=semantics, vmem_limit_bytes=VMEM_LIMIT)


def _tile(n, pref):
    t = min(n, pref)
    assert n % t == 0, (n, t)
    return t


def _nt_dot(a, b):
    return lax.dot_general(a, b, (((1,), (1,)), ((), ())), preferred_element_type=F32)


def _ada_kernel(c_ref, w_ref, b_ref, o_ref):
    c = c_ref[...]
    a = (c * jax.nn.sigmoid(c)).astype(BF16)
    o_ref[...] = jnp.dot(a, w_ref[...].astype(BF16), preferred_element_type=F32) + b_ref[...]


def ada_modulation(c, w_ada, b_ada):
    bn, d = c.shape
    n = w_ada.shape[1]
    rows = 8
    cp = jnp.zeros((rows, d), F32).at[:bn].set(c)
    tn = _tile(n, 512)
    out = pl.pallas_call(
        _ada_kernel,
        grid=(n // tn,),
        in_specs=[pl.BlockSpec((rows, d), lambda j: (0, 0)),
                  pl.BlockSpec((d, tn), lambda j: (0, j)),
                  pl.BlockSpec((1, tn), lambda j: (0, j))],
        out_specs=pl.BlockSpec((rows, tn), lambda j: (0, j)),
        out_shape=jax.ShapeDtypeStruct((rows, n), F32),
        compiler_params=_cparams(("arbitrary",)),
        name="ada_modulation",
    )(cp, w_ada, b_ada.reshape(1, n))
    return out[:bn].reshape(bn, 6, 1, d)


def _modulate_kernel(x_ref, sc_ref, sh_ref, o_ref):
    o_ref[...] = (x_ref[...] * (1.0 + sc_ref[...]) + sh_ref[...]).astype(o_ref.dtype)


def modulate(x2, mod, which_shift, which_scale, seq):
    t, d = x2.shape
    tm = _tile(seq, 512)

    def vec(which):
        return pl.BlockSpec((None, None, 1, d), lambda i: ((i * tm) // seq, which, 0, 0))

    return pl.pallas_call(
        _modulate_kernel,
        grid=(t // tm,),
        in_specs=[pl.BlockSpec((tm, d), lambda i: (i, 0)), vec(which_scale), vec(which_shift)],
        out_specs=pl.BlockSpec((tm, d), lambda i: (i, 0)),
        out_shape=jax.ShapeDtypeStruct((t, d), BF16),
        compiler_params=_cparams(("arbitrary",)),
        name="modulate",
    )(x2, mod, mod)


def _proj_kernel(flag_ref, *refs, n_lhs, any_rope):
    a_refs = refs[:n_lhs]
    w_ref, b_ref, c_ref, s1_ref, s2_ref, o_ref = refs[n_lhs:]
    j = pl.program_id(1)
    tm, tn = o_ref.shape
    if any_rope:
        rope_on = jnp.full((tm, HEAD_DIM), flag_ref[j], jnp.int32) != 0
        cm = jnp.where(rope_on, c_ref[...], 1.0)
        s1 = jnp.where(rope_on, s1_ref[...], 0.0)
        s2 = jnp.where(rope_on, s2_ref[...], 0.0)
    for c0 in range(0, tn, PROJ_COL_CHUNK):
        cols = slice(c0, c0 + PROJ_COL_CHUNK)
        acc = b_ref[:, cols]
        k0 = 0
        for a_ref in a_refs:
            kw = a_ref.shape[1]
            acc = acc + jnp.dot(a_ref[...], w_ref[k0:k0 + kw, cols], preferred_element_type=F32)
            k0 += kw
        if not any_rope:
            o_ref[:, cols] = acc.astype(o_ref.dtype)
            continue
        for hh in range(PROJ_COL_CHUNK // HEAD_DIM):
            sl = slice(hh * HEAD_DIM, (hh + 1) * HEAD_DIM)
            xh = acc[:, sl]
            up = pltpu.roll(xh, HEAD_DIM - ROT_HALF, axis=1)
            dn = pltpu.roll(xh, ROT_HALF, axis=1)
            o_ref[:, c0 + hh * HEAD_DIM:c0 + (hh + 1) * HEAD_DIM] = (
                xh * cm + up * s1 + dn * s2).astype(o_ref.dtype)


PROJ_COL_CHUNK = 256


def projection(lhs, w, bias, rope_flags, rope_tabs, out_dtype, any_rope, tm_pref=1024, tn_pref=512):
    m = lhs[0].shape[0]
    k, n = w.shape
    assert sum(a.shape[1] for a in lhs) == k
    tm = _tile(m, tm_pref)
    tn = _tile(n, tn_pref)
    assert rope_flags.shape == (n // tn,) and tn % PROJ_COL_CHUNK == 0
    cm, s1, s2 = rope_tabs
    tab = pl.BlockSpec((tm, HEAD_DIM), lambda i, j, f: (i, 0))
    return pl.pallas_call(
        functools.partial(_proj_kernel, n_lhs=len(lhs), any_rope=any_rope),
        grid_spec=pltpu.PrefetchScalarGridSpec(
            num_scalar_prefetch=1,
            grid=(m // tm, n // tn),
            in_specs=[pl.BlockSpec((tm, a.shape[1]), lambda i, j, f: (i, 0)) for a in lhs]
                     + [pl.BlockSpec((k, tn), lambda i, j, f: (0, j)),
                        pl.BlockSpec((1, tn), lambda i, j, f: (0, j)),
                        tab, tab, tab],
            out_specs=pl.BlockSpec((tm, tn), lambda i, j, f: (i, j)),
        ),
        out_shape=jax.ShapeDtypeStruct((m, n), out_dtype),
        compiler_params=_cparams(("arbitrary", "arbitrary")),
        name="projection",
    )(rope_flags, *lhs, w, bias.reshape(1, n).astype(F32), cm, s1, s2)


def rope_tables(positions):
    inv_freq = ROPE_THETA ** (-jnp.arange(0, ROT_DIM, 2, dtype=F32) / ROT_DIM)
    ang = positions.reshape(-1).astype(F32)[:, None] * inv_freq
    cos, sin = jnp.cos(ang), jnp.sin(ang)
    t = ang.shape[0]
    rest = HEAD_DIM - ROT_DIM
    cm = jnp.concatenate([cos, cos, jnp.ones((t, rest), F32)], axis=1)
    s1 = jnp.concatenate([-sin, jnp.zeros((t, HEAD_DIM - ROT_HALF), F32)], axis=1)
    s2 = jnp.concatenate([jnp.zeros((t, ROT_HALF), F32), sin, jnp.zeros((t, rest), F32)], axis=1)
    return cm, s1, s2


ROW_CHUNK = 256


def _lane_tile(x, width):
    return x if width == LANES else jnp.concatenate([x] * (width // LANES), axis=1)


def _flash_step(s, v, m_ref, l_ref, acc_ref):
    m_prev = m_ref[...]
    m_new = jnp.maximum(m_prev, jnp.max(s, axis=1, keepdims=True))
    alpha = jnp.exp((m_prev - m_new) * ATTN_SCALE)
    p = jnp.exp((s - _lane_tile(m_new, s.shape[1])) * ATTN_SCALE)
    if l_ref is not None:
        l_ref[...] = alpha * l_ref[...] + jnp.sum(p, axis=1, keepdims=True)
    acc_ref[...] = (_lane_tile(alpha, acc_ref.shape[-1]) * acc_ref[...]
                    + jnp.dot(p.astype(v.dtype), v, preferred_element_type=F32))
    m_ref[...] = m_new


def _flash_init(m_ref, l_ref, acc_ref):
    m_ref[...] = jnp.full(m_ref.shape, NEG, F32)
    if l_ref is not None:
        l_ref[...] = jnp.zeros(l_ref.shape, F32)
    acc_ref[...] = jnp.zeros(acc_ref.shape, F32)


def _causal_mask(tq, tk, key_offset):
    qpos = lax.broadcasted_iota(jnp.int32, (tq, tk), 0)
    kpos = lax.broadcasted_iota(jnp.int32, (tq, tk), 1) + key_offset
    return kpos <= qpos


def _diff_attn_kernel(lam_ref, q_ref, k_ref, v_ref, g_ref, o_ref, m_ref, l_ref, acc_ref,
                      *, tq, lam_init):
    qi = pl.program_id(2)
    _flash_init(m_ref, l_ref, acc_ref)

    def block(j, mask):
        start = pl.multiple_of(j * tq, tq)
        vb = v_ref[pl.ds(start, tq), :]
        for mp in range(2):
            sl = slice(mp * HEAD_DIM, (mp + 1) * HEAD_DIM)
            kb = k_ref[pl.ds(start, tq), sl]
            for rc in range(tq // ROW_CHUNK):
                rows = slice(rc * ROW_CHUNK, (rc + 1) * ROW_CHUNK)
                s = _nt_dot(q_ref[rows, sl], kb)
                if mask is not None:
                    s = jnp.where(mask[rows], s, NEG)
                _flash_step(s, vb, m_ref.at[mp, rows], l_ref.at[mp, rows], acc_ref.at[mp, rows])

    def body(j, carry):
        block(j, None)
        return carry

    lax.fori_loop(0, qi, body, 0)
    block(qi, _causal_mask(tq, tq, 0))

    lam_full = (jnp.exp(jnp.sum(lam_ref[0:1, :] * lam_ref[1:2, :], axis=1, keepdims=True))
                - jnp.exp(jnp.sum(lam_ref[2:3, :] * lam_ref[3:4, :], axis=1, keepdims=True)) + lam_init)
    width = acc_ref.shape[-1]
    o = (acc_ref[0] / _lane_tile(l_ref[0], width)
         - lam_full * (acc_ref[1] / _lane_tile(l_ref[1], width)))
    o = o * lax.rsqrt(jnp.mean(o * o, axis=1, keepdims=True) + RMS_EPS) * g_ref[...]
    o_ref[...] = (o * (1.0 - lam_init)).astype(o_ref.dtype)


def diff_attention(proj, lam_vecs, subln_g, *, batch, seq, n_heads, q_col, k_col, v_col, lam_init):
    t = proj.shape[0]
    width = 2 * HEAD_DIM
    tq = _tile(seq, 512)
    assert tq % ROW_CHUNK == 0
    nq = seq // tq
    kern = functools.partial(_diff_attn_kernel, tq=tq, lam_init=lam_init)
    return pl.pallas_call(
        kern,
        grid=(batch, n_heads, nq),
        in_specs=[pl.BlockSpec((4, HEAD_DIM), lambda b, h, qi: (0, 0)),
                  pl.BlockSpec((tq, width), lambda b, h, qi: (b * nq + qi, q_col + h)),
                  pl.BlockSpec((seq, width), lambda b, h, qi: (b, k_col + h)),
                  pl.BlockSpec((seq, width), lambda b, h, qi: (b, v_col + h)),
                  pl.BlockSpec((1, width), lambda b, h, qi: (0, 0))],
        out_specs=pl.BlockSpec((tq, width), lambda b, h, qi: (b * nq + qi, h)),
        out_shape=jax.ShapeDtypeStruct((t, n_heads * width), BF16),
        scratch_shapes=[pltpu.VMEM((2, tq, LANES), F32), pltpu.VMEM((2, tq, LANES), F32),
                        pltpu.VMEM((2, tq, width), F32)],
        compiler_params=_cparams(("arbitrary", "arbitrary", "arbitrary")),
        name="diff_attention",
    )(lam_vecs, proj, proj, proj, subln_g.reshape(1, width).astype(F32))


def _kmean_kernel(k_ref, o_ref, *, nb):
    k = k_ref[...].astype(F32)
    o_ref[...] = jnp.mean(k.reshape(nb, MOBA_BLOCK, HEAD_DIM), axis=1)


def moba_kmean(proj, *, batch, seq, n_heads, k_col):
    nb = seq // MOBA_BLOCK
    return pl.pallas_call(
        functools.partial(_kmean_kernel, nb=nb),
        grid=(batch, n_heads),
        in_specs=[pl.BlockSpec((seq, HEAD_DIM), lambda b, h: (b, k_col + h))],
        out_specs=pl.BlockSpec((None, None, nb, HEAD_DIM), lambda b, h: (b, h, 0, 0)),
        out_shape=jax.ShapeDtypeStruct((batch, n_heads, nb, HEAD_DIM), F32),
        compiler_params=_cparams(("arbitrary", "arbitrary")),
        name="moba_kmean",
    )(proj)


def _moba_block_bias(q, km, qi, *, nb, tq):
    km_hi = km.astype(BF16)
    km_lo = (km - km_hi.astype(F32)).astype(BF16)
    gate = _nt_dot(km_hi, q) + _nt_dot(km_lo, q)
    blk = lax.broadcasted_iota(jnp.int32, (nb, tq), 0).astype(F32)
    own = (qi * (tq // MOBA_BLOCK)
           + lax.broadcasted_iota(jnp.int32, (nb, tq), 1) // MOBA_BLOCK).astype(F32)
    valid = blk < own
    g = jnp.where(valid, gate, -jnp.inf)
    visible = blk == own
    for _ in range(MOBA_TOPK):
        best = jnp.max(g, axis=0, keepdims=True)
        cand = jnp.logical_and(g == best, valid)
        first = jnp.min(jnp.where(cand, blk, float(nb)), axis=0, keepdims=True)
        pick = blk == first
        visible = jnp.logical_or(visible, pick)
        valid = jnp.logical_and(valid, jnp.logical_not(pick))
        g = jnp.where(pick, -jnp.inf, g)
    bias = jnp.where(visible, 0.0, NEG)
    if nb < LANES:
        bias = jnp.concatenate([bias, jnp.zeros((LANES - nb, tq), F32)], axis=0)
    return bias.T


MOBA_HEADS_PER_STEP = 2


def _moba_kernel(q_ref, k_ref, v_ref, km_ref, o_ref, kaug_ref, vaug_ref, qaug_ref, m_ref, acc_ref,
                 *, nb, tq):
    qi = pl.program_id(2)
    seq = k_ref.shape[0]
    heads = [slice(hh * HEAD_DIM, (hh + 1) * HEAD_DIM) for hh in range(MOBA_HEADS_PER_STEP)]

    @pl.when(qi == 0)
    def _():
        key_blk = lax.broadcasted_iota(jnp.int32, (seq, LANES), 0) // MOBA_BLOCK
        lane = lax.broadcasted_iota(jnp.int32, (seq, LANES), 1)
        onehot = jnp.where(key_blk == lane, 1.0, 0.0).astype(BF16)
        for hh, sl in enumerate(heads):
            kaug_ref[hh, :, :HEAD_DIM] = k_ref[:, sl]
            kaug_ref[hh, :, HEAD_DIM:] = onehot
            vaug_ref[hh, :, :HEAD_DIM] = v_ref[:, sl]
            vaug_ref[hh, :, HEAD_DIM:] = jnp.ones((seq, LANES), BF16)

    _flash_init(m_ref, None, acc_ref)
    for hh, sl in enumerate(heads):
        q = q_ref[:, sl]
        qaug_ref[hh, :, :HEAD_DIM] = q
        qaug_ref[hh, :, HEAD_DIM:] = _moba_block_bias(q, km_ref[hh], qi, nb=nb, tq=tq).astype(BF16)

    def block(j, mask):
        start = pl.multiple_of(j * tq, tq)
        for hh in range(MOBA_HEADS_PER_STEP):
            kb = kaug_ref[hh, pl.ds(start, tq), :]
            vb = vaug_ref[hh, pl.ds(start, tq), :]
            for rc in range(tq // ROW_CHUNK):
                rows = slice(rc * ROW_CHUNK, (rc + 1) * ROW_CHUNK)
                s = _nt_dot(qaug_ref[hh, rows], kb)
                if mask is not None:
                    s = jnp.where(mask[rows], s, NEG)
                _flash_step(s, vb, m_ref.at[hh, rows], None, acc_ref.at[hh, rows])

    def body(j, carry):
        block(j, None)
        return carry

    lax.fori_loop(0, qi, body, 0)
    block(qi, _causal_mask(tq, tq, 0))
    for hh, sl in enumerate(heads):
        o_ref[:, sl] = (acc_ref[hh, :, :HEAD_DIM] / acc_ref[hh, :, HEAD_DIM:]).astype(o_ref.dtype)


def moba_attention(proj, kmean, *, batch, seq, n_heads, q_col, k_col, v_col):
    t = proj.shape[0]
    nb = seq // MOBA_BLOCK
    assert nb <= LANES
    tq = _tile(seq, 2 * MOBA_BLOCK)
    assert tq % MOBA_BLOCK == 0 and tq % ROW_CHUNK == 0
    nq = seq // tq
    hps = MOBA_HEADS_PER_STEP
    assert n_heads % hps == 0 and q_col % hps == 0 and k_col % hps == 0 and v_col % hps == 0
    width = hps * HEAD_DIM
    return pl.pallas_call(
        functools.partial(_moba_kernel, nb=nb, tq=tq),
        grid=(batch, n_heads // hps, nq),
        in_specs=[pl.BlockSpec((tq, width), lambda b, h, qi: (b * nq + qi, q_col // hps + h)),
                  pl.BlockSpec((seq, width), lambda b, h, qi: (b, k_col // hps + h)),
                  pl.BlockSpec((seq, width), lambda b, h, qi: (b, v_col // hps + h)),
                  pl.BlockSpec((None, hps, nb, HEAD_DIM), lambda b, h, qi: (b, h, 0, 0))],
        out_specs=pl.BlockSpec((tq, width), lambda b, h, qi: (b * nq + qi, h)),
        out_shape=jax.ShapeDtypeStruct((t, n_heads * HEAD_DIM), BF16),
        scratch_shapes=[pltpu.VMEM((hps, seq, 2 * HEAD_DIM), BF16),
                        pltpu.VMEM((hps, seq, HEAD_DIM + LANES), BF16),
                        pltpu.VMEM((hps, tq, 2 * HEAD_DIM), BF16),
                        pltpu.VMEM((hps, tq, LANES), F32), pltpu.VMEM((hps, tq, HEAD_DIM + LANES), F32)],
        compiler_params=_cparams(("arbitrary", "arbitrary", "arbitrary")),
        name="moba_attention",
    )(proj, proj, proj, kmean)


def _swa_kernel(sink_ref, q_ref, kc_ref, kp_ref, vc_ref, vp_ref, o_ref, *, group, nsub):
    g = pl.program_id(1)
    i = pl.program_id(2)
    w = WINDOW
    rows = group * w
    qrow = lax.broadcasted_iota(jnp.int32, (rows, 2 * w), 0) % w + w
    kcol = lax.broadcasted_iota(jnp.int32, (rows, 2 * w), 1)
    rel = qrow - kcol
    band = jnp.logical_and(rel >= 0, rel < w)
    head_of_row = lax.broadcasted_iota(jnp.int32, (rows, 1), 0) // w
    sink = jnp.zeros((rows, 1), F32)
    for jh in range(group):
        sink = jnp.where(head_of_row == jh, sink_ref[g * group + jh], sink)

    for n in range(nsub):
        qs = jnp.concatenate(
            [q_ref[n * w:(n + 1) * w, jh * HEAD_DIM:(jh + 1) * HEAD_DIM] for jh in range(group)], axis=0)
        if n == 0:
            kk = jnp.concatenate([kp_ref[...], kc_ref[0:w, :]], axis=0)
            vv = jnp.concatenate([vp_ref[...], vc_ref[0:w, :]], axis=0)
            mask = jnp.logical_and(band, kcol >= jnp.where(i > 0, 0, w))
        else:
            kk = kc_ref[(n - 1) * w:(n + 1) * w, :]
            vv = vc_ref[(n - 1) * w:(n + 1) * w, :]
            mask = band
        s = jnp.where(mask, _nt_dot(qs, kk) * ATTN_SCALE, NEG)
        m = jnp.maximum(jnp.max(s, axis=1, keepdims=True), sink)
        p = jnp.exp(s - m)
        denom = jnp.sum(p, axis=1, keepdims=True) + jnp.exp(sink - m)
        o = jnp.dot(p.astype(vv.dtype), vv, preferred_element_type=F32) / denom
        for jh in range(group):
            o_ref[n * w:(n + 1) * w, jh * HEAD_DIM:(jh + 1) * HEAD_DIM] = (
                o[jh * w:(jh + 1) * w, :].astype(o_ref.dtype))


def swa_attention(proj, sinks, *, batch, seq, n_q_heads, n_kv_heads, k_col, v_col):
    t = proj.shape[0]
    group = n_q_heads // n_kv_heads
    tq = _tile(seq, 4 * WINDOW)
    nsub = tq // WINDOW
    nq = seq // tq
    gw = group * HEAD_DIM

    def prev_map(col):
        def f(b, g, i):
            return (b * (seq // WINDOW) + jnp.maximum(i * nsub - 1, 0), col + g)
        return f

    return pl.pallas_call(
        functools.partial(_swa_kernel, group=group, nsub=nsub),
        grid_spec=pltpu.PrefetchScalarGridSpec(
            num_scalar_prefetch=1,
            grid=(batch, n_kv_heads, nq),
            in_specs=[pl.BlockSpec((tq, gw), lambda b, g, i, s: (b * nq + i, g)),
                      pl.BlockSpec((tq, HEAD_DIM), lambda b, g, i, s: (b * nq + i, k_col + g)),
                      pl.BlockSpec((WINDOW, HEAD_DIM), lambda b, g, i, s: prev_map(k_col)(b, g, i)),
                      pl.BlockSpec((tq, HEAD_DIM), lambda b, g, i, s: (b * nq + i, v_col + g)),
                      pl.BlockSpec((WINDOW, HEAD_DIM), lambda b, g, i, s: prev_map(v_col)(b, g, i))],
            out_specs=pl.BlockSpec((tq, gw), lambda b, g, i, s: (b * nq + i, g)),
        ),
        out_shape=jax.ShapeDtypeStruct((t, n_q_heads * HEAD_DIM), BF16),
        compiler_params=_cparams(("arbitrary", "arbitrary", "arbitrary")),
        name="swa_attention",
    )(sinks.astype(F32), proj, proj, proj, proj, proj)


def _layer_norm_rows(z, g, b):
    mu = jnp.mean(z, axis=1, keepdims=True)
    zc = z - mu
    var = jnp.mean(zc * zc, axis=1, keepdims=True)
    return zc * lax.rsqrt(var + LN_EPS) * g + b


def _pack_bf16_halves(h):
    half = h.shape[1] // 2
    lo = lax.bitcast_convert_type(h[:, :half].astype(BF16).astype(F32), U32)
    hi = lax.bitcast_convert_type(h[:, half:].astype(BF16).astype(F32), U32)
    return jnp.bitwise_or(jnp.right_shift(lo, jnp.uint32(16)), hi)


def _unpack_bf16_halves(w):
    lo = lax.bitcast_convert_type(jnp.left_shift(w, jnp.uint32(16)), F32).astype(BF16)
    hi = lax.bitcast_convert_type(jnp.bitwise_and(w, jnp.uint32(0xFFFF0000)), F32).astype(BF16)
    return lo, hi


def _route(scores, biased):
    e, tm = scores.shape
    epg = EXPERTS_PER_GROUP
    gs = []
    for gi in range(N_GROUPS):
        rows = [biased[gi * epg + r:gi * epg + r + 1, :] for r in range(epg)]
        best = None
        for a in range(epg):
            for b in range(a + 1, epg):
                pair = rows[a] + rows[b]
                best = pair if best is None else jnp.maximum(best, pair)
        gs.append(best)
    gsel = []
    for gi in range(N_GROUPS):
        ok = None
        for go in range(N_GROUPS):
            if go == gi:
                continue
            c = (gs[gi] > gs[go]) if go < gi else (gs[gi] >= gs[go])
            ok = c if ok is None else jnp.logical_and(ok, c)
        gsel.append(ok)
    sel_rows, comb_rows = [], []
    for gi in range(N_GROUPS):
        rows = [biased[gi * epg + r:gi * epg + r + 1, :] for r in range(epg)]
        for r in range(epg):
            rank = jnp.zeros((1, tm), jnp.int32)
            for o in range(epg):
                if o == r:
                    continue
                before = (rows[o] >= rows[r]) if o < r else (rows[o] > rows[r])
                rank = rank + jnp.where(before, 1, 0)
            s = jnp.logical_and(gsel[gi], rank < 2)
            sel_rows.append(jnp.where(s, 1.0, 0.0))
            comb_rows.append(jnp.where(s, scores[gi * epg + r:gi * epg + r + 1, :], 0.0))
    sel = jnp.concatenate(sel_rows, axis=0)
    comb = jnp.concatenate(comb_rows, axis=0)
    comb = comb / jnp.sum(comb, axis=0, keepdims=True)
    return sel, comb


def _ln_router_kernel(x_ref, y_ref, gate_ref, g_ref, b_ref, sc_ref, sh_ref, rwh_ref, rwl_ref, rb_ref,
                      x1_ref, hp_ref, sel_ref, comb_ref):
    z = DN_ALPHA * x_ref[...] + gate_ref[...] * y_ref[...]
    x1 = _layer_norm_rows(z, g_ref[...], b_ref[...])
    x1_ref[...] = x1
    h = x1 * (1.0 + sc_ref[...]) + sh_ref[...]
    hp_ref[...] = _pack_bf16_halves(h)
    h_hi = h.astype(BF16)
    h_lo = (h - h_hi.astype(F32)).astype(BF16)
    logits = _nt_dot(rwh_ref[...], h_hi) + _nt_dot(rwl_ref[...], h_hi) + _nt_dot(rwh_ref[...], h_lo)
    scores = jax.nn.sigmoid(logits)
    sel, comb = _route(scores, scores + rb_ref[...])
    sel_ref[...] = sel
    comb_ref[...] = comb


def ln_router(x2, y, mod, which_gate, which_scale, which_shift, ln_g, ln_b, router_w, router_bias, seq):
    t, d = x2.shape
    tm = _tile(seq, 256)
    e = router_w.shape[1]
    rwt = router_w.T.astype(F32)
    rw_hi = rwt.astype(BF16)
    rw_lo = (rwt - rw_hi.astype(F32)).astype(BF16)

    def vec(which):
        return pl.BlockSpec((None, None, 1, d), lambda i: ((i * tm) // seq, which, 0, 0))

    row = pl.BlockSpec((tm, d), lambda i: (i, 0))
    const = lambda shape: pl.BlockSpec(shape, lambda i: (0, 0))
    return pl.pallas_call(
        _ln_router_kernel,
        grid=(t // tm,),
        in_specs=[row, row, vec(which_gate), const((1, d)), const((1, d)), vec(which_scale), vec(which_shift),
                  const((e, d)), const((e, d)), const((e, 1))],
        out_specs=[row, pl.BlockSpec((tm, d // 2), lambda i: (i, 0)),
                   pl.BlockSpec((e, tm), lambda i: (0, i)), pl.BlockSpec((e, tm), lambda i: (0, i))],
        out_shape=[jax.ShapeDtypeStruct((t, d), F32), jax.ShapeDtypeStruct((t, d // 2), U32),
                   jax.ShapeDtypeStruct((e, t), F32), jax.ShapeDtypeStruct((e, t), F32)],
        compiler_params=_cparams(("arbitrary",)),
        name="ln_router",
    )(x2, y, mod, ln_g.reshape(1, d), ln_b.reshape(1, d), mod, mod, rw_hi, rw_lo,
      router_bias.reshape(e, 1).astype(F32))


DMA_LOOP_UNROLL = 8


def _gather_rows(idx_of_row, src_ref, dst_ref, sem, n_rows, *, wait):
    def step(r, c):
        cp = pltpu.make_async_copy(src_ref.at[pl.ds(idx_of_row(r), 1)], dst_ref.at[pl.ds(r, 1)], sem)
        if wait:
            cp.wait()
        else:
            cp.start()
        return c

    lax.fori_loop(0, n_rows, step, 0, unroll=DMA_LOOP_UNROLL)


def _moe_ffn_kernel(te_ref, nv_ref, idx_ref, nxt_ref, hp_ref, wg_ref, wu_ref, wd_ref, ys_ref,
                    xs_ref, xb_ref, hm_ref, sem):
    i = pl.program_id(0)
    f = pl.program_id(1)
    nff = pl.num_programs(1)
    tm, half = xs_ref.shape[1], xs_ref.shape[2]
    tf = wg_ref.shape[1]

    @pl.when(i < nv_ref[0])
    def _():
        @pl.when(f == 0)
        def _():
            slot = i % 2

            @pl.when(i == 0)
            def _():
                _gather_rows(lambda r: idx_ref[0, 0, r], hp_ref, xs_ref.at[0], sem.at[0], tm, wait=False)

            @pl.when(i + 1 < nv_ref[0])
            def _():
                _gather_rows(lambda r: nxt_ref[0, 0, r], hp_ref, xs_ref.at[1 - slot], sem.at[1 - slot], tm,
                             wait=False)

            _gather_rows(lambda r: idx_ref[0, 0, r], hp_ref, xs_ref.at[slot], sem.at[slot], tm, wait=True)
            lo, hi = _unpack_bf16_halves(xs_ref[slot])
            xb_ref[:, :half] = lo
            xb_ref[:, half:] = hi

        xb = xb_ref[...]
        gt = jnp.dot(xb, wg_ref[...], preferred_element_type=F32)
        up = jnp.dot(xb, wu_ref[...], preferred_element_type=F32)
        hm = (gt * jax.nn.sigmoid(gt) * up).astype(BF16)
        for c in range(hm_ref.shape[1] // tf):
            @pl.when(f == c)
            def _(c=c):
                hm_ref[:, c * tf:(c + 1) * tf] = hm

        @pl.when(f == nff - 1)
        def _():
            ys_ref[...] = jnp.dot(hm_ref[...], wd_ref[...], preferred_element_type=F32)

    @pl.when(jnp.logical_and(i >= nv_ref[0], f == 0))
    def _():
        ys_ref[...] = jnp.zeros(ys_ref.shape, ys_ref.dtype)


def moe_ffn(hp, row_token, w_gate, w_up, w_down, tile_expert, n_valid, tm):
    r = row_token.shape[0]
    half = hp.shape[1]
    e, d, ff = w_gate.shape
    tf = _tile(ff, 256)
    nff = ff // tf
    n_tiles = r // tm

    def f_eff(i, f, nv):
        return jnp.where(i < nv[0], f, nff - 1)

    idx_spec = lambda shift: pl.BlockSpec(
        (1, 1, tm), lambda i, f, te, nv: (jnp.minimum(i + shift, n_tiles - 1), 0, 0), memory_space=pltpu.SMEM)
    return pl.pallas_call(
        _moe_ffn_kernel,
        grid_spec=pltpu.PrefetchScalarGridSpec(
            num_scalar_prefetch=2,
            grid=(n_tiles, nff),
            in_specs=[idx_spec(0), idx_spec(1), pl.BlockSpec(memory_space=pl.ANY),
                      pl.BlockSpec((None, d, tf), lambda i, f, te, nv: (te[i], 0, f_eff(i, f, nv))),
                      pl.BlockSpec((None, d, tf), lambda i, f, te, nv: (te[i], 0, f_eff(i, f, nv))),
                      pl.BlockSpec((None, ff, d), lambda i, f, te, nv: (te[i], 0, 0),
                                   pipeline_mode=pl.Buffered(1))],
            out_specs=pl.BlockSpec((tm, d), lambda i, f, te, nv: (i, 0)),
            scratch_shapes=[pltpu.VMEM((2, tm, half), U32), pltpu.VMEM((tm, d), BF16),
                            pltpu.VMEM((tm, ff), BF16), pltpu.SemaphoreType.DMA((2,))],
        ),
        out_shape=jax.ShapeDtypeStruct((r, d), F32),
        compiler_params=_cparams(("arbitrary", "arbitrary")),
        name="moe_ffn",
    )(tile_expert, n_valid, row_token.reshape(n_tiles, 1, tm), row_token.reshape(n_tiles, 1, tm), hp,
      w_gate, w_up, w_down)


def moe_dispatch(sel_t, comb_t, tm):
    e, t = sel_t.shape
    k = 2
    sel = sel_t.T > 0.5
    seli = sel.astype(jnp.int32)
    counts = jnp.sum(seli, axis=0)
    padded = ((counts + tm - 1) // tm) * tm
    ends = jnp.cumsum(padded)
    starts = ends - padded
    pos = jnp.cumsum(seli, axis=0) - seli
    dest_all = starts[None, :] + pos
    nth = jnp.cumsum(seli, axis=1) * seli
    pick = [(nth == n + 1) for n in range(k)]
    dest = jnp.stack([jnp.sum(jnp.where(p, dest_all, 0), axis=1) for p in pick], axis=1)
    wts = jnp.stack([jnp.sum(jnp.where(p, comb_t.T, 0.0), axis=1) for p in pick], axis=1)
    r_max = k * t + e * tm
    flat = dest.reshape(-1)
    row_token = jnp.zeros((r_max,), jnp.int32).at[flat].set(jnp.repeat(jnp.arange(t, dtype=jnp.int32), k))
    n_tiles = r_max // tm
    tile_start = jnp.arange(n_tiles, dtype=jnp.int32) * tm
    n_valid = (ends[-1] // tm).astype(jnp.int32)
    tile_expert = jnp.sum((tile_start[:, None] >= ends[None, :]).astype(jnp.int32), axis=1)
    tile_expert = jnp.minimum(tile_expert, e - 1)
    last_valid_expert = tile_expert[jnp.maximum(n_valid - 1, 0)]
    tile_expert = jnp.where(jnp.arange(n_tiles) < n_valid, tile_expert, last_valid_expert)
    return row_token, tile_expert, n_valid.reshape(1), dest, wts


def _ln_combine_kernel(dest_ref, nxt_ref, x_ref, w_ref, ys_ref, gate_ref, g_ref, b_ref, sc_ref, sh_ref,
                       xo_ref, h_ref, buf_ref, sem, *, tm, emit_h):
    i = pl.program_id(0)
    slot = i % 2

    def gather(idx_ref, to_slot, wait):
        for k in range(2):
            _gather_rows(lambda r, k=k: idx_ref[0, k, r], ys_ref, buf_ref.at[to_slot, k], sem.at[to_slot], tm,
                         wait=wait)

    @pl.when(i == 0)
    def _():
        gather(dest_ref, 0, False)

    @pl.when(i + 1 < pl.num_programs(0))
    def _():
        gather(nxt_ref, 1 - slot, False)

    gather(dest_ref, slot, True)
    w = w_ref[...]
    y = w[:, 0:1] * buf_ref[slot, 0] + w[:, 1:2] * buf_ref[slot, 1]
    z = DN_ALPHA * x_ref[...] + gate_ref[...] * y
    xo = _layer_norm_rows(z, g_ref[...], b_ref[...])
    xo_ref[...] = xo
    if emit_h:
        h_ref[...] = (xo * (1.0 + sc_ref[...]) + sh_ref[...]).astype(h_ref.dtype)
    else:
        h_ref[...] = jnp.zeros(h_ref.shape, h_ref.dtype)


def ln_combine(x1, ys, dest, wts, mod, which_gate, ln_g, ln_b, next_mod, seq, emit_h):
    t, d = x1.shape
    tm = _tile(seq, 256)

    def vec(which):
        return pl.BlockSpec((None, None, 1, d), lambda i: ((i * tm) // seq, which, 0, 0))

    row = pl.BlockSpec((tm, d), lambda i: (i, 0))
    const = lambda shape: pl.BlockSpec(shape, lambda i: (0, 0))
    h_rows = tm if emit_h else 8
    h_spec = pl.BlockSpec((h_rows, d), (lambda i: (i, 0)) if emit_h else (lambda i: (0, 0)))
    n_steps = t // tm
    dest_t = dest.reshape(n_steps, tm, 2).transpose(0, 2, 1)
    xo, h = pl.pallas_call(
        functools.partial(_ln_combine_kernel, tm=tm, emit_h=emit_h),
        grid=(n_steps,),
        in_specs=[pl.BlockSpec((1, 2, tm), lambda i: (i, 0, 0), memory_space=pltpu.SMEM),
                  pl.BlockSpec((1, 2, tm), lambda i: (jnp.minimum(i + 1, n_steps - 1), 0, 0),
                               memory_space=pltpu.SMEM),
                  row, pl.BlockSpec((tm, 2), lambda i: (i, 0)), pl.BlockSpec(memory_space=pl.ANY),
                  vec(which_gate), const((1, d)), const((1, d)),
                  pl.BlockSpec((None, None, 1, d), lambda i: ((i * tm) // seq, 1, 0, 0)),
                  pl.BlockSpec((None, None, 1, d), lambda i: ((i * tm) // seq, 0, 0, 0))],
        out_specs=[row, h_spec],
        out_shape=[jax.ShapeDtypeStruct((t, d), F32),
                   jax.ShapeDtypeStruct((t if emit_h else 8, d), BF16)],
        scratch_shapes=[pltpu.VMEM((2, 2, tm, d), F32), pltpu.SemaphoreType.DMA((2,))],
        compiler_params=_cparams(("arbitrary",)),
        name="ln_combine",
    )(dest_t, dest_t, x1, wts, ys, mod, ln_g.reshape(1, d), ln_b.reshape(1, d), next_mod, next_mod)
    return xo, h


MOE_TILE = 512


def _moe_block(x1, hp, sel_t, comb_t, mod, ln_g, ln_b, w_gate, w_up, w_down, next_mod, seq, emit_h):
    tm = min(MOE_TILE, x1.shape[0])
    row_token, tile_expert, n_valid, dest, wts = moe_dispatch(sel_t, comb_t, tm)
    ys = moe_ffn(hp, row_token, w_gate.astype(BF16), w_up.astype(BF16), w_down.astype(BF16),
                 tile_expert, n_valid, tm)
    return ln_combine(x1, ys, dest, wts, mod, 5, ln_g, ln_b, next_mod, seq, emit_h)


def _flags(n, tn, rope_ranges):
    out = []
    for j in range(n // tn):
        c = j * tn
        out.append(1 if any(lo <= c < hi for lo, hi in rope_ranges) else 0)
    return jnp.asarray(out, jnp.int32)


def kernel(x, c, positions, router_w, router_bias, l0_w_ada, l0_b_ada, l0_w_in, l0_lambda_q1, l0_lambda_k1, l0_lambda_q2, l0_lambda_k2, l0_subln_g, l0_w_out, l0_ln1_g, l0_ln1_b, l0_w_gate, l0_w_up, l0_w_down, l0_ln2_g, l0_ln2_b, l1_w_ada, l1_b_ada, l1_w_in, l1_b_in, l1_sinks, l1_w_out, l1_ln1_g, l1_ln1_b, l1_w_gate, l1_w_up, l1_w_down, l1_ln2_g, l1_ln2_b):
    batch, seq, d = x.shape
    t = batch * seq
    a_heads = d // (4 * HEAD_DIM)
    a_width = a_heads * 2 * HEAD_DIM
    b_heads = d // (2 * HEAD_DIM)
    b_width = b_heads * HEAD_DIM
    c_heads = d // HEAD_DIM
    c_kv = c_heads // 4
    assert l0_w_in.shape[1] == 3 * a_width + 3 * b_width
    assert seq % MOBA_BLOCK == 0 and seq % WINDOW == 0

    x2 = x.reshape(t, d)
    tabs = rope_tables(positions)
    mod0 = ada_modulation(c, l0_w_ada, l0_b_ada)
    mod1 = ada_modulation(c, l1_w_ada, l1_b_ada)
    tn = 1024 if d >= 4096 else 256

    h = modulate(x2, mod0, 0, 1, seq)
    n0 = l0_w_in.shape[1]
    rope0 = [(0, 2 * a_width), (3 * a_width, 3 * a_width + 2 * b_width)]
    proj = projection([h], l0_w_in.astype(BF16), jnp.zeros((n0,), F32), _flags(n0, tn, rope0), tabs, BF16,
                      True, tn_pref=tn)
    lam_vecs = jnp.stack([l0_lambda_q1, l0_lambda_k1, l0_lambda_q2, l0_lambda_k2]).astype(F32)
    lam_init = 0.8 - 0.6 * math.exp(-0.3 * 0)
    wa = 2 * HEAD_DIM
    o_a = diff_attention(proj, lam_vecs, l0_subln_g, batch=batch, seq=seq, n_heads=a_heads,
                         q_col=0, k_col=a_width // wa, v_col=2 * a_width // wa, lam_init=lam_init)
    bq0 = 3 * a_width // HEAD_DIM
    nbh = b_width // HEAD_DIM
    kmean = moba_kmean(proj, batch=batch, seq=seq, n_heads=b_heads, k_col=bq0 + nbh)
    o_b = moba_attention(proj, kmean, batch=batch, seq=seq, n_heads=b_heads,
                         q_col=bq0, k_col=bq0 + nbh, v_col=bq0 + 2 * nbh)
    y = projection([o_a, o_b], l0_w_out.astype(BF16), jnp.zeros((d,), F32), _flags(d, tn, []), tabs, F32,
                   False, tn_pref=tn)
    x1, hp, sel_t, comb_t = ln_router(x2, y, mod0, 2, 4, 3, l0_ln1_g, l0_ln1_b, router_w, router_bias, seq)
    x2, h = _moe_block(x1, hp, sel_t, comb_t, mod0, l0_ln2_g, l0_ln2_b, l0_w_gate, l0_w_up, l0_w_down,
                       mod1, seq, True)

    n1 = l1_w_in.shape[1]
    cq = c_heads * HEAD_DIM
    ckv = c_kv * HEAD_DIM
    rope1 = [(0, cq + ckv)]
    proj = projection([h], l1_w_in.astype(BF16), l1_b_in, _flags(n1, tn, rope1), tabs, BF16, True, tn_pref=tn)
    o = swa_attention(proj, l1_sinks, batch=batch, seq=seq, n_q_heads=c_heads, n_kv_heads=c_kv,
                      k_col=cq // HEAD_DIM, v_col=(cq + ckv) // HEAD_DIM)
    y = projection([o], l1_w_out.astype(BF16), jnp.zeros((d,), F32), _flags(d, tn, []), tabs, F32, False,
                   tn_pref=tn)
    x1, hp, sel_t, comb_t = ln_router(x2, y, mod1, 2, 4, 3, l1_ln1_g, l1_ln1_b, router_w, router_bias, seq)
    x2, _ = _moe_block(x1, hp, sel_t, comb_t, mod1, l1_ln2_g, l1_ln2_b, l1_w_gate, l1_w_up, l1_w_down,
                       mod1, seq, False)
    return x2.reshape(batch, seq, d)
```

```python
import functools
import math

import jax
import jax.numpy as jnp
from jax import lax
from jax.experimental import pallas as pl
from jax.experimental.pallas import tpu as pltpu

F32 = jnp.float32
BF16 = jnp.bfloat16
U32 = jnp.uint32

HEAD_DIM = 128
ROT_DIM = HEAD_DIM // 4
ROT_HALF = ROT_DIM // 2
ROPE_THETA = 500000.0
MOBA_BLOCK = 256
MOBA_TOPK = 3
WINDOW = 128
N_EXPERTS = 16
N_GROUPS = 4
EXPERTS_PER_GROUP = N_EXPERTS // N_GROUPS
DEPTH = 2
DN_ALPHA = float((2 * DEPTH) ** 0.25)
LN_EPS = 1e-5
RMS_EPS = 1e-5
ATTN_SCALE = HEAD_DIM ** -0.5
SCALE_LOG2E = ATTN_SCALE * math.log2(math.e)
NEG = -1e30

LANES = 128
V7X_VMEM_BYTES = 64 * 1024 * 1024
VMEM_LIMIT = 56 * 1024 * 1024


def _cparams(semantics):
    return pltpu.CompilerParams(dimension_semantics=semantics, vmem_limit_bytes=VMEM_LIMIT)


def _tile(n, pref):
    t = min(n, pref)
    assert n % t == 0, (n, t)
    return t


def _nt_dot(a, b):
    return lax.dot_general(a, b, (((1,), (1,)), ((), ())), preferred_element_type=F32)


def _ada_kernel(c_ref, w_ref, b_ref, o_ref):
    c = c_ref[...]
    a = (c * jax.nn.sigmoid(c)).astype(BF16)
    o_ref[...] = jnp.dot(a, w_ref[...].astype(BF16), preferred_element_type=F32) + b_ref[...]


def ada_modulation(c, w_ada, b_ada):
    bn, d = c.shape
    n = w_ada.shape[1]
    rows = 8
    cp = jnp.zeros((rows, d), F32).at[:bn].set(c)
    tn = _tile(n, 512)
    out = pl.pallas_call(
        _ada_kernel,
        grid=(n // tn,),
        in_specs=[pl.BlockSpec((rows, d), lambda j: (0, 0)),
                  pl.BlockSpec((d, tn), lambda j: (0, j)),
                  pl.BlockSpec((1, tn), lambda j: (0, j))],
        out_specs=pl.BlockSpec((rows, tn), lambda j: (0, j)),
        out_shape=jax.ShapeDtypeStruct((rows, n), F32),
        compiler_params=_cparams(("arbitrary",)),
        name="ada_modulation",
    )(cp, w_ada, b_ada.reshape(1, n))
    return out[:bn].reshape(bn, 6, 1, d)


def _modulate_kernel(x_ref, sc_ref, sh_ref, o_ref):
    o_ref[...] = (x_ref[...] * (1.0 + sc_ref[...]) + sh_ref[...]).astype(o_ref.dtype)


def modulate(x2, mod, which_shift, which_scale, seq):
    t, d = x2.shape
    tm = _tile(seq, 512)

    def vec(which):
        return pl.BlockSpec((None, None, 1, d), lambda i: ((i * tm) // seq, which, 0, 0))

    return pl.pallas_call(
        _modulate_kernel,
        grid=(t // tm,),
        in_specs=[pl.BlockSpec((tm, d), lambda i: (i, 0)), vec(which_scale), vec(which_shift)],
        out_specs=pl.BlockSpec((tm, d), lambda i: (i, 0)),
        out_shape=jax.ShapeDtypeStruct((t, d), BF16),
        compiler_params=_cparams(("arbitrary",)),
        name="modulate",
    )(x2, mod, mod)


def _proj_kernel(flag_ref, *refs, n_lhs, any_rope):
    a_refs = refs[:n_lhs]
    w_ref, b_ref, c_ref, s1_ref, s2_ref, o_ref = refs[n_lhs:]
    j = pl.program_id(1)
    tm, tn = o_ref.shape
    if any_rope:
        rope_on = jnp.full((tm, HEAD_DIM), flag_ref[j], jnp.int32) != 0
        cm = jnp.where(rope_on, c_ref[...], 1.0)
        s1 = jnp.where(rope_on, s1_ref[...], 0.0)
        s2 = jnp.where(rope_on, s2_ref[...], 0.0)
    for c0 in range(0, tn, PROJ_COL_CHUNK):
        cols = slice(c0, c0 + PROJ_COL_CHUNK)
        acc = b_ref[:, cols]
        k0 = 0
        for a_ref in a_refs:
            kw = a_ref.shape[1]
            acc = acc + jnp.dot(a_ref[...], w_ref[k0:k0 + kw, cols], preferred_element_type=F32)
            k0 += kw
        if not any_rope:
            o_ref[:, cols] = acc.astype(o_ref.dtype)
            continue
        for hh in range(PROJ_COL_CHUNK // HEAD_DIM):
            sl = slice(hh * HEAD_DIM, (hh + 1) * HEAD_DIM)
            xh = acc[:, sl]
            up = pltpu.roll(xh, HEAD_DIM - ROT_HALF, axis=1)
            dn = pltpu.roll(xh, ROT_HALF, axis=1)
            o_ref[:, c0 + hh * HEAD_DIM:c0 + (hh + 1) * HEAD_DIM] = (
                xh * cm + up * s1 + dn * s2).astype(o_ref.dtype)


PROJ_COL_CHUNK = 256


def projection(lhs, w, bias, rope_flags, rope_tabs, out_dtype, any_rope, tm_pref=1024, tn_pref=512):
    m = lhs[0].shape[0]
    k, n = w.shape
    assert sum(a.shape[1] for a in lhs) == k
    tm = _tile(m, tm_pref)
    tn = _tile(n, tn_pref)
    assert rope_flags.shape == (n // tn,) and tn % PROJ_COL_CHUNK == 0
    cm, s1, s2 = rope_tabs
    tab = pl.BlockSpec((tm, HEAD_DIM), lambda i, j, f: (i, 0))
    return pl.pallas_call(
        functools.partial(_proj_kernel, n_lhs=len(lhs), any_rope=any_rope),
        grid_spec=pltpu.PrefetchScalarGridSpec(
            num_scalar_prefetch=1,
            grid=(m // tm, n // tn),
            in_specs=[pl.BlockSpec((tm, a.shape[1]), lambda i, j, f: (i, 0)) for a in lhs]
                     + [pl.BlockSpec((k, tn), lambda i, j, f: (0, j)),
                        pl.BlockSpec((1, tn), lambda i, j, f: (0, j)),
                        tab, tab, tab],
            out_specs=pl.BlockSpec((tm, tn), lambda i, j, f: (i, j)),
        ),
        out_shape=jax.ShapeDtypeStruct((m, n), out_dtype),
        compiler_params=_cparams(("arbitrary", "arbitrary")),
        name="projection",
    )(rope_flags, *lhs, w, bias.reshape(1, n).astype(F32), cm, s1, s2)


def rope_tables(positions):
    inv_freq = ROPE_THETA ** (-jnp.arange(0, ROT_DIM, 2, dtype=F32) / ROT_DIM)
    ang = positions.reshape(-1).astype(F32)[:, None] * inv_freq
    cos, sin = jnp.cos(ang), jnp.sin(ang)
    t = ang.shape[0]
    rest = HEAD_DIM - ROT_DIM
    cm = jnp.concatenate([cos, cos, jnp.ones((t, rest), F32)], axis=1)
    s1 = jnp.concatenate([-sin, jnp.zeros((t, HEAD_DIM - ROT_HALF), F32)], axis=1)
    s2 = jnp.concatenate([jnp.zeros((t, ROT_HALF), F32), sin, jnp.zeros((t, rest), F32)], axis=1)
    return cm, s1, s2


ROW_CHUNK = 256


def _lane_tile(x, width):
    return x if width == LANES else jnp.concatenate([x] * (width // LANES), axis=1)


def _flash_step(s, v, m_ref, l_ref, acc_ref):
    m_prev = m_ref[...]
    m_new = jnp.maximum(m_prev, jnp.max(s, axis=1, keepdims=True))
    alpha = jnp.exp2((m_prev - m_new) * SCALE_LOG2E)
    p = jnp.exp2((s - _lane_tile(m_new, s.shape[1])) * SCALE_LOG2E)
    if l_ref is not None:
        l_ref[...] = alpha * l_ref[...] + jnp.sum(p, axis=1, keepdims=True)
    acc_ref[...] = (_lane_tile(alpha, acc_ref.shape[-1]) * acc_ref[...]
                    + jnp.dot(p.astype(v.dtype), v, preferred_element_type=F32))
    m_ref[...] = m_new


def _flash_init(m_ref, l_ref, acc_ref):
    m_ref[...] = jnp.full(m_ref.shape, NEG, F32)
    if l_ref is not None:
        l_ref[...] = jnp.zeros(l_ref.shape, F32)
    acc_ref[...] = jnp.zeros(acc_ref.shape, F32)


def _causal_mask(tq, tk, key_offset):
    qpos = lax.broadcasted_iota(jnp.int32, (tq, tk), 0)
    kpos = lax.broadcasted_iota(jnp.int32, (tq, tk), 1) + key_offset
    return kpos <= qpos


def _diff_attn_kernel(lam_ref, q_ref, k_ref, v_ref, g_ref, o_ref, m_ref, l_ref, acc_ref,
                      *, tq, lam_init):
    qi = pl.program_id(2)
    _flash_init(m_ref, l_ref, acc_ref)

    def block(j, mask):
        start = pl.multiple_of(j * tq, tq)
        vb = v_ref[pl.ds(start, tq), :]
        for mp in range(2):
            sl = slice(mp * HEAD_DIM, (mp + 1) * HEAD_DIM)
            kb = k_ref[pl.ds(start, tq), sl]
            for rc in range(tq // ROW_CHUNK):
                rows = slice(rc * ROW_CHUNK, (rc + 1) * ROW_CHUNK)
                s = _nt_dot(q_ref[rows, sl], kb)
                if mask is not None:
                    s = jnp.where(mask[rows], s, NEG)
                _flash_step(s, vb, m_ref.at[mp, rows], l_ref.at[mp, rows], acc_ref.at[mp, rows])

    def body(j, carry):
        block(j, None)
        return carry

    lax.fori_loop(0, qi, body, 0)
    block(qi, _causal_mask(tq, tq, 0))

    lam_full = (jnp.exp(jnp.sum(lam_ref[0:1, :] * lam_ref[1:2, :], axis=1, keepdims=True))
                - jnp.exp(jnp.sum(lam_ref[2:3, :] * lam_ref[3:4, :], axis=1, keepdims=True)) + lam_init)
    width = acc_ref.shape[-1]
    o = (acc_ref[0] / _lane_tile(l_ref[0], width)
         - lam_full * (acc_ref[1] / _lane_tile(l_ref[1], width)))
    o = o * lax.rsqrt(jnp.mean(o * o, axis=1, keepdims=True) + RMS_EPS) * g_ref[...]
    o_ref[...] = (o * (1.0 - lam_init)).astype(o_ref.dtype)


def diff_attention(proj, lam_vecs, subln_g, *, batch, seq, n_heads, q_col, k_col, v_col, lam_init):
    t = proj.shape[0]
    width = 2 * HEAD_DIM
    tq = _tile(seq, 512)
    assert tq % ROW_CHUNK == 0
    nq = seq // tq
    kern = functools.partial(_diff_attn_kernel, tq=tq, lam_init=lam_init)
    return pl.pallas_call(
        kern,
        grid=(batch, n_heads, nq),
        in_specs=[pl.BlockSpec((4, HEAD_DIM), lambda b, h, qi: (0, 0)),
                  pl.BlockSpec((tq, width), lambda b, h, qi: (b * nq + qi, q_col + h)),
                  pl.BlockSpec((seq, width), lambda b, h, qi: (b, k_col + h)),
                  pl.BlockSpec((seq, width), lambda b, h, qi: (b, v_col + h)),
                  pl.BlockSpec((1, width), lambda b, h, qi: (0, 0))],
        out_specs=pl.BlockSpec((tq, width), lambda b, h, qi: (b * nq + qi, h)),
        out_shape=jax.ShapeDtypeStruct((t, n_heads * width), BF16),
        scratch_shapes=[pltpu.VMEM((2, tq, LANES), F32), pltpu.VMEM((2, tq, LANES), F32),
                        pltpu.VMEM((2, tq, width), F32)],
        compiler_params=_cparams(("arbitrary", "arbitrary", "arbitrary")),
        name="diff_attention",
    )(lam_vecs, proj, proj, proj, subln_g.reshape(1, width).astype(F32))


def _kmean_kernel(k_ref, o_ref, *, nb):
    k = k_ref[...].astype(F32)
    o_ref[...] = jnp.mean(k.reshape(nb, MOBA_BLOCK, HEAD_DIM), axis=1)


def moba_kmean(proj, *, batch, seq, n_heads, k_col):
    nb = seq // MOBA_BLOCK
    return pl.pallas_call(
        functools.partial(_kmean_kernel, nb=nb),
        grid=(batch, n_heads),
        in_specs=[pl.BlockSpec((seq, HEAD_DIM), lambda b, h: (b, k_col + h))],
        out_specs=pl.BlockSpec((None, None, nb, HEAD_DIM), lambda b, h: (b, h, 0, 0)),
        out_shape=jax.ShapeDtypeStruct((batch, n_heads, nb, HEAD_DIM), F32),
        compiler_params=_cparams(("arbitrary", "arbitrary")),
        name="moba_kmean",
    )(proj)


def _moba_block_bias(q, km, qi, *, nb, tq):
    km_hi = km.astype(BF16)
    km_lo = (km - km_hi.astype(F32)).astype(BF16)
    gate = _nt_dot(km_hi, q) + _nt_dot(km_lo, q)
    blk = lax.broadcasted_iota(jnp.int32, (nb, tq), 0).astype(F32)
    own = (qi * (tq // MOBA_BLOCK)
           + lax.broadcasted_iota(jnp.int32, (nb, tq), 1) // MOBA_BLOCK).astype(F32)
    valid = blk < own
    g = jnp.where(valid, gate, -jnp.inf)
    visible = blk == own
    for _ in range(MOBA_TOPK):
        best = jnp.max(g, axis=0, keepdims=True)
        cand = jnp.logical_and(g == best, valid)
        first = jnp.min(jnp.where(cand, blk, float(nb)), axis=0, keepdims=True)
        pick = blk == first
        visible = jnp.logical_or(visible, pick)
        valid = jnp.logical_and(valid, jnp.logical_not(pick))
        g = jnp.where(pick, -jnp.inf, g)
    bias = jnp.where(visible, 0.0, NEG)
    if nb < LANES:
        bias = jnp.concatenate([bias, jnp.zeros((LANES - nb, tq), F32)], axis=0)
    return bias.T


MOBA_HEADS_PER_STEP = 2


def _moba_kernel(q_ref, k_ref, v_ref, km_ref, o_ref, kaug_ref, vaug_ref, qaug_ref, m_ref, acc_ref,
                 *, nb, tq):
    qi = pl.program_id(2)
    seq = k_ref.shape[0]
    heads = [slice(hh * HEAD_DIM, (hh + 1) * HEAD_DIM) for hh in range(MOBA_HEADS_PER_STEP)]

    @pl.when(qi == 0)
    def _():
        key_blk = lax.broadcasted_iota(jnp.int32, (seq, LANES), 0) // MOBA_BLOCK
        lane = lax.broadcasted_iota(jnp.int32, (seq, LANES), 1)
        onehot = jnp.where(key_blk == lane, 1.0, 0.0).astype(BF16)
        for hh, sl in enumerate(heads):
            kaug_ref[hh, :, :HEAD_DIM] = k_ref[:, sl]
            kaug_ref[hh, :, HEAD_DIM:] = onehot
            vaug_ref[hh, :, :HEAD_DIM] = v_ref[:, sl]
            vaug_ref[hh, :, HEAD_DIM:] = jnp.ones((seq, LANES), BF16)

    _flash_init(m_ref, None, acc_ref)
    for hh, sl in enumerate(heads):
        q = q_ref[:, sl]
        qaug_ref[hh, :, :HEAD_DIM] = q
        qaug_ref[hh, :, HEAD_DIM:] = _moba_block_bias(q, km_ref[hh], qi, nb=nb, tq=tq).astype(BF16)

    def block(j, mask):
        start = pl.multiple_of(j * tq, tq)
        for hh in range(MOBA_HEADS_PER_STEP):
            kb = kaug_ref[hh, pl.ds(start, tq), :]
            vb = vaug_ref[hh, pl.ds(start, tq), :]
            for rc in range(tq // ROW_CHUNK):
                rows = slice(rc * ROW_CHUNK, (rc + 1) * ROW_CHUNK)
                s = _nt_dot(qaug_ref[hh, rows], kb)
                if mask is not None:
                    s = jnp.where(mask[rows], s, NEG)
                _flash_step(s, vb, m_ref.at[hh, rows], None, acc_ref.at[hh, rows])

    def body(j, carry):
        block(j, None)
        return carry

    lax.fori_loop(0, qi, body, 0)
    block(qi, _causal_mask(tq, tq, 0))
    for hh, sl in enumerate(heads):
        o_ref[:, sl] = (acc_ref[hh, :, :HEAD_DIM] / acc_ref[hh, :, HEAD_DIM:]).astype(o_ref.dtype)


def moba_attention(proj, kmean, *, batch, seq, n_heads, q_col, k_col, v_col):
    t = proj.shape[0]
    nb = seq // MOBA_BLOCK
    assert nb <= LANES
    tq = _tile(seq, 2 * MOBA_BLOCK)
    assert tq % MOBA_BLOCK == 0 and tq % ROW_CHUNK == 0
    nq = seq // tq
    hps = MOBA_HEADS_PER_STEP
    assert n_heads % hps == 0 and q_col % hps == 0 and k_col % hps == 0 and v_col % hps == 0
    width = hps * HEAD_DIM
    return pl.pallas_call(
        functools.partial(_moba_kernel, nb=nb, tq=tq),
        grid=(batch, n_heads // hps, nq),
        in_specs=[pl.BlockSpec((tq, width), lambda b, h, qi: (b * nq + qi, q_col // hps + h)),
                  pl.BlockSpec((seq, width), lambda b, h, qi: (b, k_col // hps + h)),
                  pl.BlockSpec((seq, width), lambda b, h, qi: (b, v_col // hps + h)),
                  pl.BlockSpec((None, hps, nb, HEAD_DIM), lambda b, h, qi: (b, h, 0, 0))],
        out_specs=pl.BlockSpec((tq, width), lambda b, h, qi: (b * nq + qi, h)),
        out_shape=jax.ShapeDtypeStruct((t, n_heads * HEAD_DIM), BF16),
        scratch_shapes=[pltpu.VMEM((hps, seq, 2 * HEAD_DIM), BF16),
                        pltpu.VMEM((hps, seq, HEAD_DIM + LANES), BF16),
                        pltpu.VMEM((hps, tq, 2 * HEAD_DIM), BF16),
                        pltpu.VMEM((hps, tq, LANES), F32), pltpu.VMEM((hps, tq, HEAD_DIM + LANES), F32)],
        compiler_params=_cparams(("arbitrary", "arbitrary", "arbitrary")),
        name="moba_attention",
    )(proj, proj, proj, kmean)


def _swa_kernel(sink_ref, q_ref, kc_ref, kp_ref, vc_ref, vp_ref, o_ref, *, group, nsub):
    g = pl.program_id(1)
    i = pl.program_id(2)
    w = WINDOW
    rows = group * w
    qrow = lax.broadcasted_iota(jnp.int32, (rows, 2 * w), 0) % w + w
    kcol = lax.broadcasted_iota(jnp.int32, (rows, 2 * w), 1)
    rel = qrow - kcol
    band = jnp.logical_and(rel >= 0, rel < w)
    head_of_row = lax.broadcasted_iota(jnp.int32, (rows, 1), 0) // w
    sink = jnp.zeros((rows, 1), F32)
    for jh in range(group):
        sink = jnp.where(head_of_row == jh, sink_ref[g * group + jh], sink)

    for n in range(nsub):
        qs = jnp.concatenate(
            [q_ref[n * w:(n + 1) * w, jh * HEAD_DIM:(jh + 1) * HEAD_DIM] for jh in range(group)], axis=0)
        if n == 0:
            kk = jnp.concatenate([kp_ref[...], kc_ref[0:w, :]], axis=0)
            vv = jnp.concatenate([vp_ref[...], vc_ref[0:w, :]], axis=0)
            mask = jnp.logical_and(band, kcol >= jnp.where(i > 0, 0, w))
        else:
            kk = kc_ref[(n - 1) * w:(n + 1) * w, :]
            vv = vc_ref[(n - 1) * w:(n + 1) * w, :]
            mask = band
        s = jnp.where(mask, _nt_dot(qs, kk) * ATTN_SCALE, NEG)
        m = jnp.maximum(jnp.max(s, axis=1, keepdims=True), sink)
        p = jnp.exp(s - m)
        denom = jnp.sum(p, axis=1, keepdims=True) + jnp.exp(sink - m)
        o = jnp.dot(p.astype(vv.dtype), vv, preferred_element_type=F32) / denom
        for jh in range(group):
            o_ref[n * w:(n + 1) * w, jh * HEAD_DIM:(jh + 1) * HEAD_DIM] = (
                o[jh * w:(jh + 1) * w, :].astype(o_ref.dtype))


def swa_attention(proj, sinks, *, batch, seq, n_q_heads, n_kv_heads, k_col, v_col):
    t = proj.shape[0]
    group = n_q_heads // n_kv_heads
    tq = _tile(seq, 4 * WINDOW)
    nsub = tq // WINDOW
    nq = seq // tq
    gw = group * HEAD_DIM

    def prev_map(col):
        def f(b, g, i):
            return (b * (seq // WINDOW) + jnp.maximum(i * nsub - 1, 0), col + g)
        return f

    return pl.pallas_call(
        functools.partial(_swa_kernel, group=group, nsub=nsub),
        grid_spec=pltpu.PrefetchScalarGridSpec(
            num_scalar_prefetch=1,
            grid=(batch, n_kv_heads, nq),
            in_specs=[pl.BlockSpec((tq, gw), lambda b, g, i, s: (b * nq + i, g)),
                      pl.BlockSpec((tq, HEAD_DIM), lambda b, g, i, s: (b * nq + i, k_col + g)),
                      pl.BlockSpec((WINDOW, HEAD_DIM), lambda b, g, i, s: prev_map(k_col)(b, g, i)),
                      pl.BlockSpec((tq, HEAD_DIM), lambda b, g, i, s: (b * nq + i, v_col + g)),
                      pl.BlockSpec((WINDOW, HEAD_DIM), lambda b, g, i, s: prev_map(v_col)(b, g, i))],
            out_specs=pl.BlockSpec((tq, gw), lambda b, g, i, s: (b * nq + i, g)),
        ),
        out_shape=jax.ShapeDtypeStruct((t, n_q_heads * HEAD_DIM), BF16),
        compiler_params=_cparams(("arbitrary", "arbitrary", "arbitrary")),
        name="swa_attention",
    )(sinks.astype(F32), proj, proj, proj, proj, proj)


def _layer_norm_rows(z, g, b):
    mu = jnp.mean(z, axis=1, keepdims=True)
    zc = z - mu
    var = jnp.mean(zc * zc, axis=1, keepdims=True)
    return zc * lax.rsqrt(var + LN_EPS) * g + b


def _pack_bf16_halves(h):
    half = h.shape[1] // 2
    lo = lax.bitcast_convert_type(h[:, :half].astype(BF16).astype(F32), U32)
    hi = lax.bitcast_convert_type(h[:, half:].astype(BF16).astype(F32), U32)
    return jnp.bitwise_or(jnp.right_shift(lo, jnp.uint32(16)), hi)


def _unpack_bf16_halves(w):
    lo = lax.bitcast_convert_type(jnp.left_shift(w, jnp.uint32(16)), F32).astype(BF16)
    hi = lax.bitcast_convert_type(jnp.bitwise_and(w, jnp.uint32(0xFFFF0000)), F32).astype(BF16)
    return lo, hi


def _route(scores, biased):
    e, tm = scores.shape
    epg = EXPERTS_PER_GROUP
    gs = []
    for gi in range(N_GROUPS):
        rows = [biased[gi * epg + r:gi * epg + r + 1, :] for r in range(epg)]
        best = None
        for a in range(epg):
            for b in range(a + 1, epg):
                pair = rows[a] + rows[b]
                best = pair if best is None else jnp.maximum(best, pair)
        gs.append(best)
    gsel = []
    for gi in range(N_GROUPS):
        ok = None
        for go in range(N_GROUPS):
            if go == gi:
                continue
            c = (gs[gi] > gs[go]) if go < gi else (gs[gi] >= gs[go])
            ok = c if ok is None else jnp.logical_and(ok, c)
        gsel.append(ok)
    sel_rows, comb_rows = [], []
    for gi in range(N_GROUPS):
        rows = [biased[gi * epg + r:gi * epg + r + 1, :] for r in range(epg)]
        for r in range(epg):
            rank = jnp.zeros((1, tm), jnp.int32)
            for o in range(epg):
                if o == r:
                    continue
                before = (rows[o] >= rows[r]) if o < r else (rows[o] > rows[r])
                rank = rank + jnp.where(before, 1, 0)
            s = jnp.logical_and(gsel[gi], rank < 2)
            sel_rows.append(jnp.where(s, 1.0, 0.0))
            comb_rows.append(jnp.where(s, scores[gi * epg + r:gi * epg + r + 1, :], 0.0))
    sel = jnp.concatenate(sel_rows, axis=0)
    comb = jnp.concatenate(comb_rows, axis=0)
    comb = comb / jnp.sum(comb, axis=0, keepdims=True)
    return sel, comb


def _ln_router_kernel(x_ref, y_ref, gate_ref, g_ref, b_ref, sc_ref, sh_ref, rwh_ref, rwl_ref, rb_ref,
                      x1_ref, hp_ref, sel_ref, comb_ref):
    z = DN_ALPHA * x_ref[...] + gate_ref[...] * y_ref[...]
    x1 = _layer_norm_rows(z, g_ref[...], b_ref[...])
    x1_ref[...] = x1
    h = x1 * (1.0 + sc_ref[...]) + sh_ref[...]
    hp_ref[...] = _pack_bf16_halves(h)
    h_hi = h.astype(BF16)
    h_lo = (h - h_hi.astype(F32)).astype(BF16)
    logits = _nt_dot(rwh_ref[...], h_hi) + _nt_dot(rwl_ref[...], h_hi) + _nt_dot(rwh_ref[...], h_lo)
    scores = jax.nn.sigmoid(logits)
    sel, comb = _route(scores, scores + rb_ref[...])
    sel_ref[...] = sel
    comb_ref[...] = comb


def ln_router(x2, y, mod, which_gate, which_scale, which_shift, ln_g, ln_b, router_w, router_bias, seq):
    t, d = x2.shape
    tm = _tile(seq, 256)
    e = router_w.shape[1]
    rwt = router_w.T.astype(F32)
    rw_hi = rwt.astype(BF16)
    rw_lo = (rwt - rw_hi.astype(F32)).astype(BF16)

    def vec(which):
        return pl.BlockSpec((None, None, 1, d), lambda i: ((i * tm) // seq, which, 0, 0))

    row = pl.BlockSpec((tm, d), lambda i: (i, 0))
    const = lambda shape: pl.BlockSpec(shape, lambda i: (0, 0))
    return pl.pallas_call(
        _ln_router_kernel,
        grid=(t // tm,),
        in_specs=[row, row, vec(which_gate), const((1, d)), const((1, d)), vec(which_scale), vec(which_shift),
                  const((e, d)), const((e, d)), const((e, 1))],
        out_specs=[row, pl.BlockSpec((tm, d // 2), lambda i: (i, 0)),
                   pl.BlockSpec((e, tm), lambda i: (0, i)), pl.BlockSpec((e, tm), lambda i: (0, i))],
        out_shape=[jax.ShapeDtypeStruct((t, d), F32), jax.ShapeDtypeStruct((t, d // 2), U32),
                   jax.ShapeDtypeStruct((e, t), F32), jax.ShapeDtypeStruct((e, t), F32)],
        compiler_params=_cparams(("arbitrary",)),
        name="ln_router",
    )(x2, y, mod, ln_g.reshape(1, d), ln_b.reshape(1, d), mod, mod, rw_hi, rw_lo,
      router_bias.reshape(e, 1).astype(F32))


DMA_LOOP_UNROLL = 8


def _gather_rows(idx_of_row, src_ref, dst_ref, sem, n_rows, *, wait):
    def step(r, c):
        cp = pltpu.make_async_copy(src_ref.at[pl.ds(idx_of_row(r), 1)], dst_ref.at[pl.ds(r, 1)], sem)
        if wait:
            cp.wait()
        else:
            cp.start()
        return c

    lax.fori_loop(0, n_rows, step, 0, unroll=DMA_LOOP_UNROLL)


def _moe_ffn_kernel(te_ref, nv_ref, idx_ref, nxt_ref, hp_ref, wg_ref, wu_ref, wd_ref, ys_ref,
                    xs_ref, xb_ref, hm_ref, sem):
    i = pl.program_id(0)
    f = pl.program_id(1)
    tm, half = xs_ref.shape[1], xs_ref.shape[2]
    tf = wg_ref.shape[1]
    nff = hm_ref.shape[1] // tf
    rows_per_step = tm // nff

    @pl.when(i < nv_ref[0])
    def _():
        slot = i % 2

        @pl.when(f == 0)
        def _():
            @pl.when(i == 0)
            def _():
                _gather_rows(lambda r: idx_ref[0, 0, r], hp_ref, xs_ref.at[0], sem.at[0], tm, wait=False)

            _gather_rows(lambda r: idx_ref[0, 0, r], hp_ref, xs_ref.at[slot], sem.at[slot], tm, wait=True)
            lo, hi = _unpack_bf16_halves(xs_ref[slot])
            xb_ref[:, :half] = lo
            xb_ref[:, half:] = hi

        base = f * rows_per_step
        for r in range(rows_per_step):
            pltpu.make_async_copy(hp_ref.at[pl.ds(nxt_ref[0, 0, base + r], 1)],
                                  xs_ref.at[1 - slot, pl.ds(base + r, 1)], sem.at[1 - slot]).start()

        xb = xb_ref[...]
        gt = jnp.dot(xb, wg_ref[...], preferred_element_type=F32)
        up = jnp.dot(xb, wu_ref[...], preferred_element_type=F32)
        hm = (gt * jax.nn.sigmoid(gt) * up).astype(BF16)
        for c in range(hm_ref.shape[1] // tf):
            @pl.when(f == c)
            def _(c=c):
                hm_ref[:, c * tf:(c + 1) * tf] = hm

        @pl.when(f == nff - 1)
        def _():
            ys_ref[...] = jnp.dot(hm_ref[...], wd_ref[...], preferred_element_type=F32)

        @pl.when(jnp.logical_and(f == nff - 1, i == nv_ref[0] - 1))
        def _():
            _gather_rows(lambda r: nxt_ref[0, 0, r], hp_ref, xs_ref.at[1 - slot], sem.at[1 - slot], tm, wait=True)

    @pl.when(jnp.logical_and(i >= nv_ref[0], f == 0))
    def _():
        ys_ref[...] = jnp.zeros(ys_ref.shape, ys_ref.dtype)


def moe_ffn(hp, row_token, w_gate, w_up, w_down, tile_expert, n_valid, tm):
    r = row_token.shape[0]
    half = hp.shape[1]
    e, d, ff = w_gate.shape
    tf = _tile(ff, 256)
    nff = ff // tf
    n_tiles = r // tm

    def f_eff(i, f, nv):
        return jnp.where(i < nv[0], f, nff - 1)

    idx_spec = lambda shift: pl.BlockSpec(
        (1, 1, tm), lambda i, f, te, nv: (jnp.minimum(i + shift, n_tiles - 1), 0, 0), memory_space=pltpu.SMEM)
    return pl.pallas_call(
        _moe_ffn_kernel,
        grid_spec=pltpu.PrefetchScalarGridSpec(
            num_scalar_prefetch=2,
            grid=(n_tiles, nff),
            in_specs=[idx_spec(0), idx_spec(1), pl.BlockSpec(memory_space=pl.ANY),
                      pl.BlockSpec((None, d, tf), lambda i, f, te, nv: (te[i], 0, f_eff(i, f, nv))),
                      pl.BlockSpec((None, d, tf), lambda i, f, te, nv: (te[i], 0, f_eff(i, f, nv))),
                      pl.BlockSpec((None, ff, d), lambda i, f, te, nv: (te[i], 0, 0),
                                   pipeline_mode=pl.Buffered(1))],
            out_specs=pl.BlockSpec((tm, d), lambda i, f, te, nv: (i, 0)),
            scratch_shapes=[pltpu.VMEM((2, tm, half), U32), pltpu.VMEM((tm, d), BF16),
                            pltpu.VMEM((tm, ff), BF16), pltpu.SemaphoreType.DMA((2,))],
        ),
        out_shape=jax.ShapeDtypeStruct((r, d), F32),
        compiler_params=_cparams(("arbitrary", "arbitrary")),
        name="moe_ffn",
    )(tile_expert, n_valid, row_token.reshape(n_tiles, 1, tm), row_token.reshape(n_tiles, 1, tm), hp,
      w_gate, w_up, w_down)


def moe_dispatch(sel_t, comb_t, tm):
    e, t = sel_t.shape
    k = 2
    sel = sel_t.T > 0.5
    seli = sel.astype(jnp.int32)
    counts = jnp.sum(seli, axis=0)
    padded = ((counts + tm - 1) // tm) * tm
    ends = jnp.cumsum(padded)
    starts = ends - padded
    pos = jnp.cumsum(seli, axis=0) - seli
    dest_all = starts[None, :] + pos
    nth = jnp.cumsum(seli, axis=1) * seli
    pick = [(nth == n + 1) for n in range(k)]
    dest = jnp.stack([jnp.sum(jnp.where(p, dest_all, 0), axis=1) for p in pick], axis=1)
    wts = jnp.stack([jnp.sum(jnp.where(p, comb_t.T, 0.0), axis=1) for p in pick], axis=1)
    r_max = k * t + e * tm
    flat = dest.reshape(-1)
    row_token = jnp.zeros((r_max,), jnp.int32).at[flat].set(jnp.repeat(jnp.arange(t, dtype=jnp.int32), k))
    n_tiles = r_max // tm
    tile_start = jnp.arange(n_tiles, dtype=jnp.int32) * tm
    n_valid = (ends[-1] // tm).astype(jnp.int32)
    tile_expert = jnp.sum((tile_start[:, None] >= ends[None, :]).astype(jnp.int32), axis=1)
    tile_expert = jnp.minimum(tile_expert, e - 1)
    last_valid_expert = tile_expert[jnp.maximum(n_valid - 1, 0)]
    tile_expert = jnp.where(jnp.arange(n_tiles) < n_valid, tile_expert, last_valid_expert)
    return row_token, tile_expert, n_valid.reshape(1), dest, wts


def _ln_combine_kernel(dest_ref, nxt_ref, x_ref, w_ref, ys_ref, gate_ref, g_ref, b_ref, sc_ref, sh_ref,
                       xo_ref, h_ref, buf_ref, sem, *, tm, emit_h):
    i = pl.program_id(0)
    slot = i % 2

    def gather(idx_ref, to_slot, wait):
        for k in range(2):
            _gather_rows(lambda r, k=k: idx_ref[0, k, r], ys_ref, buf_ref.at[to_slot, k], sem.at[to_slot], tm,
                         wait=wait)

    @pl.when(i == 0)
    def _():
        gather(dest_ref, 0, False)

    @pl.when(i + 1 < pl.num_programs(0))
    def _():
        gather(nxt_ref, 1 - slot, False)

    gather(dest_ref, slot, True)
    w = w_ref[...]
    y = w[:, 0:1] * buf_ref[slot, 0] + w[:, 1:2] * buf_ref[slot, 1]
    z = DN_ALPHA * x_ref[...] + gate_ref[...] * y
    xo = _layer_norm_rows(z, g_ref[...], b_ref[...])
    xo_ref[...] = xo
    if emit_h:
        h_ref[...] = (xo * (1.0 + sc_ref[...]) + sh_ref[...]).astype(h_ref.dtype)
    else:
        h_ref[...] = jnp.zeros(h_ref.shape, h_ref.dtype)


def ln_combine(x1, ys, dest, wts, mod, which_gate, ln_g, ln_b, next_mod, seq, emit_h):
    t, d = x1.shape
    tm = _tile(seq, 256)

    def vec(which):
        return pl.BlockSpec((None, None, 1, d), lambda i: ((i * tm) // seq, which, 0, 0))

    row = pl.BlockSpec((tm, d), lambda i: (i, 0))
    const = lambda shape: pl.BlockSpec(shape, lambda i: (0, 0))
    h_rows = tm if emit_h else 8
    h_spec = pl.BlockSpec((h_rows, d), (lambda i: (i, 0)) if emit_h else (lambda i: (0, 0)))
    n_steps = t // tm
    dest_t = dest.reshape(n_steps, tm, 2).transpose(0, 2, 1)
    xo, h = pl.pallas_call(
        functools.partial(_ln_combine_kernel, tm=tm, emit_h=emit_h),
        grid=(n_steps,),
        in_specs=[pl.BlockSpec((1, 2, tm), lambda i: (i, 0, 0), memory_space=pltpu.SMEM),
                  pl.BlockSpec((1, 2, tm), lambda i: (jnp.minimum(i + 1, n_steps - 1), 0, 0),
                               memory_space=pltpu.SMEM),
                  row, pl.BlockSpec((tm, 2), lambda i: (i, 0)), pl.BlockSpec(memory_space=pl.ANY),
                  vec(which_gate), const((1, d)), const((1, d)),
                  pl.BlockSpec((None, None, 1, d), lambda i: ((i * tm) // seq, 1, 0, 0)),
                  pl.BlockSpec((None, None, 1, d), lambda i: ((i * tm) // seq, 0, 0, 0))],
        out_specs=[row, h_spec],
        out_shape=[jax.ShapeDtypeStruct((t, d), F32),
                   jax.ShapeDtypeStruct((t if emit_h else 8, d), BF16)],
        scratch_shapes=[pltpu.VMEM((2, 2, tm, d), F32), pltpu.SemaphoreType.DMA((2,))],
        compiler_params=_cparams(("arbitrary",)),
        name="ln_combine",
    )(dest_t, dest_t, x1, wts, ys, mod, ln_g.reshape(1, d), ln_b.reshape(1, d), next_mod, next_mod)
    return xo, h


MOE_TILE = 512


def _moe_block(x1, hp, sel_t, comb_t, mod, ln_g, ln_b, w_gate, w_up, w_down, next_mod, seq, emit_h):
    tm = min(MOE_TILE, x1.shape[0])
    row_token, tile_expert, n_valid, dest, wts = moe_dispatch(sel_t, comb_t, tm)
    ys = moe_ffn(hp, row_token, w_gate.astype(BF16), w_up.astype(BF16), w_down.astype(BF16),
                 tile_expert, n_valid, tm)
    return ln_combine(x1, ys, dest, wts, mod, 5, ln_g, ln_b, next_mod, seq, emit_h)


def _flags(n, tn, rope_ranges):
    out = []
    for j in range(n // tn):
        c = j * tn
        out.append(1 if any(lo <= c < hi for lo, hi in rope_ranges) else 0)
    return jnp.asarray(out, jnp.int32)


def kernel(x, c, positions, router_w, router_bias, l0_w_ada, l0_b_ada, l0_w_in, l0_lambda_q1, l0_lambda_k1, l0_lambda_q2, l0_lambda_k2, l0_subln_g, l0_w_out, l0_ln1_g, l0_ln1_b, l0_w_gate, l0_w_up, l0_w_down, l0_ln2_g, l0_ln2_b, l1_w_ada, l1_b_ada, l1_w_in, l1_b_in, l1_sinks, l1_w_out, l1_ln1_g, l1_ln1_b, l1_w_gate, l1_w_up, l1_w_down, l1_ln2_g, l1_ln2_b):
    batch, seq, d = x.shape
    t = batch * seq
    a_heads = d // (4 * HEAD_DIM)
    a_width = a_heads * 2 * HEAD_DIM
    b_heads = d // (2 * HEAD_DIM)
    b_width = b_heads * HEAD_DIM
    c_heads = d // HEAD_DIM
    c_kv = c_heads // 4
    assert l0_w_in.shape[1] == 3 * a_width + 3 * b_width
    assert seq % MOBA_BLOCK == 0 and seq % WINDOW == 0

    x2 = x.reshape(t, d)
    tabs = rope_tables(positions)
    mod0 = ada_modulation(c, l0_w_ada, l0_b_ada)
    mod1 = ada_modulation(c, l1_w_ada, l1_b_ada)
    tn = 1024 if d >= 4096 else 256

    h = modulate(x2, mod0, 0, 1, seq)
    n0 = l0_w_in.shape[1]
    rope0 = [(0, 2 * a_width), (3 * a_width, 3 * a_width + 2 * b_width)]
    proj = projection([h], l0_w_in.astype(BF16), jnp.zeros((n0,), F32), _flags(n0, tn, rope0), tabs, BF16,
                      True, tn_pref=tn)
    lam_vecs = jnp.stack([l0_lambda_q1, l0_lambda_k1, l0_lambda_q2, l0_lambda_k2]).astype(F32)
    lam_init = 0.8 - 0.6 * math.exp(-0.3 * 0)
    wa = 2 * HEAD_DIM
    o_a = diff_attention(proj, lam_vecs, l0_subln_g, batch=batch, seq=seq, n_heads=a_heads,
                         q_col=0, k_col=a_width // wa, v_col=2 * a_width // wa, lam_init=lam_init)
    bq0 = 3 * a_width // HEAD_DIM
    nbh = b_width // HEAD_DIM
    kmean = moba_kmean(proj, batch=batch, seq=seq, n_heads=b_heads, k_col=bq0 + nbh)
    o_b = moba_attention(proj, kmean, batch=batch, seq=seq, n_heads=b_heads,
                         q_col=bq0, k_col=bq0 + nbh, v_col=bq0 + 2 * nbh)
    y = projection([o_a, o_b], l0_w_out.astype(BF16), jnp.zeros((d,), F32), _flags(d, tn, []), tabs, F32,
                   False, tn_pref=tn)
    x1, hp, sel_t, comb_t = ln_router(x2, y, mod0, 2, 4, 3, l0_ln1_g, l0_ln1_b, router_w, router_bias, seq)
    x2, h = _moe_block(x1, hp, sel_t, comb_t, mod0, l0_ln2_g, l0_ln2_b, l0_w_gate, l0_w_up, l0_w_down,
                       mod1, seq, True)

    n1 = l1_w_in.shape[1]
    cq = c_heads * HEAD_DIM
    ckv = c_kv * HEAD_DIM
    rope1 = [(0, cq + ckv)]
    proj = projection([h], l1_w_in.astype(BF16), l1_b_in, _flags(n1, tn, rope1), tabs, BF16, True, tn_pref=tn)
    o = swa_attention(proj, l1_sinks, batch=batch, seq=seq, n_q_heads=c_heads, n_kv_heads=c_kv,
                      k_col=cq // HEAD_DIM, v_col=(cq + ckv) // HEAD_DIM)
    y = projection([o], l1_w_out.astype(BF16), jnp.zeros((d,), F32), _flags(d, tn, []), tabs, F32, False,
                   tn_pref=tn)
    x1, hp, sel_t, comb_t = ln_router(x2, y, mod1, 2, 4, 3, l1_ln1_g, l1_ln1_b, router_w, router_bias, seq)
    x2, _ = _moe_block(x1, hp, sel_t, comb_t, mod1, l1_ln2_g, l1_ln2_b, l1_w_gate, l1_w_up, l1_w_down,
                       mod1, seq, False)
    return x2.reshape(batch, seq, d)
```

```python
import functools
import math

import jax
import jax.numpy as jnp
from jax import lax
from jax.experimental import pallas as pl
from jax.experimental.pallas import tpu as pltpu

F32 = jnp.float32
BF16 = jnp.bfloat16
U32 = jnp.uint32

HEAD_DIM = 128
ROT_DIM = HEAD_DIM // 4
ROT_HALF = ROT_DIM // 2
ROPE_THETA = 500000.0
MOBA_BLOCK = 256
MOBA_TOPK = 3
WINDOW = 128
N_EXPERTS = 16
N_GROUPS = 4
EXPERTS_PER_GROUP = N_EXPERTS // N_GROUPS
DEPTH = 2
DN_ALPHA = float((2 * DEPTH) ** 0.25)
LN_EPS = 1e-5
RMS_EPS = 1e-5
ATTN_SCALE = HEAD_DIM ** -0.5
SCALE_LOG2E = ATTN_SCALE * math.log2(math.e)
NEG = -1e30

LANES = 128
V7X_VMEM_BYTES = 64 * 1024 * 1024
VMEM_LIMIT = 56 * 1024 * 1024


def _cparams(semantics):
    return pltpu.CompilerParams(dimension_semantics=semantics, vmem_limit_bytes=VMEM_LIMIT)


def _tile(n, pref):
    t = min(n, pref)
    assert n % t == 0, (n, t)
    return t


def _nt_dot(a, b):
    return lax.dot_general(a, b, (((1,), (1,)), ((), ())), preferred_element_type=F32)


def _ada_kernel(c_ref, w_ref, b_ref, o_ref):
    c = c_ref[...]
    a = (c * jax.nn.sigmoid(c)).astype(BF16)
    o_ref[...] = jnp.dot(a, w_ref[...].astype(BF16), preferred_element_type=F32) + b_ref[...]


def ada_modulation(c, w_ada, b_ada):
    bn, d = c.shape
    n = w_ada.shape[1]
    rows = 8
    cp = jnp.zeros((rows, d), F32).at[:bn].set(c)
    tn = _tile(n, 512)
    out = pl.pallas_call(
        _ada_kernel,
        grid=(n // tn,),
        in_specs=[pl.BlockSpec((rows, d), lambda j: (0, 0)),
                  pl.BlockSpec((d, tn), lambda j: (0, j)),
                  pl.BlockSpec((1, tn), lambda j: (0, j))],
        out_specs=pl.BlockSpec((rows, tn), lambda j: (0, j)),
        out_shape=jax.ShapeDtypeStruct((rows, n), F32),
        compiler_params=_cparams(("arbitrary",)),
        name="ada_modulation",
    )(cp, w_ada, b_ada.reshape(1, n))
    return out[:bn].reshape(bn, 6, 1, d)


def _modulate_kernel(x_ref, sc_ref, sh_ref, o_ref):
    o_ref[...] = (x_ref[...] * (1.0 + sc_ref[...]) + sh_ref[...]).astype(o_ref.dtype)


def modulate(x2, mod, which_shift, which_scale, seq):
    t, d = x2.shape
    tm = _tile(seq, 512)

    def vec(which):
        return pl.BlockSpec((None, None, 1, d), lambda i: ((i * tm) // seq, which, 0, 0))

    return pl.pallas_call(
        _modulate_kernel,
        grid=(t // tm,),
        in_specs=[pl.BlockSpec((tm, d), lambda i: (i, 0)), vec(which_scale), vec(which_shift)],
        out_specs=pl.BlockSpec((tm, d), lambda i: (i, 0)),
        out_shape=jax.ShapeDtypeStruct((t, d), BF16),
        compiler_params=_cparams(("arbitrary",)),
        name="modulate",
    )(x2, mod, mod)


def _proj_kernel(flag_ref, *refs, n_lhs, any_rope):
    a_refs = refs[:n_lhs]
    w_ref, b_ref, c_ref, s1_ref, s2_ref, o_ref = refs[n_lhs:]
    j = pl.program_id(1)
    tm, tn = o_ref.shape
    if any_rope:
        rope_on = jnp.full((tm, HEAD_DIM), flag_ref[j], jnp.int32) != 0
        cm = jnp.where(rope_on, c_ref[...], 1.0)
        s1 = jnp.where(rope_on, s1_ref[...], 0.0)
        s2 = jnp.where(rope_on, s2_ref[...], 0.0)
    for c0 in range(0, tn, PROJ_COL_CHUNK):
        cols = slice(c0, c0 + PROJ_COL_CHUNK)
        acc = b_ref[:, cols]
        k0 = 0
        for a_ref in a_refs:
            kw = a_ref.shape[1]
            acc = acc + jnp.dot(a_ref[...], w_ref[k0:k0 + kw, cols], preferred_element_type=F32)
            k0 += kw
        if not any_rope:
            o_ref[:, cols] = acc.astype(o_ref.dtype)
            continue
        for hh in range(PROJ_COL_CHUNK // HEAD_DIM):
            sl = slice(hh * HEAD_DIM, (hh + 1) * HEAD_DIM)
            xh = acc[:, sl]
            up = pltpu.roll(xh, HEAD_DIM - ROT_HALF, axis=1)
            dn = pltpu.roll(xh, ROT_HALF, axis=1)
            o_ref[:, c0 + hh * HEAD_DIM:c0 + (hh + 1) * HEAD_DIM] = (
                xh * cm + up * s1 + dn * s2).astype(o_ref.dtype)


PROJ_COL_CHUNK = 256


def projection(lhs, w, bias, rope_flags, rope_tabs, out_dtype, any_rope, tm_pref=1024, tn_pref=512):
    m = lhs[0].shape[0]
    k, n = w.shape
    assert sum(a.shape[1] for a in lhs) == k
    tm = _tile(m, tm_pref)
    tn = _tile(n, tn_pref)
    assert rope_flags.shape == (n // tn,) and tn % PROJ_COL_CHUNK == 0
    cm, s1, s2 = rope_tabs
    tab = pl.BlockSpec((tm, HEAD_DIM), lambda i, j, f: (i, 0))
    return pl.pallas_call(
        functools.partial(_proj_kernel, n_lhs=len(lhs), any_rope=any_rope),
        grid_spec=pltpu.PrefetchScalarGridSpec(
            num_scalar_prefetch=1,
            grid=(m // tm, n // tn),
            in_specs=[pl.BlockSpec((tm, a.shape[1]), lambda i, j, f: (i, 0)) for a in lhs]
                     + [pl.BlockSpec((k, tn), lambda i, j, f: (0, j)),
                        pl.BlockSpec((1, tn), lambda i, j, f: (0, j)),
                        tab, tab, tab],
            out_specs=pl.BlockSpec((tm, tn), lambda i, j, f: (i, j)),
        ),
        out_shape=jax.ShapeDtypeStruct((m, n), out_dtype),
        compiler_params=_cparams(("arbitrary", "arbitrary")),
        name="projection",
    )(rope_flags, *lhs, w, bias.reshape(1, n).astype(F32), cm, s1, s2)


def rope_tables(positions):
    inv_freq = ROPE_THETA ** (-jnp.arange(0, ROT_DIM, 2, dtype=F32) / ROT_DIM)
    ang = positions.reshape(-1).astype(F32)[:, None] * inv_freq
    cos, sin = jnp.cos(ang), jnp.sin(ang)
    t = ang.shape[0]
    rest = HEAD_DIM - ROT_DIM
    cm = jnp.concatenate([cos, cos, jnp.ones((t, rest), F32)], axis=1)
    s1 = jnp.concatenate([-sin, jnp.zeros((t, HEAD_DIM - ROT_HALF), F32)], axis=1)
    s2 = jnp.concatenate([jnp.zeros((t, ROT_HALF), F32), sin, jnp.zeros((t, rest), F32)], axis=1)
    return cm, s1, s2


ROW_CHUNK = 256


def _lane_tile(x, width):
    return x if width == LANES else jnp.concatenate([x] * (width // LANES), axis=1)


def _flash_step(s, v, m_ref, l_ref, acc_ref):
    m_prev = m_ref[...]
    m_new = jnp.maximum(m_prev, jnp.max(s, axis=1, keepdims=True))
    alpha = jnp.exp2((m_prev - m_new) * SCALE_LOG2E)
    p = jnp.exp2((s - _lane_tile(m_new, s.shape[1])) * SCALE_LOG2E)
    if l_ref is not None:
        l_ref[...] = alpha * l_ref[...] + jnp.sum(p, axis=1, keepdims=True)
    acc_ref[...] = (_lane_tile(alpha, acc_ref.shape[-1]) * acc_ref[...]
                    + jnp.dot(p.astype(v.dtype), v, preferred_element_type=F32))
    m_ref[...] = m_new


def _flash_init(m_ref, l_ref, acc_ref):
    m_ref[...] = jnp.full(m_ref.shape, NEG, F32)
    if l_ref is not None:
        l_ref[...] = jnp.zeros(l_ref.shape, F32)
    acc_ref[...] = jnp.zeros(acc_ref.shape, F32)


def _causal_mask(tq, tk, key_offset):
    qpos = lax.broadcasted_iota(jnp.int32, (tq, tk), 0)
    kpos = lax.broadcasted_iota(jnp.int32, (tq, tk), 1) + key_offset
    return kpos <= qpos


def _diff_attn_kernel(lam_ref, q_ref, k_ref, v_ref, g_ref, o_ref, m_ref, l_ref, acc_ref,
                      *, tq, lam_init):
    qi = pl.program_id(2)
    _flash_init(m_ref, l_ref, acc_ref)

    def block(j, mask):
        start = pl.multiple_of(j * tq, tq)
        vb = v_ref[pl.ds(start, tq), :]
        for mp in range(2):
            sl = slice(mp * HEAD_DIM, (mp + 1) * HEAD_DIM)
            kb = k_ref[pl.ds(start, tq), sl]
            for rc in range(tq // ROW_CHUNK):
                rows = slice(rc * ROW_CHUNK, (rc + 1) * ROW_CHUNK)
                s = _nt_dot(q_ref[rows, sl], kb)
                if mask is not None:
                    s = jnp.where(mask[rows], s, NEG)
                _flash_step(s, vb, m_ref.at[mp, rows], l_ref.at[mp, rows], acc_ref.at[mp, rows])

    def body(j, carry):
        block(j, None)
        return carry

    lax.fori_loop(0, qi, body, 0)
    block(qi, _causal_mask(tq, tq, 0))

    lam_full = (jnp.exp(jnp.sum(lam_ref[0:1, :] * lam_ref[1:2, :], axis=1, keepdims=True))
                - jnp.exp(jnp.sum(lam_ref[2:3, :] * lam_ref[3:4, :], axis=1, keepdims=True)) + lam_init)
    width = acc_ref.shape[-1]
    o = (acc_ref[0] / _lane_tile(l_ref[0], width)
         - lam_full * (acc_ref[1] / _lane_tile(l_ref[1], width)))
    o = o * lax.rsqrt(jnp.mean(o * o, axis=1, keepdims=True) + RMS_EPS) * g_ref[...]
    o_ref[...] = (o * (1.0 - lam_init)).astype(o_ref.dtype)


def diff_attention(proj, lam_vecs, subln_g, *, batch, seq, n_heads, q_col, k_col, v_col, lam_init):
    t = proj.shape[0]
    width = 2 * HEAD_DIM
    tq = _tile(seq, 512)
    assert tq % ROW_CHUNK == 0
    nq = seq // tq
    kern = functools.partial(_diff_attn_kernel, tq=tq, lam_init=lam_init)
    return pl.pallas_call(
        kern,
        grid=(batch, n_heads, nq),
        in_specs=[pl.BlockSpec((4, HEAD_DIM), lambda b, h, qi: (0, 0)),
                  pl.BlockSpec((tq, width), lambda b, h, qi: (b * nq + qi, q_col + h)),
                  pl.BlockSpec((seq, width), lambda b, h, qi: (b, k_col + h)),
                  pl.BlockSpec((seq, width), lambda b, h, qi: (b, v_col + h)),
                  pl.BlockSpec((1, width), lambda b, h, qi: (0, 0))],
        out_specs=pl.BlockSpec((tq, width), lambda b, h, qi: (b * nq + qi, h)),
        out_shape=jax.ShapeDtypeStruct((t, n_heads * width), BF16),
        scratch_shapes=[pltpu.VMEM((2, tq, LANES), F32), pltpu.VMEM((2, tq, LANES), F32),
                        pltpu.VMEM((2, tq, width), F32)],
        compiler_params=_cparams(("arbitrary", "arbitrary", "arbitrary")),
        name="diff_attention",
    )(lam_vecs, proj, proj, proj, subln_g.reshape(1, width).astype(F32))


def _kmean_kernel(k_ref, o_ref, *, nb):
    k = k_ref[...].astype(F32)
    o_ref[...] = jnp.mean(k.reshape(nb, MOBA_BLOCK, HEAD_DIM), axis=1)


def moba_kmean(proj, *, batch, seq, n_heads, k_col):
    nb = seq // MOBA_BLOCK
    return pl.pallas_call(
        functools.partial(_kmean_kernel, nb=nb),
        grid=(batch, n_heads),
        in_specs=[pl.BlockSpec((seq, HEAD_DIM), lambda b, h: (b, k_col + h))],
        out_specs=pl.BlockSpec((None, None, nb, HEAD_DIM), lambda b, h: (b, h, 0, 0)),
        out_shape=jax.ShapeDtypeStruct((batch, n_heads, nb, HEAD_DIM), F32),
        compiler_params=_cparams(("arbitrary", "arbitrary")),
        name="moba_kmean",
    )(proj)


def _moba_block_bias(q, km, qi, *, nb, tq):
    km_hi = km.astype(BF16)
    km_lo = (km - km_hi.astype(F32)).astype(BF16)
    gate = _nt_dot(km_hi, q) + _nt_dot(km_lo, q)
    blk = lax.broadcasted_iota(jnp.int32, (nb, tq), 0).astype(F32)
    own = (qi * (tq // MOBA_BLOCK)
           + lax.broadcasted_iota(jnp.int32, (nb, tq), 1) // MOBA_BLOCK).astype(F32)
    valid = blk < own
    g = jnp.where(valid, gate, -jnp.inf)
    visible = blk == own
    for _ in range(MOBA_TOPK):
        best = jnp.max(g, axis=0, keepdims=True)
        cand = jnp.logical_and(g == best, valid)
        first = jnp.min(jnp.where(cand, blk, float(nb)), axis=0, keepdims=True)
        pick = blk == first
        visible = jnp.logical_or(visible, pick)
        valid = jnp.logical_and(valid, jnp.logical_not(pick))
        g = jnp.where(pick, -jnp.inf, g)
    bias = jnp.where(visible, 0.0, NEG)
    if nb < LANES:
        bias = jnp.concatenate([bias, jnp.zeros((LANES - nb, tq), F32)], axis=0)
    return bias.T


MOBA_HEADS_PER_STEP = 2


def _moba_kernel(q_ref, k_ref, v_ref, km_ref, o_ref, kaug_ref, vaug_ref, qaug_ref, m_ref, acc_ref,
                 *, nb, tq):
    qi = pl.program_id(2)
    seq = k_ref.shape[0]
    heads = [slice(hh * HEAD_DIM, (hh + 1) * HEAD_DIM) for hh in range(MOBA_HEADS_PER_STEP)]

    @pl.when(qi == 0)
    def _():
        key_blk = lax.broadcasted_iota(jnp.int32, (seq, LANES), 0) // MOBA_BLOCK
        lane = lax.broadcasted_iota(jnp.int32, (seq, LANES), 1)
        onehot = jnp.where(key_blk == lane, 1.0, 0.0).astype(BF16)
        for hh, sl in enumerate(heads):
            kaug_ref[hh, :, :HEAD_DIM] = k_ref[:, sl]
            kaug_ref[hh, :, HEAD_DIM:] = onehot
            vaug_ref[hh, :, :HEAD_DIM] = v_ref[:, sl]
            vaug_ref[hh, :, HEAD_DIM:] = jnp.ones((seq, LANES), BF16)

    _flash_init(m_ref, None, acc_ref)
    for hh, sl in enumerate(heads):
        q = q_ref[:, sl]
        qaug_ref[hh, :, :HEAD_DIM] = q
        qaug_ref[hh, :, HEAD_DIM:] = _moba_block_bias(q, km_ref[hh], qi, nb=nb, tq=tq).astype(BF16)

    def block(j, mask):
        start = pl.multiple_of(j * tq, tq)
        for hh in range(MOBA_HEADS_PER_STEP):
            kb = kaug_ref[hh, pl.ds(start, tq), :]
            vb = vaug_ref[hh, pl.ds(start, tq), :]
            for rc in range(tq // ROW_CHUNK):
                rows = slice(rc * ROW_CHUNK, (rc + 1) * ROW_CHUNK)
                s = _nt_dot(qaug_ref[hh, rows], kb)
                if mask is not None:
                    s = jnp.where(mask[rows], s, NEG)
                _flash_step(s, vb, m_ref.at[hh, rows], None, acc_ref.at[hh, rows])

    def body(j, carry):
        block(j, None)
        return carry

    lax.fori_loop(0, qi, body, 0)
    block(qi, _causal_mask(tq, tq, 0))
    for hh, sl in enumerate(heads):
        o_ref[:, sl] = (acc_ref[hh, :, :HEAD_DIM] / acc_ref[hh, :, HEAD_DIM:]).astype(o_ref.dtype)


def moba_attention(proj, kmean, *, batch, seq, n_heads, q_col, k_col, v_col):
    t = proj.shape[0]
    nb = seq // MOBA_BLOCK
    assert nb <= LANES
    tq = _tile(seq, 2 * MOBA_BLOCK)
    assert tq % MOBA_BLOCK == 0 and tq % ROW_CHUNK == 0
    nq = seq // tq
    hps = MOBA_HEADS_PER_STEP
    assert n_heads % hps == 0 and q_col % hps == 0 and k_col % hps == 0 and v_col % hps == 0
    width = hps * HEAD_DIM
    return pl.pallas_call(
        functools.partial(_moba_kernel, nb=nb, tq=tq),
        grid=(batch, n_heads // hps, nq),
        in_specs=[pl.BlockSpec((tq, width), lambda b, h, qi: (b * nq + qi, q_col // hps + h)),
                  pl.BlockSpec((seq, width), lambda b, h, qi: (b, k_col // hps + h)),
                  pl.BlockSpec((seq, width), lambda b, h, qi: (b, v_col // hps + h)),
                  pl.BlockSpec((None, hps, nb, HEAD_DIM), lambda b, h, qi: (b, h, 0, 0))],
        out_specs=pl.BlockSpec((tq, width), lambda b, h, qi: (b * nq + qi, h)),
        out_shape=jax.ShapeDtypeStruct((t, n_heads * HEAD_DIM), BF16),
        scratch_shapes=[pltpu.VMEM((hps, seq, 2 * HEAD_DIM), BF16),
                        pltpu.VMEM((hps, seq, HEAD_DIM + LANES), BF16),
                        pltpu.VMEM((hps, tq, 2 * HEAD_DIM), BF16),
                        pltpu.VMEM((hps, tq, LANES), F32), pltpu.VMEM((hps, tq, HEAD_DIM + LANES), F32)],
        compiler_params=_cparams(("arbitrary", "arbitrary", "arbitrary")),
        name="moba_attention",
    )(proj, proj, proj, kmean)


def _swa_kernel(sink_ref, q_ref, kc_ref, kp_ref, vc_ref, vp_ref, o_ref, *, group, nsub):
    g = pl.program_id(1)
    i = pl.program_id(2)
    w = WINDOW
    rows = group * w
    qrow = lax.broadcasted_iota(jnp.int32, (rows, 2 * w), 0) % w + w
    kcol = lax.broadcasted_iota(jnp.int32, (rows, 2 * w), 1)
    rel = qrow - kcol
    band = jnp.logical_and(rel >= 0, rel < w)
    head_of_row = lax.broadcasted_iota(jnp.int32, (rows, 1), 0) // w
    sink = jnp.zeros((rows, 1), F32)
    for jh in range(group):
        sink = jnp.where(head_of_row == jh, sink_ref[g * group + jh], sink)

    for n in range(nsub):
        qs = jnp.concatenate(
            [q_ref[n * w:(n + 1) * w, jh * HEAD_DIM:(jh + 1) * HEAD_DIM] for jh in range(group)], axis=0)
        if n == 0:
            kk = jnp.concatenate([kp_ref[...], kc_ref[0:w, :]], axis=0)
            vv = jnp.concatenate([vp_ref[...], vc_ref[0:w, :]], axis=0)
            mask = jnp.logical_and(band, kcol >= jnp.where(i > 0, 0, w))
        else:
            kk = kc_ref[(n - 1) * w:(n + 1) * w, :]
            vv = vc_ref[(n - 1) * w:(n + 1) * w, :]
            mask = band
        s = jnp.where(mask, _nt_dot(qs, kk) * ATTN_SCALE, NEG)
        m = jnp.maximum(jnp.max(s, axis=1, keepdims=True), sink)
        p = jnp.exp(s - m)
        denom = jnp.sum(p, axis=1, keepdims=True) + jnp.exp(sink - m)
        o = jnp.dot(p.astype(vv.dtype), vv, preferred_element_type=F32) / denom
        for jh in range(group):
            o_ref[n * w:(n + 1) * w, jh * HEAD_DIM:(jh + 1) * HEAD_DIM] = (
                o[jh * w:(jh + 1) * w, :].astype(o_ref.dtype))


def swa_attention(proj, sinks, *, batch, seq, n_q_heads, n_kv_heads, k_col, v_col):
    t = proj.shape[0]
    group = n_q_heads // n_kv_heads
    tq = _tile(seq, 4 * WINDOW)
    nsub = tq // WINDOW
    nq = seq // tq
    gw = group * HEAD_DIM

    def prev_map(col):
        def f(b, g, i):
            return (b * (seq // WINDOW) + jnp.maximum(i * nsub - 1, 0), col + g)
        return f

    return pl.pallas_call(
        functools.partial(_swa_kernel, group=group, nsub=nsub),
        grid_spec=pltpu.PrefetchScalarGridSpec(
            num_scalar_prefetch=1,
            grid=(batch, n_kv_heads, nq),
            in_specs=[pl.BlockSpec((tq, gw), lambda b, g, i, s: (b * nq + i, g)),
                      pl.BlockSpec((tq, HEAD_DIM), lambda b, g, i, s: (b * nq + i, k_col + g)),
                      pl.BlockSpec((WINDOW, HEAD_DIM), lambda b, g, i, s: prev_map(k_col)(b, g, i)),
                      pl.BlockSpec((tq, HEAD_DIM), lambda b, g, i, s: (b * nq + i, v_col + g)),
                      pl.BlockSpec((WINDOW, HEAD_DIM), lambda b, g, i, s: prev_map(v_col)(b, g, i))],
            out_specs=pl.BlockSpec((tq, gw), lambda b, g, i, s: (b * nq + i, g)),
        ),
        out_shape=jax.ShapeDtypeStruct((t, n_q_heads * HEAD_DIM), BF16),
        compiler_params=_cparams(("arbitrary", "arbitrary", "arbitrary")),
        name="swa_attention",
    )(sinks.astype(F32), proj, proj, proj, proj, proj)


def _layer_norm_rows(z, g, b):
    mu = jnp.mean(z, axis=1, keepdims=True)
    zc = z - mu
    var = jnp.mean(zc * zc, axis=1, keepdims=True)
    return zc * lax.rsqrt(var + LN_EPS) * g + b


def _pack_bf16_halves(h):
    half = h.shape[1] // 2
    lo = lax.bitcast_convert_type(h[:, :half].astype(BF16).astype(F32), U32)
    hi = lax.bitcast_convert_type(h[:, half:].astype(BF16).astype(F32), U32)
    return jnp.bitwise_or(jnp.right_shift(lo, jnp.uint32(16)), hi)


def _unpack_bf16_halves(w):
    lo = lax.bitcast_convert_type(jnp.left_shift(w, jnp.uint32(16)), F32).astype(BF16)
    hi = lax.bitcast_convert_type(jnp.bitwise_and(w, jnp.uint32(0xFFFF0000)), F32).astype(BF16)
    return lo, hi


def _route(scores, biased):
    e, tm = scores.shape
    epg = EXPERTS_PER_GROUP
    gs = []
    for gi in range(N_GROUPS):
        rows = [biased[gi * epg + r:gi * epg + r + 1, :] for r in range(epg)]
        best = None
        for a in range(epg):
            for b in range(a + 1, epg):
                pair = rows[a] + rows[b]
                best = pair if best is None else jnp.maximum(best, pair)
        gs.append(best)
    gsel = []
    for gi in range(N_GROUPS):
        ok = None
        for go in range(N_GROUPS):
            if go == gi:
                continue
            c = (gs[gi] > gs[go]) if go < gi else (gs[gi] >= gs[go])
            ok = c if ok is None else jnp.logical_and(ok, c)
        gsel.append(ok)
    sel_rows, comb_rows = [], []
    for gi in range(N_GROUPS):
        rows = [biased[gi * epg + r:gi * epg + r + 1, :] for r in range(epg)]
        for r in range(epg):
            rank = jnp.zeros((1, tm), jnp.int32)
            for o in range(epg):
                if o == r:
                    continue
                before = (rows[o] >= rows[r]) if o < r else (rows[o] > rows[r])
                rank = rank + jnp.where(before, 1, 0)
            s = jnp.logical_and(gsel[gi], rank < 2)
            sel_rows.append(jnp.where(s, 1.0, 0.0))
            comb_rows.append(jnp.where(s, scores[gi * epg + r:gi * epg + r + 1, :], 0.0))
    sel = jnp.concatenate(sel_rows, axis=0)
    comb = jnp.concatenate(comb_rows, axis=0)
    comb = comb / jnp.sum(comb, axis=0, keepdims=True)
    return sel, comb


def _ln_router_kernel(x_ref, y_ref, gate_ref, g_ref, b_ref, sc_ref, sh_ref, rwh_ref, rwl_ref, rb_ref,
                      x1_ref, hp_ref, sel_ref, comb_ref):
    z = DN_ALPHA * x_ref[...] + gate_ref[...] * y_ref[...]
    x1 = _layer_norm_rows(z, g_ref[...], b_ref[...])
    x1_ref[...] = x1
    h = x1 * (1.0 + sc_ref[...]) + sh_ref[...]
    hp_ref[...] = _pack_bf16_halves(h)
    h_hi = h.astype(BF16)
    h_lo = (h - h_hi.astype(F32)).astype(BF16)
    logits = _nt_dot(rwh_ref[...], h_hi) + _nt_dot(rwl_ref[...], h_hi) + _nt_dot(rwh_ref[...], h_lo)
    scores = jax.nn.sigmoid(logits)
    sel, comb = _route(scores, scores + rb_ref[...])
    sel_ref[...] = sel
    comb_ref[...] = comb


def ln_router(x2, y, mod, which_gate, which_scale, which_shift, ln_g, ln_b, router_w, router_bias, seq):
    t, d = x2.shape
    tm = _tile(seq, 256)
    e = router_w.shape[1]
    rwt = router_w.T.astype(F32)
    rw_hi = rwt.astype(BF16)
    rw_lo = (rwt - rw_hi.astype(F32)).astype(BF16)

    def vec(which):
        return pl.BlockSpec((None, None, 1, d), lambda i: ((i * tm) // seq, which, 0, 0))

    row = pl.BlockSpec((tm, d), lambda i: (i, 0))
    const = lambda shape: pl.BlockSpec(shape, lambda i: (0, 0))
    return pl.pallas_call(
        _ln_router_kernel,
        grid=(t // tm,),
        in_specs=[row, row, vec(which_gate), const((1, d)), const((1, d)), vec(which_scale), vec(which_shift),
                  const((e, d)), const((e, d)), const((e, 1))],
        out_specs=[row, pl.BlockSpec((tm, d // 2), lambda i: (i, 0)),
                   pl.BlockSpec((e, tm), lambda i: (0, i)), pl.BlockSpec((e, tm), lambda i: (0, i))],
        out_shape=[jax.ShapeDtypeStruct((t, d), F32), jax.ShapeDtypeStruct((t, d // 2), U32),
                   jax.ShapeDtypeStruct((e, t), F32), jax.ShapeDtypeStruct((e, t), F32)],
        compiler_params=_cparams(("arbitrary",)),
        name="ln_router",
    )(x2, y, mod, ln_g.reshape(1, d), ln_b.reshape(1, d), mod, mod, rw_hi, rw_lo,
      router_bias.reshape(e, 1).astype(F32))


DMA_LOOP_UNROLL = 8


def _gather_rows(idx_of_row, src_ref, dst_ref, sem, n_rows, *, wait):
    def step(r, c):
        cp = pltpu.make_async_copy(src_ref.at[pl.ds(idx_of_row(r), 1)], dst_ref.at[pl.ds(r, 1)], sem)
        if wait:
            cp.wait()
        else:
            cp.start()
        return c

    lax.fori_loop(0, n_rows, step, 0, unroll=DMA_LOOP_UNROLL)


def _moe_ffn_kernel(te_ref, nv_ref, idx_ref, nxt_ref, hp_ref, wg_ref, wu_ref, wd_ref, ys_ref,
                    xs_ref, xb_ref, hm_ref, sem):
    i = pl.program_id(0)
    f = pl.program_id(1)
    tm, half = xs_ref.shape[1], xs_ref.shape[2]
    tf = wg_ref.shape[1]
    nff = hm_ref.shape[1] // tf
    rows_per_step = tm // nff

    @pl.when(i < nv_ref[0])
    def _():
        slot = i % 2

        @pl.when(f == 0)
        def _():
            @pl.when(i == 0)
            def _():
                _gather_rows(lambda r: idx_ref[0, 0, r], hp_ref, xs_ref.at[0], sem.at[0], tm, wait=False)

            _gather_rows(lambda r: idx_ref[0, 0, r], hp_ref, xs_ref.at[slot], sem.at[slot], tm, wait=True)
            lo, hi = _unpack_bf16_halves(xs_ref[slot])
            xb_ref[:, :half] = lo
            xb_ref[:, half:] = hi

        base = f * rows_per_step
        for r in range(rows_per_step):
            pltpu.make_async_copy(hp_ref.at[pl.ds(nxt_ref[0, 0, base + r], 1)],
                                  xs_ref.at[1 - slot, pl.ds(base + r, 1)], sem.at[1 - slot]).start()

        xb = xb_ref[...]
        gt = jnp.dot(xb, wg_ref[...], preferred_element_type=F32)
        up = jnp.dot(xb, wu_ref[...], preferred_element_type=F32)
        hm = (gt * jax.nn.sigmoid(gt) * up).astype(BF16)
        for c in range(hm_ref.shape[1] // tf):
            @pl.when(f == c)
            def _(c=c):
                hm_ref[:, c * tf:(c + 1) * tf] = hm

        @pl.when(f == nff - 1)
        def _():
            ys_ref[...] = jnp.dot(hm_ref[...], wd_ref[...], preferred_element_type=F32)

        @pl.when(jnp.logical_and(f == nff - 1, i == nv_ref[0] - 1))
        def _():
            _gather_rows(lambda r: nxt_ref[0, 0, r], hp_ref, xs_ref.at[1 - slot], sem.at[1 - slot], tm, wait=True)

    @pl.when(jnp.logical_and(i >= nv_ref[0], f == 0))
    def _():
        ys_ref[...] = jnp.zeros(ys_ref.shape, ys_ref.dtype)


def moe_ffn(hp, row_token, w_gate, w_up, w_down, tile_expert, n_valid, tm):
    r = row_token.shape[0]
    half = hp.shape[1]
    e, d, ff = w_gate.shape
    tf = _tile(ff, 256)
    nff = ff // tf
    n_tiles = r // tm

    def f_eff(i, f, nv):
        return jnp.where(i < nv[0], f, nff - 1)

    idx_spec = lambda shift: pl.BlockSpec(
        (1, 1, tm), lambda i, f, te, nv: (jnp.minimum(i + shift, n_tiles - 1), 0, 0), memory_space=pltpu.SMEM)
    return pl.pallas_call(
        _moe_ffn_kernel,
        grid_spec=pltpu.PrefetchScalarGridSpec(
            num_scalar_prefetch=2,
            grid=(n_tiles, nff),
            in_specs=[idx_spec(0), idx_spec(1), pl.BlockSpec(memory_space=pl.ANY),
                      pl.BlockSpec((None, d, tf), lambda i, f, te, nv: (te[i], 0, f_eff(i, f, nv))),
                      pl.BlockSpec((None, d, tf), lambda i, f, te, nv: (te[i], 0, f_eff(i, f, nv))),
                      pl.BlockSpec((None, ff, d), lambda i, f, te, nv: (te[i], 0, 0),
                                   pipeline_mode=pl.Buffered(1))],
            out_specs=pl.BlockSpec((tm, d), lambda i, f, te, nv: (i, 0)),
            scratch_shapes=[pltpu.VMEM((2, tm, half), U32), pltpu.VMEM((tm, d), BF16),
                            pltpu.VMEM((tm, ff), BF16), pltpu.SemaphoreType.DMA((2,))],
        ),
        out_shape=jax.ShapeDtypeStruct((r, d), F32),
        compiler_params=_cparams(("arbitrary", "arbitrary")),
        name="moe_ffn",
    )(tile_expert, n_valid, row_token.reshape(n_tiles, 1, tm), row_token.reshape(n_tiles, 1, tm), hp,
      w_gate, w_up, w_down)


def moe_dispatch(sel_t, comb_t, tm):
    e, t = sel_t.shape
    k = 2
    sel = sel_t.T > 0.5
    seli = sel.astype(jnp.int32)
    counts = jnp.sum(seli, axis=0)
    padded = ((counts + tm - 1) // tm) * tm
    ends = jnp.cumsum(padded)
    starts = ends - padded
    pos = jnp.cumsum(seli, axis=0) - seli
    dest_all = starts[None, :] + pos
    nth = jnp.cumsum(seli, axis=1) * seli
    pick = [(nth == n + 1) for n in range(k)]
    dest = jnp.stack([jnp.sum(jnp.where(p, dest_all, 0), axis=1) for p in pick], axis=1)
    wts = jnp.stack([jnp.sum(jnp.where(p, comb_t.T, 0.0), axis=1) for p in pick], axis=1)
    r_max = k * t + e * tm
    flat = dest.reshape(-1)
    row_token = jnp.zeros((r_max,), jnp.int32).at[flat].set(jnp.repeat(jnp.arange(t, dtype=jnp.int32), k))
    n_tiles = r_max // tm
    tile_start = jnp.arange(n_tiles, dtype=jnp.int32) * tm
    n_valid = (ends[-1] // tm).astype(jnp.int32)
    tile_expert = jnp.sum((tile_start[:, None] >= ends[None, :]).astype(jnp.int32), axis=1)
    tile_expert = jnp.minimum(tile_expert, e - 1)
    last_valid_expert = tile_expert[jnp.maximum(n_valid - 1, 0)]
    tile_expert = jnp.where(jnp.arange(n_tiles) < n_valid, tile_expert, last_valid_expert)
    return row_token, tile_expert, n_valid.reshape(1), dest, wts


def _ln_combine_kernel(dest_ref, nxt_ref, x_ref, w_ref, ys_ref, gate_ref, g_ref, b_ref, sc_ref, sh_ref,
                       xo_ref, h_ref, buf_ref, sem, *, tm, emit_h):
    i = pl.program_id(0)
    slot = i % 2

    def gather(idx_ref, to_slot, wait):
        for k in range(2):
            _gather_rows(lambda r, k=k: idx_ref[0, k, r], ys_ref, buf_ref.at[to_slot, k], sem.at[to_slot], tm,
                         wait=wait)

    @pl.when(i == 0)
    def _():
        gather(dest_ref, 0, False)

    gather(dest_ref, slot, True)
    for k in range(2):
        for r in range(tm):
            pltpu.make_async_copy(ys_ref.at[pl.ds(nxt_ref[0, k, r], 1)],
                                  buf_ref.at[1 - slot, k, pl.ds(r, 1)], sem.at[1 - slot]).start()
    w = w_ref[...]
    y = w[:, 0:1] * buf_ref[slot, 0] + w[:, 1:2] * buf_ref[slot, 1]
    z = DN_ALPHA * x_ref[...] + gate_ref[...] * y
    xo = _layer_norm_rows(z, g_ref[...], b_ref[...])
    xo_ref[...] = xo
    if emit_h:
        h_ref[...] = (xo * (1.0 + sc_ref[...]) + sh_ref[...]).astype(h_ref.dtype)
    else:
        h_ref[...] = jnp.zeros(h_ref.shape, h_ref.dtype)

    @pl.when(i == pl.num_programs(0) - 1)
    def _():
        gather(nxt_ref, 1 - slot, True)


def ln_combine(x1, ys, dest, wts, mod, which_gate, ln_g, ln_b, next_mod, seq, emit_h):
    t, d = x1.shape
    tm = _tile(seq, 256)

    def vec(which):
        return pl.BlockSpec((None, None, 1, d), lambda i: ((i * tm) // seq, which, 0, 0))

    row = pl.BlockSpec((tm, d), lambda i: (i, 0))
    const = lambda shape: pl.BlockSpec(shape, lambda i: (0, 0))
    h_rows = tm if emit_h else 8
    h_spec = pl.BlockSpec((h_rows, d), (lambda i: (i, 0)) if emit_h else (lambda i: (0, 0)))
    n_steps = t // tm
    dest_t = dest.reshape(n_steps, tm, 2).transpose(0, 2, 1)
    xo, h = pl.pallas_call(
        functools.partial(_ln_combine_kernel, tm=tm, emit_h=emit_h),
        grid=(n_steps,),
        in_specs=[pl.BlockSpec((1, 2, tm), lambda i: (i, 0, 0), memory_space=pltpu.SMEM),
                  pl.BlockSpec((1, 2, tm), lambda i: (jnp.minimum(i + 1, n_steps - 1), 0, 0),
                               memory_space=pltpu.SMEM),
                  row, pl.BlockSpec((tm, 2), lambda i: (i, 0)), pl.BlockSpec(memory_space=pl.ANY),
                  vec(which_gate), const((1, d)), const((1, d)),
                  pl.BlockSpec((None, None, 1, d), lambda i: ((i * tm) // seq, 1, 0, 0)),
                  pl.BlockSpec((None, None, 1, d), lambda i: ((i * tm) // seq, 0, 0, 0))],
        out_specs=[row, h_spec],
        out_shape=[jax.ShapeDtypeStruct((t, d), F32),
                   jax.ShapeDtypeStruct((t if emit_h else 8, d), BF16)],
        scratch_shapes=[pltpu.VMEM((2, 2, tm, d), F32), pltpu.SemaphoreType.DMA((2,))],
        compiler_params=_cparams(("arbitrary",)),
        name="ln_combine",
    )(dest_t, dest_t, x1, wts, ys, mod, ln_g.reshape(1, d), ln_b.reshape(1, d), next_mod, next_mod)
    return xo, h


MOE_TILE = 512


def _moe_block(x1, hp, sel_t, comb_t, mod, ln_g, ln_b, w_gate, w_up, w_down, next_mod, seq, emit_h):
    tm = min(MOE_TILE, x1.shape[0])
    row_token, tile_expert, n_valid, dest, wts = moe_dispatch(sel_t, comb_t, tm)
    ys = moe_ffn(hp, row_token, w_gate.astype(BF16), w_up.astype(BF16), w_down.astype(BF16),
                 tile_expert, n_valid, tm)
    return ln_combine(x1, ys, dest, wts, mod, 5, ln_g, ln_b, next_mod, seq, emit_h)


def _flags(n, tn, rope_ranges):
    out = []
    for j in range(n // tn):
        c = j * tn
        out.append(1 if any(lo <= c < hi for lo, hi in rope_ranges) else 0)
    return jnp.asarray(out, jnp.int32)


def kernel(x, c, positions, router_w, router_bias, l0_w_ada, l0_b_ada, l0_w_in, l0_lambda_q1, l0_lambda_k1, l0_lambda_q2, l0_lambda_k2, l0_subln_g, l0_w_out, l0_ln1_g, l0_ln1_b, l0_w_gate, l0_w_up, l0_w_down, l0_ln2_g, l0_ln2_b, l1_w_ada, l1_b_ada, l1_w_in, l1_b_in, l1_sinks, l1_w_out, l1_ln1_g, l1_ln1_b, l1_w_gate, l1_w_up, l1_w_down, l1_ln2_g, l1_ln2_b):
    batch, seq, d = x.shape
    t = batch * seq
    a_heads = d // (4 * HEAD_DIM)
    a_width = a_heads * 2 * HEAD_DIM
    b_heads = d // (2 * HEAD_DIM)
    b_width = b_heads * HEAD_DIM
    c_heads = d // HEAD_DIM
    c_kv = c_heads // 4
    assert l0_w_in.shape[1] == 3 * a_width + 3 * b_width
    assert seq % MOBA_BLOCK == 0 and seq % WINDOW == 0

    x2 = x.reshape(t, d)
    tabs = rope_tables(positions)
    mod0 = ada_modulation(c, l0_w_ada, l0_b_ada)
    mod1 = ada_modulation(c, l1_w_ada, l1_b_ada)
    tn = 1024 if d >= 4096 else 256

    h = modulate(x2, mod0, 0, 1, seq)
    n0 = l0_w_in.shape[1]
    rope0 = [(0, 2 * a_width), (3 * a_width, 3 * a_width + 2 * b_width)]
    proj = projection([h], l0_w_in.astype(BF16), jnp.zeros((n0,), F32), _flags(n0, tn, rope0), tabs, BF16,
                      True, tn_pref=tn)
    lam_vecs = jnp.stack([l0_lambda_q1, l0_lambda_k1, l0_lambda_q2, l0_lambda_k2]).astype(F32)
    lam_init = 0.8 - 0.6 * math.exp(-0.3 * 0)
    wa = 2 * HEAD_DIM
    o_a = diff_attention(proj, lam_vecs, l0_subln_g, batch=batch, seq=seq, n_heads=a_heads,
                         q_col=0, k_col=a_width // wa, v_col=2 * a_width // wa, lam_init=lam_init)
    bq0 = 3 * a_width // HEAD_DIM
    nbh = b_width // HEAD_DIM
    kmean = moba_kmean(proj, batch=batch, seq=seq, n_heads=b_heads, k_col=bq0 + nbh)
    o_b = moba_attention(proj, kmean, batch=batch, seq=seq, n_heads=b_heads,
                         q_col=bq0, k_col=bq0 + nbh, v_col=bq0 + 2 * nbh)
    y = projection([o_a, o_b], l0_w_out.astype(BF16), jnp.zeros((d,), F32), _flags(d, tn, []), tabs, F32,
                   False, tn_pref=tn)
    x1, hp, sel_t, comb_t = ln_router(x2, y, mod0, 2, 4, 3, l0_ln1_g, l0_ln1_b, router_w, router_bias, seq)
    x2, h = _moe_block(x1, hp, sel_t, comb_t, mod0, l0_ln2_g, l0_ln2_b, l0_w_gate, l0_w_up, l0_w_down,
                       mod1, seq, True)

    n1 = l1_w_in.shape[1]
    cq = c_heads * HEAD_DIM
    ckv = c_kv * HEAD_DIM
    rope1 = [(0, cq + ckv)]
    proj = projection([h], l1_w_in.astype(BF16), l1_b_in, _flags(n1, tn, rope1), tabs, BF16, True, tn_pref=tn)
    o = swa_attention(proj, l1_sinks, batch=batch, seq=seq, n_q_heads=c_heads, n_kv_heads=c_kv,
                      k_col=cq // HEAD_DIM, v_col=(cq + ckv) // HEAD_DIM)
    y = projection([o], l1_w_out.astype(BF16), jnp.zeros((d,), F32), _flags(d, tn, []), tabs, F32, False,
                   tn_pref=tn)
    x1, hp, sel_t, comb_t = ln_router(x2, y, mod1, 2, 4, 3, l1_ln1_g, l1_ln1_b, router_w, router_bias, seq)
    x2, _ = _moe_block(x1, hp, sel_t, comb_t, mod1, l1_ln2_g, l1_ln2_b, l1_w_gate, l1_w_up, l1_w_down,
                       mod1, seq, False)
    return x2.reshape(batch, seq, d)
```
